```python
import math
import jax, jax.numpy as jnp
from jax import lax
import numpy as np

D_MODEL = 1024
BATCH = 8
SEQ = 2048
DEPTH = 1
DEC_BATCH = 32
DEC_SEQ = 8
PAST_LEN = 16384
PAGE_SIZE = 128

MLA_HEADS = 8
QK_NOPE_DIM = 64
ROPE_DIM = 32
V_HEAD_DIM = 64
Q_RANK = 256
KV_RANK = 256
ROPE_THETA = 10000.0
D_ATTN = MLA_HEADS * V_HEAD_DIM
D_MIX = D_MODEL
D_SSM = D_MIX - D_ATTN
SSM_CH = 16
SSM_GROUPS = D_SSM // SSM_CH
SSM_STATE = 64
D_IN = Q_RANK + KV_RANK + ROPE_DIM + D_SSM
N_MEM = 256
MEM_HEADS = 4
MEM_HEAD_DIM = 128
D_FF = 2816
CONV_W = 3

Q_BLOCK = 128
EPS = 1e-6
ATTN_SCALE = (QK_NOPE_DIM + ROPE_DIM) ** -0.5
MEM_SCALE = MEM_HEAD_DIM ** -0.5

kernel_name = 'hybrid_mla_s5_memxattn_convffn_step'

F32 = jnp.float32


def rmsnorm(x, g):
    xf = x.astype(F32)
    y = xf * lax.rsqrt(jnp.mean(xf * xf, axis=-1, keepdims=True) + EPS)
    return (y * g.astype(F32)).astype(x.dtype)


def rope(x, pos):
    half = ROPE_DIM // 2
    inv = ROPE_THETA ** (-jnp.arange(half, dtype=F32) * (2.0 / ROPE_DIM))
    ang = pos.astype(F32)[:, None] * inv[None, :]
    shape = (ang.shape[0],) + (1,) * (x.ndim - 3) + (half,)
    cos = jnp.cos(ang).reshape(shape)
    sin = jnp.sin(ang).reshape(shape)
    xf = x.astype(F32)
    x1, x2 = xf[..., :half], xf[..., half:]
    return jnp.concatenate([x1 * cos - x2 * sin, x1 * sin + x2 * cos], axis=-1).astype(x.dtype)


def attend(q_lat, q_pe, q_pos, k_lat, k_pe, k_pos):
    b, l, h, r = q_lat.shape
    blk = min(Q_BLOCK, l)
    nb = l // blk
    ql = q_lat.reshape(b, nb, blk, h, r).transpose(1, 0, 2, 3, 4)
    qp = q_pe.reshape(b, nb, blk, h, ROPE_DIM).transpose(1, 0, 2, 3, 4)
    qpos = q_pos.reshape(nb, blk)

    def one_block(args):
        ql_b, qp_b, pos_b = args
        s = (jnp.einsum('bqhr,bkr->bhqk', ql_b, k_lat)
             + jnp.einsum('bqhe,bke->bhqk', qp_b, k_pe)).astype(F32) * ATTN_SCALE
        mask = k_pos[None, :] <= pos_b[:, None]
        s = jnp.where(mask[None, None], s, -jnp.inf)
        pr = jax.nn.softmax(s, axis=-1).astype(k_lat.dtype)
        return jnp.einsum('bhqk,bkr->bqhr', pr, k_lat)

    o = lax.map(one_block, (ql, qp, qpos))
    return o.transpose(1, 0, 2, 3, 4).reshape(b, l, h, r)


def mla_mixer(c_q, c_kv, k_r, pos, p, past):
    b, l, _ = c_q.shape
    c_kv = rmsnorm(c_kv, p['kv_norm'])
    k_r = rope(k_r, pos)
    q = jnp.einsum('blr,rhe->blhe', rmsnorm(c_q, p['q_norm']), p['w_uq'])
    q_nope = q[..., :QK_NOPE_DIM]
    q_pe = rope(q[..., QK_NOPE_DIM:], pos)
    q_lat = jnp.einsum('blhd,rhd->blhr', q_nope, p['w_uk'])
    if past is None:
        keys_lat, keys_pe = c_kv, k_r
    else:
        keys_lat = jnp.concatenate([past[0], c_kv], axis=1)
        keys_pe = jnp.concatenate([past[1], k_r], axis=1)
    k_pos = jnp.arange(keys_lat.shape[1], dtype=jnp.int32)
    o_lat = attend(q_lat, q_pe, pos, keys_lat, keys_pe, k_pos)
    out = jnp.einsum('blhr,rhv->blhv', o_lat, p['w_uv']).reshape(b, l, D_ATTN)
    return out, c_kv, k_r


def _cmul_combine(e1, e2):
    a1r, a1i, b1r, b1i = e1
    a2r, a2i, b2r, b2i = e2
    ar = a2r * a1r - a2i * a1i
    ai = a2r * a1i + a2i * a1r
    br = a2r * b1r - a2i * b1i + b2r
    bi = a2r * b1i + a2i * b1r + b2i
    return ar, ai, br, bi


def s5_mixer(u, p, h0):
    b, l, _ = u.shape
    uf = u.astype(F32)
    ug = uf.reshape(b, l, SSM_GROUPS, SSM_CH)
    a_re = p['ssm_a_re'].astype(F32)
    a_im = p['ssm_a_im'].astype(F32)
    dt = jnp.exp(p['ssm_log_dt'].astype(F32))[:, None]
    mag = jnp.exp(dt * a_re)
    abr = mag * jnp.cos(dt * a_im)
    abi = mag * jnp.sin(dt * a_im)
    den = a_re * a_re + a_im * a_im
    nr, ni = abr - 1.0, abi
    fr = (nr * a_re + ni * a_im) / den
    fi = (ni * a_re - nr * a_im) / den
    b_re = p['ssm_b_re'].astype(F32)
    b_im = p['ssm_b_im'].astype(F32)
    bbr = fr[..., None] * b_re - fi[..., None] * b_im
    bbi = fr[..., None] * b_im + fi[..., None] * b_re
    bur = jnp.einsum('blgc,gnc->blgn', ug, bbr)
    bui = jnp.einsum('blgc,gnc->blgn', ug, bbi)
    if h0 is not None:
        h0r = h0[0].astype(F32)
        h0i = h0[1].astype(F32)
        bur = bur.at[:, 0].add(abr * h0r - abi * h0i)
        bui = bui.at[:, 0].add(abr * h0i + abi * h0r)
    a_r = jnp.broadcast_to(abr, bur.shape)
    a_i = jnp.broadcast_to(abi, bui.shape)
    _, _, hr, hi = lax.associative_scan(_cmul_combine, (a_r, a_i, bur, bui), axis=1)
    y = (jnp.einsum('blgn,gcn->blgc', hr, p['ssm_c_re'].astype(F32))
         - jnp.einsum('blgn,gcn->blgc', hi, p['ssm_c_im'].astype(F32)))
    y = y.reshape(b, l, D_SSM) + p['ssm_d'].astype(F32) * uf
    g = jax.nn.gelu(y)
    out = g * jax.nn.sigmoid(g @ p['ssm_w_glu'].astype(F32))
    return out.astype(u.dtype), hr[:, -1], hi[:, -1]


def mem_kv(mem, p):
    m = rmsnorm(mem, p['mem_norm'])
    b, n, _ = m.shape
    k = (m @ p['w_k_mem']).reshape(b, n, MEM_HEADS, MEM_HEAD_DIM)
    v = (m @ p['w_v_mem']).reshape(b, n, MEM_HEADS, MEM_HEAD_DIM)
    return k, v


def mem_attend(h, mk, mv, p):
    b, l, _ = h.shape
    q = (h @ p['w_q_mem']).reshape(b, l, MEM_HEADS, MEM_HEAD_DIM)
    s = jnp.einsum('blhd,bmhd->bhlm', q, mk).astype(F32) * MEM_SCALE
    pr = jax.nn.softmax(s, axis=-1).astype(mv.dtype)
    o = jnp.einsum('bhlm,bmhd->blhd', pr, mv).reshape(b, l, MEM_HEADS * MEM_HEAD_DIM)
    return o @ p['w_o_mem']


def conv_ffn(h, p, conv_prev):
    b, l, _ = h.shape
    g = h @ p['w_gate']
    up = h @ p['w_up']
    if conv_prev is None:
        conv_prev = jnp.zeros((b, CONV_W - 1, D_FF), g.dtype)
    gp = jnp.concatenate([conv_prev.astype(g.dtype), g], axis=1)
    w = p['ffn_conv_w']
    gc = p['ffn_conv_b'] + sum(w[k] * gp[:, k:k + l] for k in range(CONV_W))
    out = (jax.nn.silu(gc) * up) @ p['w_down']
    return out, gp[:, -(CONV_W - 1):]


def layer(x, pos, mk, mv, p, past_kv, ssm_h0, conv_prev):
    h = rmsnorm(x, p['norm_mix_pre'])
    z = h @ p['w_in']
    o1 = Q_RANK
    o2 = o1 + KV_RANK
    o3 = o2 + ROPE_DIM
    c_q, c_kv, k_r, u = z[..., :o1], z[..., o1:o2], z[..., o2:o3], z[..., o3:]
    attn_out, c_kv_n, k_r_n = mla_mixer(c_q, c_kv, k_r, pos, p, past_kv)
    ssm_out, hr, hi = s5_mixer(u, p, ssm_h0)
    mix = jnp.concatenate([rmsnorm(attn_out, p['norm_attn_out']),
                           rmsnorm(ssm_out, p['norm_ssm_out'])], axis=-1) @ p['w_out']
    x = x + rmsnorm(mix, p['norm_mix_post'])
    h = rmsnorm(x, p['norm_mem_pre'])
    x = x + rmsnorm(mem_attend(h, mk, mv, p), p['norm_mem_post'])
    h = rmsnorm(x, p['norm_ffn_pre'])
    f, conv_new = conv_ffn(h, p, conv_prev)
    x = x + rmsnorm(f, p['norm_ffn_post'])
    return x, c_kv_n, k_r_n, hr, hi, conv_new


def setup_inputs(seed: int = 0) -> dict:
    key = jax.random.key(seed)
    ks = iter(jax.random.split(key, 64))

    def nrm(shape, scale=1.0):
        return scale * jax.random.normal(next(ks), shape, F32)

    def gain(n):
        return 1.0 + 0.05 * nrm((DEPTH, n))

    n_pages = PAST_LEN // PAGE_SIZE
    n_used = DEC_BATCH * n_pages
    n_pool = n_used + n_used // 4
    perm = jax.random.permutation(next(ks), n_pool)[:n_used]
    page_table = perm.reshape(DEC_BATCH, n_pages).astype(jnp.int32)
    n_idx = jnp.arange(SSM_STATE, dtype=F32)
    G, N, C = SSM_GROUPS, SSM_STATE, SSM_CH
    return {
        'x_prompt': nrm((BATCH, SEQ, D_MODEL)),
        'x_sample': nrm((DEC_BATCH, DEC_SEQ, D_MODEL)),
        'mem_prompt': nrm((BATCH, N_MEM, D_MODEL)),
        'cache_kv_latent': nrm((DEPTH, n_pool, PAGE_SIZE, KV_RANK)),
        'cache_k_rope': nrm((DEPTH, n_pool, PAGE_SIZE, ROPE_DIM)),
        'page_table': page_table,
        'state_ssm_re': nrm((DEPTH, DEC_BATCH, G, N), 0.3),
        'state_ssm_im': nrm((DEPTH, DEC_BATCH, G, N), 0.3),
        'state_ffn_conv': nrm((DEPTH, DEC_BATCH, CONV_W - 1, D_FF)),
        'cache_mem_k': nrm((DEPTH, DEC_BATCH, N_MEM, MEM_HEADS, MEM_HEAD_DIM)),
        'cache_mem_v': nrm((DEPTH, DEC_BATCH, N_MEM, MEM_HEADS, MEM_HEAD_DIM)),
        'norm_mix_pre': gain(D_MODEL),
        'w_in': nrm((DEPTH, D_MODEL, D_IN), D_MODEL ** -0.5),
        'q_norm': gain(Q_RANK),
        'kv_norm': gain(KV_RANK),
        'w_uq': nrm((DEPTH, Q_RANK, MLA_HEADS, QK_NOPE_DIM + ROPE_DIM), Q_RANK ** -0.5),
        'w_uk': nrm((DEPTH, KV_RANK, MLA_HEADS, QK_NOPE_DIM), KV_RANK ** -0.5),
        'w_uv': nrm((DEPTH, KV_RANK, MLA_HEADS, V_HEAD_DIM), KV_RANK ** -0.5),
        'ssm_a_re': -0.5 + 0.01 * nrm((DEPTH, G, N)),
        'ssm_a_im': math.pi * n_idx + 0.01 * nrm((DEPTH, G, N)),
        'ssm_log_dt': jax.random.uniform(next(ks), (DEPTH, G), F32, math.log(1e-3), math.log(1e-1)),
        'ssm_b_re': nrm((DEPTH, G, N, C), C ** -0.5),
        'ssm_b_im': nrm((DEPTH, G, N, C), C ** -0.5),
        'ssm_c_re': nrm((DEPTH, G, C, N), N ** -0.5),
        'ssm_c_im': nrm((DEPTH, G, C, N), N ** -0.5),
        'ssm_d': nrm((DEPTH, D_SSM)),
        'ssm_w_glu': nrm((DEPTH, D_SSM, D_SSM), D_SSM ** -0.5),
        'norm_attn_out': gain(D_ATTN),
        'norm_ssm_out': gain(D_SSM),
        'w_out': nrm((DEPTH, D_MIX, D_MODEL), D_MIX ** -0.5),
        'norm_mix_post': gain(D_MODEL),
        'norm_mem_pre': gain(D_MODEL),
        'mem_norm': gain(D_MODEL),
        'w_q_mem': nrm((DEPTH, D_MODEL, MEM_HEADS * MEM_HEAD_DIM), D_MODEL ** -0.5),
        'w_k_mem': nrm((DEPTH, D_MODEL, MEM_HEADS * MEM_HEAD_DIM), D_MODEL ** -0.5),
        'w_v_mem': nrm((DEPTH, D_MODEL, MEM_HEADS * MEM_HEAD_DIM), D_MODEL ** -0.5),
        'w_o_mem': nrm((DEPTH, MEM_HEADS * MEM_HEAD_DIM, D_MODEL), (MEM_HEADS * MEM_HEAD_DIM) ** -0.5),
        'norm_mem_post': gain(D_MODEL),
        'norm_ffn_pre': gain(D_MODEL),
        'w_gate': nrm((DEPTH, D_MODEL, D_FF), D_MODEL ** -0.5),
        'w_up': nrm((DEPTH, D_MODEL, D_FF), D_MODEL ** -0.5),
        'ffn_conv_w': nrm((DEPTH, CONV_W, D_FF), CONV_W ** -0.5),
        'ffn_conv_b': nrm((DEPTH, D_FF), 0.02),
        'w_down': nrm((DEPTH, D_FF, D_MODEL), D_FF ** -0.5),
        'norm_ffn_post': gain(D_MODEL),
    }


def reference(x_prompt, x_sample, mem_prompt, cache_kv_latent, cache_k_rope, page_table,
              state_ssm_re, state_ssm_im, state_ffn_conv, cache_mem_k, cache_mem_v,
              norm_mix_pre, w_in, q_norm, kv_norm, w_uq, w_uk, w_uv,
              ssm_a_re, ssm_a_im, ssm_log_dt, ssm_b_re, ssm_b_im, ssm_c_re, ssm_c_im,
              ssm_d, ssm_w_glu, norm_attn_out, norm_ssm_out, w_out, norm_mix_post,
              norm_mem_pre, mem_norm, w_q_mem, w_k_mem, w_v_mem, w_o_mem, norm_mem_post,
              norm_ffn_pre, w_gate, w_up, ffn_conv_w, ffn_conv_b, w_down, norm_ffn_post):
    weights = dict(
        norm_mix_pre=norm_mix_pre, w_in=w_in, q_norm=q_norm, kv_norm=kv_norm,
        w_uq=w_uq, w_uk=w_uk, w_uv=w_uv, ssm_a_re=ssm_a_re, ssm_a_im=ssm_a_im,
        ssm_log_dt=ssm_log_dt, ssm_b_re=ssm_b_re, ssm_b_im=ssm_b_im, ssm_c_re=ssm_c_re,
        ssm_c_im=ssm_c_im, ssm_d=ssm_d, ssm_w_glu=ssm_w_glu, norm_attn_out=norm_attn_out,
        norm_ssm_out=norm_ssm_out, w_out=w_out, norm_mix_post=norm_mix_post,
        norm_mem_pre=norm_mem_pre, mem_norm=mem_norm, w_q_mem=w_q_mem, w_k_mem=w_k_mem,
        w_v_mem=w_v_mem, w_o_mem=w_o_mem, norm_mem_post=norm_mem_post,
        norm_ffn_pre=norm_ffn_pre, w_gate=w_gate, w_up=w_up, ffn_conv_w=ffn_conv_w,
        ffn_conv_b=ffn_conv_b, w_down=w_down, norm_ffn_post=norm_ffn_post)
    db = x_sample.shape[0]
    past_len = page_table.shape[1] * cache_kv_latent.shape[2]
    pos_p = jnp.arange(x_prompt.shape[1], dtype=jnp.int32)
    pos_s = past_len + jnp.arange(x_sample.shape[1], dtype=jnp.int32)

    xp, xs = x_prompt, x_sample
    p_kv, p_kr, p_sr, p_si, p_cv, p_mk, p_mv = [], [], [], [], [], [], []
    s_kv, s_kr, s_sr, s_si, s_cv = [], [], [], [], []
    for l in range(DEPTH):
        p = {name: w[l] for name, w in weights.items()}
        mk, mv = mem_kv(mem_prompt, p)
        xp, ckv, kr, hr, hi, cv = layer(xp, pos_p, mk, mv, p, None, None, None)
        p_kv.append(ckv); p_kr.append(kr); p_sr.append(hr); p_si.append(hi)
        p_cv.append(cv); p_mk.append(mk); p_mv.append(mv)
        past_lat = jnp.take(cache_kv_latent[l], page_table, axis=0).reshape(db, past_len, KV_RANK)
        past_pe = jnp.take(cache_k_rope[l], page_table, axis=0).reshape(db, past_len, ROPE_DIM)
        xs, ckv, kr, hr, hi, cv = layer(
            xs, pos_s, cache_mem_k[l], cache_mem_v[l], p, (past_lat, past_pe),
            (state_ssm_re[l], state_ssm_im[l]), state_ffn_conv[l])
        s_kv.append(ckv); s_kr.append(kr); s_sr.append(hr); s_si.append(hi); s_cv.append(cv)

    return (xp, xs,
            jnp.stack(p_kv), jnp.stack(p_kr), jnp.stack(p_sr), jnp.stack(p_si),
            jnp.stack(p_cv), jnp.stack(p_mk), jnp.stack(p_mv),
            jnp.stack(s_kv), jnp.stack(s_kr), jnp.stack(s_sr), jnp.stack(s_si),
            jnp.stack(s_cv))
```

```python
import functools
import math

import jax
import jax.numpy as jnp
from jax import lax
from jax.experimental import pallas as pl
from jax.experimental.pallas import tpu as pltpu

F32 = jnp.float32
BF16 = jnp.bfloat16

EPS = 1e-6
ROPE_THETA = 10000.0
LANES = 128
NEG_BIG = -1e30
LOG2E = 1.4426950408889634
VMEM_LIMIT = 56 * 1024 * 1024


def _rms(x, g):
    y = x * lax.rsqrt(jnp.mean(x * x, axis=-1, keepdims=True) + EPS)
    return y * g


def _dot(a, b):
    return jnp.dot(a, b, preferred_element_type=F32)


def _dot_nt(a, b):
    return lax.dot_general(a, b, (((1,), (1,)), ((), ())), preferred_element_type=F32)


def _rep_lanes(x, n):
    return jnp.concatenate([x] * n, axis=1) if n > 1 else x


def _const_spec(shape):
    nd = len(shape)
    return pl.BlockSpec(shape, lambda *_: (0,) * nd, pipeline_mode=pl.Buffered(1))


def _params(*sem):
    return pltpu.CompilerParams(dimension_semantics=sem, vmem_limit_bytes=VMEM_LIMIT)


def _pre_body(x_ref, tab_ref, gpre_ref, gq_ref, gkv_ref, win_ref, wq_ref, wkv_ref,
              q_ref, k_ref, v_ref, ckv_ref, kr_ref, u_ref, *, n_heads, q_rank, kv_rank, d_ssm, rope_dim):
    x = x_ref[...]
    h = _rms(x, gpre_ref[...]).astype(BF16)
    z = _dot(h, win_ref[...])
    o1 = q_rank
    o2 = o1 + kv_rank
    o3 = o2 + d_ssm
    o4 = o3 + LANES
    cq, ckv, u = z[:, :o1], z[:, o1:o2], z[:, o2:o3]
    k1, k2 = z[:, o3:o4], z[:, o4:o4 + LANES]
    krs = k1 * tab_ref[0] + k2 * tab_ref[1]
    kr_ref[...] = krs[:, :rope_dim]
    ckv_n = _rms(ckv, gkv_ref[...])
    ckv_ref[...] = ckv_n
    kv2 = _dot(ckv_n.astype(BF16), wkv_ref[...])
    nk = n_heads * LANES
    k_ref[...] = (kv2[:, :nk] + _rep_lanes(krs, n_heads)).astype(k_ref.dtype)
    v_ref[...] = kv2[:, nk:].astype(v_ref.dtype)
    qn = _rms(cq, gq_ref[...]).astype(BF16)
    qq = _dot(qn, wq_ref[...])
    q = qq[:, :nk] * _rep_lanes(tab_ref[2], n_heads) + qq[:, nk:] * _rep_lanes(tab_ref[3], n_heads)
    q_ref[...] = q.astype(q_ref.dtype)
    u_ref[...] = u


def _pre_call(x2, tab, gpre, gq, gkv, win, wq, wkv, *, tm, n_heads, q_rank, kv_rank, d_ssm, rope_dim,
              q_dtype, v_width):
    t, d = x2.shape
    ntab = tab.shape[1] // tm
    nk = n_heads * LANES
    row = lambda w: pl.BlockSpec((tm, w), lambda i: (i, 0))
    body = functools.partial(_pre_body, n_heads=n_heads, q_rank=q_rank, kv_rank=kv_rank, d_ssm=d_ssm,
                             rope_dim=rope_dim)
    return pl.pallas_call(
        body,
        grid=(t // tm,),
        in_specs=[row(d),
                  pl.BlockSpec((4, tm, LANES), lambda i: (0, i % ntab, 0)),
                  _const_spec(gpre.shape), _const_spec(gq.shape), _const_spec(gkv.shape),
                  _const_spec(win.shape), _const_spec(wq.shape), _const_spec(wkv.shape)],
        out_specs=[row(nk), row(nk), row(v_width), row(kv_rank), row(rope_dim), row(d_ssm)],
        out_shape=[jax.ShapeDtypeStruct((t, nk), q_dtype),
                   jax.ShapeDtypeStruct((t, nk), BF16),
                   jax.ShapeDtypeStruct((t, v_width), BF16),
                   jax.ShapeDtypeStruct((t, kv_rank), F32),
                   jax.ShapeDtypeStruct((t, rope_dim), F32),
                   jax.ShapeDtypeStruct((t, d_ssm), F32)],
        compiler_params=_params("parallel"),
        name="pre_proj",
    )(x2, tab, gpre, gq, gkv, win, wq, wkv)


def _attn_body(q_ref, k_ref, v_ref, o_ref, m_ref, l_ref, acc_ref, *, n_heads, tq, tk, dv):
    i = pl.program_id(1)
    j = pl.program_id(2)

    @pl.when(j == 0)
    def _init():
        m_ref[...] = jnp.full(m_ref.shape, NEG_BIG, F32)
        l_ref[...] = jnp.zeros(l_ref.shape, F32)
        acc_ref[...] = jnp.zeros(acc_ref.shape, F32)

    def step(masked):
        for h in range(n_heads):
            qh = q_ref[:, h * LANES:(h + 1) * LANES]
            kh = k_ref[:, h * LANES:(h + 1) * LANES]
            vh = v_ref[:, h * dv:(h + 1) * dv]
            s = _dot_nt(qh, kh)
            if masked:
                row = lax.broadcasted_iota(jnp.int32, (tq, tk), 0)
                col = lax.broadcasted_iota(jnp.int32, (tq, tk), 1)
                s = jnp.where(col <= row, s, NEG_BIG)
            m_prev = m_ref[h]
            m_next = jnp.maximum(m_prev, jnp.max(s, axis=1, keepdims=True))
            alpha = jnp.exp2(m_prev - m_next)
            p = jnp.exp2(s - _rep_lanes(m_next, tk // LANES))
            l_ref[h] = alpha * l_ref[h] + jnp.sum(p, axis=1, keepdims=True)
            m_ref[h] = m_next
            acc_ref[h] = acc_ref[h] * alpha[:, :dv] + _dot(p.astype(BF16), vh)

    @pl.when(j < i)
    def _off_diag():
        step(False)

    @pl.when(j == i)
    def _diag():
        step(True)
        for h in range(n_heads):
            o_ref[:, h * dv:(h + 1) * dv] = acc_ref[h] / l_ref[h][:, :dv]


def _attn_call(q, k, v, *, n_heads, dv, tq):
    b, l, nk = q.shape
    tk = tq
    body = functools.partial(_attn_body, n_heads=n_heads, tq=tq, tk=tk, dv=dv)
    return pl.pallas_call(
        body,
        grid=(b, l // tq, l // tk),
        in_specs=[pl.BlockSpec((None, tq, nk), lambda bb, i, j: (bb, i, 0)),
                  pl.BlockSpec((None, tk, nk), lambda bb, i, j: (bb, jnp.minimum(i, j), 0)),
                  pl.BlockSpec((None, tk, n_heads * dv), lambda bb, i, j: (bb, jnp.minimum(i, j), 0))],
        out_specs=pl.BlockSpec((None, tq, n_heads * dv), lambda bb, i, j: (bb, i, 0)),
        out_shape=jax.ShapeDtypeStruct((b, l, n_heads * dv), F32),
        scratch_shapes=[pltpu.VMEM((n_heads, tq, LANES), F32),
                        pltpu.VMEM((n_heads, tq, LANES), F32),
                        pltpu.VMEM((n_heads, tq, dv), F32)],
        compiler_params=_params("parallel", "parallel", "arbitrary"),
        name="mla_prompt_attn",
    )(q, k, v)


def _dec_attn_body(pt_ref, q_ref, ckv_ref, kr_ref, wukt_ref, wuvp_ref, *rest,
                   n_heads, s_len, rope_dim, pages):
    lat_refs = rest[:pages]
    pe_refs = rest[pages:2 * pages]
    o_ref, ql_s, qp_s, m_s, l_s, acc_s = rest[2 * pages:]
    j = pl.program_id(1)
    rows = n_heads * s_len

    @pl.when(j == 0)
    def _init():
        for h in range(n_heads):
            qs = q_ref[:, h * LANES:(h + 1) * LANES]
            ql_s[h * s_len:(h + 1) * s_len, :] = _dot(qs.astype(BF16), wukt_ref[h])
            qp_s[h * s_len:(h + 1) * s_len, :] = qs[:, :rope_dim]
        m_s[...] = jnp.full(m_s.shape, NEG_BIG, F32)
        l_s[...] = jnp.zeros(l_s.shape, F32)
        acc_s[...] = jnp.zeros(acc_s.shape, F32)

    ql = ql_s[...].astype(BF16)
    qp = qp_s[...].astype(BF16)

    def update(s, vals):
        m_prev = m_s[...]
        m_next = jnp.maximum(m_prev, jnp.max(s, axis=1, keepdims=True))
        alpha = jnp.exp2(m_prev - m_next)
        p = jnp.exp2(s - m_next)
        l_s[...] = alpha * l_s[...] + jnp.sum(p, axis=1, keepdims=True)
        m_s[...] = m_next
        acc_s[...] = acc_s[...] * alpha + _dot(p.astype(BF16), vals)

    lat = jnp.concatenate([r[...].astype(BF16) for r in lat_refs], axis=0)
    pe = jnp.concatenate([r[...].astype(BF16) for r in pe_refs], axis=0)
    update(_dot_nt(ql, lat) + _dot_nt(qp, pe), lat)

    @pl.when(j == pl.num_programs(1) - 1)
    def _finish():
        pad = LANES - s_len
        new_lat = jnp.concatenate([ckv_ref[...], jnp.zeros((pad, ckv_ref.shape[1]), F32)], axis=0).astype(BF16)
        new_pe = jnp.concatenate([kr_ref[...], jnp.zeros((pad, rope_dim), F32)], axis=0).astype(BF16)
        s = _dot_nt(ql, new_lat) + _dot_nt(qp, new_pe)
        row = lax.broadcasted_iota(jnp.int32, (rows, LANES), 0)
        col = lax.broadcasted_iota(jnp.int32, (rows, LANES), 1)
        s = jnp.where(col <= row % s_len, s, NEG_BIG)
        update(s, new_lat)
        o_lat = (acc_s[...] / l_s[...]).astype(BF16)
        out = _dot(o_lat[0:s_len], wuvp_ref[0])
        for h in range(1, n_heads):
            out = out + _dot(o_lat[h * s_len:(h + 1) * s_len], wuvp_ref[h])
        o_ref[...] = out


def _dec_attn_call(page_table, q, ckv, kr, wukt, wuvp, cache_lat, cache_pe, *, layer, n_heads, dv, pages):
    db, s_len, nk = q.shape
    kv_rank = ckv.shape[-1]
    rope_dim = kr.shape[-1]
    n_pages = page_table.shape[1]
    page = cache_lat.shape[2]
    assert n_pages % pages == 0
    rows = n_heads * s_len
    body = functools.partial(_dec_attn_body, n_heads=n_heads, s_len=s_len, rope_dim=rope_dim, pages=pages)

    def page_spec(width, r):
        return pl.BlockSpec((None, None, page, width),
                            lambda b, j, pt: (layer, pt[b, j * pages + r], 0, 0))

    grid_spec = pltpu.PrefetchScalarGridSpec(
        num_scalar_prefetch=1,
        grid=(db, n_pages // pages),
        in_specs=[pl.BlockSpec((None, s_len, nk), lambda b, j, pt: (b, 0, 0)),
                  pl.BlockSpec((None, s_len, kv_rank), lambda b, j, pt: (b, 0, 0)),
                  pl.BlockSpec((None, s_len, rope_dim), lambda b, j, pt: (b, 0, 0)),
                  _const_spec(wukt.shape), _const_spec(wuvp.shape)]
                 + [page_spec(kv_rank, r) for r in range(pages)]
                 + [page_spec(rope_dim, r) for r in range(pages)],
        out_specs=pl.BlockSpec((None, s_len, n_heads * dv), lambda b, j, pt: (b, 0, 0)),
        scratch_shapes=[pltpu.VMEM((rows, kv_rank), F32),
                        pltpu.VMEM((rows, rope_dim), F32),
                        pltpu.VMEM((rows, 1), F32),
                        pltpu.VMEM((rows, 1), F32),
                        pltpu.VMEM((rows, kv_rank), F32)])
    return pl.pallas_call(
        body,
        grid_spec=grid_spec,
        out_shape=jax.ShapeDtypeStruct((db, s_len, n_heads * dv), F32),
        compiler_params=_params("parallel", "arbitrary"),
        name="mla_sample_attn",
    )(page_table, q, ckv, kr, wukt, wuvp, *([cache_lat] * pages), *([cache_pe] * pages))


def _s5_disc_body(are_ref, aim_ref, ldt_ref, bre_ref, bim_ref, abr_ref, abi_ref, bbr_ref, bbi_ref):
    a_re = are_ref[...]
    a_im = aim_ref[...]
    dt = jnp.exp(ldt_ref[...])
    mag = jnp.exp(dt * a_re)
    abr = mag * jnp.cos(dt * a_im)
    abi = mag * jnp.sin(dt * a_im)
    den = a_re * a_re + a_im * a_im
    nr, ni = abr - 1.0, abi
    fr = (nr * a_re + ni * a_im) / den
    fi = (ni * a_re - nr * a_im) / den
    abr_ref[...] = abr
    abi_ref[...] = abi
    b_re = bre_ref[...]
    b_im = bim_ref[...]
    bbr_ref[...] = fr[:, None, :] * b_re - fi[:, None, :] * b_im
    bbi_ref[...] = fr[:, None, :] * b_im + fi[:, None, :] * b_re


def _s5_disc_call(a_re, a_im, log_dt, b_re_t, b_im_t):
    g, n = a_re.shape
    c = b_re_t.shape[1]
    return pl.pallas_call(
        _s5_disc_body,
        out_shape=[jax.ShapeDtypeStruct((g, n), F32), jax.ShapeDtypeStruct((g, n), F32),
                   jax.ShapeDtypeStruct((g, c, n), F32), jax.ShapeDtypeStruct((g, c, n), F32)],
        name="s5_discretise",
    )(a_re, a_im, log_dt.reshape(g, 1), b_re_t, b_im_t)


def _s5_body(u_ref, h0r_ref, h0i_ref, ar_ref, ai_ref, bbr_ref, bbi_ref, ccr_ref, cci_ref, d_ref, wglu_ref,
             y_ref, hr_out, hi_out, bur, bui, hr_s, hi_s, *, nb, tc, lane_tiles, unroll):
    c = pl.program_id(0)
    d_ssm = u_ref.shape[-1]
    n_state = hr_s.shape[-1]
    halves = bbr_ref.shape[0]
    ch_half = d_ssm // halves
    st_half = n_state // halves

    @pl.when(c == 0)
    def _init():
        hr_s[...] = h0r_ref[...]
        hi_s[...] = h0i_ref[...]

    u = u_ref[...].reshape(nb * tc, d_ssm)
    ub = u.astype(BF16)
    tiles_half = st_half // LANES
    for hf in range(halves):
        ublk = ub[:, hf * ch_half:(hf + 1) * ch_half]
        br = _dot(ublk, bbr_ref[hf])
        bi = _dot(ublk, bbi_ref[hf])
        for k in range(tiles_half):
            bur[hf * tiles_half + k] = br[:, k * LANES:(k + 1) * LANES]
            bui[hf * tiles_half + k] = bi[:, k * LANES:(k + 1) * LANES]

    for k0 in range(0, n_state // LANES, lane_tiles):
        tiles = range(k0, k0 + lane_tiles)
        ar = [jnp.broadcast_to(ar_ref[:, k * LANES:(k + 1) * LANES], (nb, LANES)) for k in tiles]
        ai = [jnp.broadcast_to(ai_ref[:, k * LANES:(k + 1) * LANES], (nb, LANES)) for k in tiles]

        def body(t, carry, tiles=tiles, ar=ar, ai=ai):
            rows = pl.ds(t, nb, stride=tc)
            out = []
            for n, k in enumerate(tiles):
                hr, hi = carry[2 * n], carry[2 * n + 1]
                nr = ar[n] * hr - ai[n] * hi + bur[k, rows, :]
                ni = ar[n] * hi + ai[n] * hr + bui[k, rows, :]
                bur[k, rows, :] = nr
                bui[k, rows, :] = ni
                out += [nr, ni]
            return tuple(out)

        init = []
        for k in tiles:
            init += [hr_s[:, k * LANES:(k + 1) * LANES], hi_s[:, k * LANES:(k + 1) * LANES]]
        fin = lax.fori_loop(0, tc, body, tuple(init), unroll=unroll)
        for n, k in enumerate(tiles):
            hr_s[:, k * LANES:(k + 1) * LANES] = fin[2 * n]
            hi_s[:, k * LANES:(k + 1) * LANES] = fin[2 * n + 1]

    hr_out[...] = hr_s[...]
    hi_out[...] = hi_s[...]

    ys = []
    for hf in range(halves):
        hr_hist = jnp.concatenate([bur[hf * tiles_half + k] for k in range(tiles_half)], axis=1).astype(BF16)
        hi_hist = jnp.concatenate([bui[hf * tiles_half + k] for k in range(tiles_half)], axis=1).astype(BF16)
        ys.append(_dot(hr_hist, ccr_ref[hf]) + _dot(hi_hist, cci_ref[hf]))
    y = jnp.concatenate(ys, axis=1) + d_ref[...] * u
    g = jax.nn.gelu(y)
    out = g * jax.nn.sigmoid(_dot(g.astype(BF16), wglu_ref[...]))
    y_ref[...] = out.reshape(nb, tc, d_ssm)


def _s5_call(u3, h0r, h0i, ar, ai, bbr, bbi, ccr, cci, d, wglu, *, tc):
    nb, l, d_ssm = u3.shape
    n_state = ar.shape[-1]
    body = functools.partial(_s5_body, nb=nb, tc=tc, lane_tiles=4, unroll=min(tc, 8))
    return pl.pallas_call(
        body,
        grid=(l // tc,),
        in_specs=[pl.BlockSpec((nb, tc, d_ssm), lambda c: (0, c, 0)),
                  _const_spec(h0r.shape), _const_spec(h0i.shape),
                  _const_spec(ar.shape), _const_spec(ai.shape),
                  _const_spec(bbr.shape), _const_spec(bbi.shape),
                  _const_spec(ccr.shape), _const_spec(cci.shape),
                  _const_spec(d.shape), _const_spec(wglu.shape)],
        out_specs=[pl.BlockSpec((nb, tc, d_ssm), lambda c: (0, c, 0)),
                   pl.BlockSpec((nb, n_state), lambda c: (0, 0)),
                   pl.BlockSpec((nb, n_state), lambda c: (0, 0))],
        out_shape=[jax.ShapeDtypeStruct((nb, l, d_ssm), F32),
                   jax.ShapeDtypeStruct((nb, n_state), F32),
                   jax.ShapeDtypeStruct((nb, n_state), F32)],
        scratch_shapes=[pltpu.VMEM((n_state // LANES, nb * tc, LANES), F32),
                        pltpu.VMEM((n_state // LANES, nb * tc, LANES), F32),
                        pltpu.VMEM((nb, n_state), F32), pltpu.VMEM((nb, n_state), F32)],
        compiler_params=_params("arbitrary"),
        name="s5_scan_glu",
    )(u3, h0r, h0i, ar, ai, bbr, bbi, ccr, cci, d, wglu)


def _memkv_body(m_ref, g_ref, wk_ref, wv_ref, k_ref, v_ref):
    m = _rms(m_ref[...], g_ref[...]).astype(BF16)
    k_ref[...] = _dot(m, wk_ref[...])
    v_ref[...] = _dot(m, wv_ref[...])


def _memkv_call(mem2, g, wk, wv, *, tm):
    t, d = mem2.shape
    n = wk.shape[1]
    return pl.pallas_call(
        _memkv_body,
        grid=(t // tm,),
        in_specs=[pl.BlockSpec((tm, d), lambda i: (i, 0)),
                  _const_spec(g.shape), _const_spec(wk.shape), _const_spec(wv.shape)],
        out_specs=[pl.BlockSpec((tm, n), lambda i: (i, 0)), pl.BlockSpec((tm, n), lambda i: (i, 0))],
        out_shape=[jax.ShapeDtypeStruct((t, n), F32), jax.ShapeDtypeStruct((t, n), F32)],
        compiler_params=_params("parallel"),
        name="mem_kv",
    )(mem2, g, wk, wv)


def _mix_mem_body(x_ref, a_ref, s_ref, mk_ref, mv_ref, gao_ref, gso_ref, woa_ref, wos_ref, gmp_ref,
                  gmem_ref, wq_ref, wo_ref, gmo_ref, o_ref, *, mem_heads, mem_scale):
    x = x_ref[...]
    a = _rms(a_ref[...], gao_ref[...]).astype(BF16)
    s = _rms(s_ref[...], gso_ref[...]).astype(BF16)
    mix = _dot(a, woa_ref[...]) + _dot(s, wos_ref[...])
    x = x + _rms(mix, gmp_ref[...])
    h = _rms(x, gmem_ref[...]).astype(BF16)
    q = _dot(h, wq_ref[...]).astype(BF16)
    hd = q.shape[1] // mem_heads
    outs = []
    for hh in range(mem_heads):
        cols = slice(hh * hd, (hh + 1) * hd)
        kh = mk_ref[:, cols].astype(BF16)
        vh = mv_ref[:, cols].astype(BF16)
        sc = _dot_nt(q[:, cols], kh) * mem_scale
        e = jnp.exp(sc - jnp.max(sc, axis=1, keepdims=True))
        p = e / jnp.sum(e, axis=1, keepdims=True)
        outs.append(_dot(p.astype(BF16), vh))
    o = jnp.concatenate(outs, axis=1).astype(BF16)
    o_ref[...] = x + _rms(_dot(o, wo_ref[...]), gmo_ref[...])


def _mix_mem_call(x2, a2, s2, mk, mv, gao, gso, woa, wos, gmp, gmem, wq, wo, gmo, *, tm, rows_per_batch,
                  mem_heads):
    t, d = x2.shape
    da, ds = a2.shape[1], s2.shape[1]
    n_mem, dm = mk.shape[1:]
    tiles_per_batch = rows_per_batch // tm
    body = functools.partial(_mix_mem_body, mem_heads=mem_heads, mem_scale=(dm // mem_heads) ** -0.5)
    row = lambda w: pl.BlockSpec((tm, w), lambda i: (i, 0))
    mem = pl.BlockSpec((None, n_mem, dm), lambda i: (i // tiles_per_batch, 0, 0))
    consts = [gao, gso, woa, wos, gmp, gmem, wq, wo, gmo]
    return pl.pallas_call(
        body,
        grid=(t // tm,),
        in_specs=[row(d), row(da), row(ds), mem, mem] + [_const_spec(c.shape) for c in consts],
        out_specs=row(d),
        out_shape=jax.ShapeDtypeStruct((t, d), F32),
        compiler_params=_params("parallel"),
        name="mix_out_mem_attn",
    )(x2, a2, s2, mk, mv, *consts)


def _ffn_body(x_ref, cprev_ref, gpre_ref, wg_ref, wu_ref, wd_ref, cw_ref, cb_ref, gpost_ref,
              o_ref, cnew_ref, halo, work, acc, *, nseq, lc, halo_rows):
    t = pl.program_id(1)
    n_chunks = wg_ref.shape[0]
    tm = nseq * lc
    fc = wg_ref.shape[2]
    keep = cprev_ref.shape[2]
    lo = halo_rows - keep

    @pl.when(t == 0)
    def _load_state():
        halo[...] = cprev_ref[...]

    x = x_ref[...]
    h = _rms(x, gpre_ref[...]).astype(BF16)
    acc[...] = jnp.zeros(acc.shape, F32)

    def chunk(c, carry):
        g = _dot(h, wg_ref[c]).reshape(nseq, lc, fc)
        up = _dot(h, wu_ref[c]).reshape(nseq, lc, fc)
        work[:, lo:halo_rows, :] = halo[c]
        work[:, halo_rows:halo_rows + lc, :] = g
        w = cw_ref[c]
        conv = w[0:1, :] * work[:, lo:lo + lc, :]
        for k in range(1, keep):
            conv = conv + w[k:k + 1, :] * work[:, lo + k:lo + k + lc, :]
        conv = conv + w[keep:keep + 1, :] * g
        gc = cb_ref[c] + conv
        act = (jax.nn.silu(gc) * up).reshape(tm, fc).astype(BF16)
        tail = work[:, lc + lo:lc + halo_rows, :]
        halo[c] = tail
        cnew_ref[c] = tail
        acc[...] += _dot(act, wd_ref[c])
        return carry

    lax.fori_loop(0, n_chunks, chunk, 0)
    o_ref[...] = x + _rms(acc[...], gpost_ref[...])


def _ffn_call(x2, cprev, gpre, wg, wu, wd, cw, cb, gpost, *, nseq, lc, n_batch_blocks, tiles_per_batch):
    t, d = x2.shape
    n_chunks, _, fc = wg.shape
    keep = cprev.shape[2]
    halo_rows = 8
    tm = nseq * lc
    body = functools.partial(_ffn_body, nseq=nseq, lc=lc, halo_rows=halo_rows)
    state = pl.BlockSpec((n_chunks, nseq, keep, fc), lambda b, i: (0, b, 0, 0))
    return pl.pallas_call(
        body,
        grid=(n_batch_blocks, tiles_per_batch),
        in_specs=[pl.BlockSpec((tm, d), lambda b, i: (b * tiles_per_batch + i, 0)),
                  state,
                  _const_spec(gpre.shape), _const_spec(wg.shape), _const_spec(wu.shape),
                  _const_spec(wd.shape), _const_spec(cw.shape), _const_spec(cb.shape),
                  _const_spec(gpost.shape)],
        out_specs=[pl.BlockSpec((tm, d), lambda b, i: (b * tiles_per_batch + i, 0)), state],
        out_shape=[jax.ShapeDtypeStruct((t, d), F32),
                   jax.ShapeDtypeStruct(cprev.shape, F32)],
        scratch_shapes=[pltpu.VMEM((n_chunks, nseq, keep, fc), F32),
                        pltpu.VMEM((nseq, halo_rows + lc, fc), F32),
                        pltpu.VMEM((tm, d), F32)],
        compiler_params=_params("parallel", "arbitrary"),
        name="conv_ffn",
    )(x2, cprev, gpre, wg, wu, wd, cw, cb, gpost)


def _rope_tables(pos, rope_dim, nope_dim, q_scale):
    half = rope_dim // 2
    inv = ROPE_THETA ** (-jnp.arange(half, dtype=F32) * (2.0 / rope_dim))
    ang = pos.astype(F32)[:, None] * inv[None, :]
    cos, sin = jnp.cos(ang), jnp.sin(ang)
    n = pos.shape[0]
    pad = jnp.zeros((n, LANES - rope_dim), F32)
    ck = jnp.concatenate([cos, cos, pad], axis=1)
    sk = jnp.concatenate([sin, sin, pad], axis=1)
    ones = jnp.concatenate([jnp.ones((n, nope_dim), F32), jnp.zeros((n, LANES - rope_dim - nope_dim), F32)], axis=1)
    cq = jnp.concatenate([cos, cos, ones], axis=1) * q_scale
    return jnp.stack([ck, sk, cq, sk * q_scale])


def _rot_half_cols(w):
    half = w.shape[-1] // 2
    return jnp.concatenate([-w[..., half:], w[..., :half]], axis=-1)


def _pad_last(w, width):
    return jnp.pad(w, [(0, 0)] * (w.ndim - 1) + [(0, width - w.shape[-1])])


def _block_diag(blocks):
    g, r, c = blocks.shape
    eye = jnp.eye(g, dtype=blocks.dtype)
    return (blocks[:, :, None, :] * eye[:, None, :, None]).reshape(g * r, g * c)


def _layer_weights(l, w_in, q_norm, kv_norm, w_uq, w_uk, w_uv, ssm_a_re, ssm_a_im, ssm_log_dt, ssm_b_re,
                   ssm_b_im, ssm_c_re, ssm_c_im, ssm_d, ssm_w_glu, w_out, w_gate, w_up, w_down, ffn_conv_w,
                   ffn_conv_b, fc):
    q_rank = q_norm.shape[-1]
    kv_rank = kv_norm.shape[-1]
    n_heads = w_uq.shape[2]
    nope = w_uk.shape[3]
    rope_dim = w_uq.shape[3] - nope
    dv = w_uv.shape[3]
    d_ssm = ssm_d.shape[-1]
    win = w_in[l]
    o1, o2, o3 = q_rank, q_rank + kv_rank, q_rank + kv_rank + rope_dim
    w_kr = win[:, o2:o3]
    p = {}
    p["win"] = jnp.concatenate([win[:, :o2], win[:, o3:], _pad_last(w_kr, LANES),
                                _pad_last(_rot_half_cols(w_kr), LANES)], axis=1).astype(BF16)
    uq = w_uq[l]
    q_nope, q_pe = uq[..., :nope], uq[..., nope:]
    wq1 = _pad_last(jnp.concatenate([q_pe, q_nope], axis=-1), LANES).reshape(q_rank, n_heads * LANES)
    wq2 = _pad_last(_rot_half_cols(q_pe), LANES).reshape(q_rank, n_heads * LANES)
    p["wq"] = jnp.concatenate([wq1, wq2], axis=1).astype(BF16)
    uk = w_uk[l]
    wuk_slots = jnp.pad(uk, ((0, 0), (0, 0), (rope_dim, LANES - rope_dim - nope)))
    p["wkv"] = jnp.concatenate([wuk_slots.reshape(kv_rank, n_heads * LANES),
                                w_uv[l].reshape(kv_rank, n_heads * dv)], axis=1).astype(BF16)
    p["wukt"] = jnp.transpose(wuk_slots, (1, 2, 0)).astype(BF16)
    uv = jnp.transpose(w_uv[l], (1, 0, 2))
    eye = jnp.eye(n_heads, dtype=F32)
    p["wuvp"] = (uv[:, :, None, :] * eye[:, None, :, None]).reshape(n_heads, kv_rank, n_heads * dv).astype(BF16)
    g, n = ssm_a_re.shape[1:]
    abr, abi, bbr_t, bbi_t = _s5_disc_call(ssm_a_re[l], ssm_a_im[l], ssm_log_dt[l],
                                           jnp.transpose(ssm_b_re[l], (0, 2, 1)),
                                           jnp.transpose(ssm_b_im[l], (0, 2, 1)))
    halves = 2
    gh = g // halves
    split = lambda blocks: jnp.stack([_block_diag(blocks[i * gh:(i + 1) * gh]) for i in range(halves)])
    p["abr"] = abr.reshape(1, g * n)
    p["abi"] = abi.reshape(1, g * n)
    p["bbr"] = split(bbr_t).astype(BF16)
    p["bbi"] = split(bbi_t).astype(BF16)
    p["ccr"] = split(jnp.transpose(ssm_c_re[l], (0, 2, 1))).astype(BF16)
    p["cci"] = split(-jnp.transpose(ssm_c_im[l], (0, 2, 1))).astype(BF16)
    p["ssm_d"] = ssm_d[l].reshape(1, d_ssm)
    p["wglu"] = ssm_w_glu[l].astype(BF16)
    d_attn = n_heads * dv
    p["woa"] = w_out[l][:d_attn].astype(BF16)
    p["wos"] = w_out[l][d_attn:].astype(BF16)
    d_model, d_ff = w_gate.shape[1:]
    nch = d_ff // fc
    p["wg"] = jnp.transpose(w_gate[l].reshape(d_model, nch, fc), (1, 0, 2)).astype(BF16)
    p["wu"] = jnp.transpose(w_up[l].reshape(d_model, nch, fc), (1, 0, 2)).astype(BF16)
    p["wd"] = w_down[l].reshape(nch, fc, d_model).astype(BF16)
    conv_w = ffn_conv_w.shape[1]
    p["cw"] = jnp.transpose(ffn_conv_w[l].reshape(conv_w, nch, fc), (1, 0, 2))
    p["cb"] = ffn_conv_b[l].reshape(nch, 1, fc)
    p["dims"] = dict(q_rank=q_rank, kv_rank=kv_rank, n_heads=n_heads, nope=nope, rope_dim=rope_dim, dv=dv,
                     d_ssm=d_ssm, g=g, n=n, d_ff=d_ff, fc=fc, nch=nch, conv_w=conv_w)
    return p


def _conv_state_in(state, nch, fc):
    b, keep, _ = state.shape
    return jnp.transpose(state.reshape(b, keep, nch, fc), (2, 0, 1, 3))


def _conv_state_out(state):
    nch, b, keep, fc = state.shape
    return jnp.transpose(state, (1, 2, 0, 3)).reshape(b, keep, nch * fc)


def _row(v):
    return v.reshape(1, -1)


def kernel(x_prompt, x_sample, mem_prompt, cache_kv_latent, cache_k_rope, page_table, state_ssm_re, state_ssm_im, state_ffn_conv, cache_mem_k, cache_mem_v, norm_mix_pre, w_in, q_norm, kv_norm, w_uq, w_uk, w_uv, ssm_a_re, ssm_a_im, ssm_log_dt, ssm_b_re, ssm_b_im, ssm_c_re, ssm_c_im, ssm_d, ssm_w_glu, norm_attn_out, norm_ssm_out, w_out, norm_mix_post, norm_mem_pre, mem_norm, w_q_mem, w_k_mem, w_v_mem, w_o_mem, norm_mem_post, norm_ffn_pre, w_gate, w_up, ffn_conv_w, ffn_conv_b, w_down, norm_ffn_post):
    depth = w_in.shape[0]
    b, l, d_model = x_prompt.shape
    db, ls, _ = x_sample.shape
    n_mem = mem_prompt.shape[1]
    mem_heads = cache_mem_k.shape[3]
    d_mem = mem_heads * cache_mem_k.shape[4]
    past_len = page_table.shape[1] * cache_kv_latent.shape[2]
    fc = 256
    tm = min(256, l)
    tq = min(512, l)
    tc = min(128, l)
    pages = min(16, page_table.shape[1])

    xp = x_prompt.reshape(b * l, d_model)
    xs = x_sample.reshape(db * ls, d_model)
    outs = {k: [] for k in ("p_kv", "p_kr", "p_sr", "p_si", "p_cv", "p_mk", "p_mv",
                            "s_kv", "s_kr", "s_sr", "s_si", "s_cv")}
    for li in range(depth):
        p = _layer_weights(li, w_in, q_norm, kv_norm, w_uq, w_uk, w_uv, ssm_a_re, ssm_a_im, ssm_log_dt,
                           ssm_b_re, ssm_b_im, ssm_c_re, ssm_c_im, ssm_d, ssm_w_glu, w_out, w_gate, w_up,
                           w_down, ffn_conv_w, ffn_conv_b, fc)
        dm = p["dims"]
        n_heads, dv, rope_dim, nope = dm["n_heads"], dm["dv"], dm["rope_dim"], dm["nope"]
        g, n, d_ssm = dm["g"], dm["n"], dm["d_ssm"]
        q_scale = (nope + rope_dim) ** -0.5 * LOG2E
        pre_kw = dict(n_heads=n_heads, q_rank=dm["q_rank"], kv_rank=dm["kv_rank"], d_ssm=d_ssm,
                      rope_dim=rope_dim, v_width=n_heads * dv)
        gpre, gq, gkv = _row(norm_mix_pre[li]), _row(q_norm[li]), _row(kv_norm[li])
        mix_consts = (_row(norm_attn_out[li]), _row(norm_ssm_out[li]), p["woa"], p["wos"],
                      _row(norm_mix_post[li]), _row(norm_mem_pre[li]), w_q_mem[li].astype(BF16),
                      w_o_mem[li].astype(BF16), _row(norm_mem_post[li]))
        ffn_consts = (_row(norm_ffn_pre[li]), p["wg"], p["wu"], p["wd"], p["cw"], p["cb"],
                      _row(norm_ffn_post[li]))

        mk, mv = _memkv_call(mem_prompt.reshape(b * n_mem, d_model), _row(mem_norm[li]),
                             w_k_mem[li].astype(BF16), w_v_mem[li].astype(BF16), tm=min(512, b * n_mem))
        tab_p = _rope_tables(jnp.arange(l, dtype=jnp.int32), rope_dim, nope, q_scale)
        q, k, v, ckv, kr, u = _pre_call(xp, tab_p, gpre, gq, gkv, p["win"], p["wq"], p["wkv"], tm=tm,
                                        q_dtype=BF16, **pre_kw)
        attn = _attn_call(q.reshape(b, l, -1), k.reshape(b, l, -1), v.reshape(b, l, -1),
                          n_heads=n_heads, dv=dv, tq=tq)
        zeros_state = jnp.zeros((b, g * n), F32)
        ssm, hr, hi = _s5_call(u.reshape(b, l, d_ssm), zeros_state, zeros_state, p["abr"], p["abi"],
                               p["bbr"], p["bbi"], p["ccr"], p["cci"], p["ssm_d"], p["wglu"], tc=tc)
        xp = _mix_mem_call(xp, attn.reshape(b * l, -1), ssm.reshape(b * l, -1),
                           mk.reshape(b, n_mem, d_mem), mv.reshape(b, n_mem, d_mem), *mix_consts,
                           tm=tm, rows_per_batch=l, mem_heads=mem_heads)
        conv0 = jnp.zeros((dm["nch"], b, dm["conv_w"] - 1, fc), F32)
        xp, cv = _ffn_call(xp, conv0, *ffn_consts, nseq=1, lc=tm, n_batch_blocks=b, tiles_per_batch=l // tm)
        outs["p_kv"].append(ckv.reshape(b, l, -1))
        outs["p_kr"].append(kr.reshape(b, l, -1))
        outs["p_sr"].append(hr.reshape(b, g, n))
        outs["p_si"].append(hi.reshape(b, g, n))
        outs["p_cv"].append(_conv_state_out(cv))
        outs["p_mk"].append(mk.reshape(b, n_mem, mem_heads, -1))
        outs["p_mv"].append(mv.reshape(b, n_mem, mem_heads, -1))

        ts = db * ls
        pos_s = past_len + jnp.arange(ls, dtype=jnp.int32)
        tab_s = jnp.tile(_rope_tables(pos_s, rope_dim, nope, q_scale), (1, db, 1))
        q, _, _, ckv, kr, u = _pre_call(xs, tab_s, gpre, gq, gkv, p["win"], p["wq"], p["wkv"], tm=ts,
                                        q_dtype=F32, **pre_kw)
        attn = _dec_attn_call(page_table, q.reshape(db, ls, -1), ckv.reshape(db, ls, -1),
                              kr.reshape(db, ls, -1), p["wukt"], p["wuvp"], cache_kv_latent, cache_k_rope,
                              layer=li, n_heads=n_heads, dv=dv, pages=pages)
        ssm, hr, hi = _s5_call(u.reshape(db, ls, d_ssm), state_ssm_re[li].reshape(db, g * n),
                               state_ssm_im[li].reshape(db, g * n), p["abr"], p["abi"], p["bbr"], p["bbi"],
                               p["ccr"], p["cci"], p["ssm_d"], p["wglu"], tc=ls)
        xs = _mix_mem_call(xs, attn.reshape(ts, -1), ssm.reshape(ts, -1),
                           cache_mem_k[li].reshape(db, n_mem, d_mem), cache_mem_v[li].reshape(db, n_mem, d_mem),
                           *mix_consts, tm=ls, rows_per_batch=ls, mem_heads=mem_heads)
        xs, cv = _ffn_call(xs, _conv_state_in(state_ffn_conv[li], dm["nch"], fc), *ffn_consts,
                           nseq=db, lc=ls, n_batch_blocks=1, tiles_per_batch=1)
        outs["s_kv"].append(ckv.reshape(db, ls, -1))
        outs["s_kr"].append(kr.reshape(db, ls, -1))
        outs["s_sr"].append(hr.reshape(db, g, n))
        outs["s_si"].append(hi.reshape(db, g, n))
        outs["s_cv"].append(_conv_state_out(cv))

    st = lambda key: jnp.stack(outs[key])
    return (xp.reshape(b, l, d_model), xs.reshape(db, ls, d_model),
            st("p_kv"), st("p_kr"), st("p_sr"), st("p_si"), st("p_cv"), st("p_mk"), st("p_mv"),
            st("s_kv"), st("s_kr"), st("s_sr"), st("s_si"), st("s_cv"))
```

```python
import functools
import math

import jax
import jax.numpy as jnp
from jax import lax
from jax.experimental import pallas as pl
from jax.experimental.pallas import tpu as pltpu

F32 = jnp.float32
BF16 = jnp.bfloat16

EPS = 1e-6
ROPE_THETA = 10000.0
LANES = 128
NEG_BIG = -1e30
LOG2E = 1.4426950408889634
VMEM_LIMIT = 56 * 1024 * 1024


def _rms(x, g):
    y = x * lax.rsqrt(jnp.mean(x * x, axis=-1, keepdims=True) + EPS)
    return y * g


def _dot(a, b):
    return jnp.dot(a, b, preferred_element_type=F32)


def _dot_nt(a, b):
    return lax.dot_general(a, b, (((1,), (1,)), ((), ())), preferred_element_type=F32)


def _rep_lanes(x, n):
    return jnp.concatenate([x] * n, axis=1) if n > 1 else x


def _const_spec(shape):
    nd = len(shape)
    return pl.BlockSpec(shape, lambda *_: (0,) * nd, pipeline_mode=pl.Buffered(1))


def _params(*sem):
    return pltpu.CompilerParams(dimension_semantics=sem, vmem_limit_bytes=VMEM_LIMIT)


def _pre_body(x_ref, tab_ref, gpre_ref, gq_ref, gkv_ref, win_ref, wq_ref, wkv_ref,
              q_ref, k_ref, v_ref, ckv_ref, kr_ref, u_ref, *, n_heads, q_rank, kv_rank, d_ssm, rope_dim):
    x = x_ref[...]
    h = _rms(x, gpre_ref[...]).astype(BF16)
    z = _dot(h, win_ref[...])
    o1 = q_rank
    o2 = o1 + kv_rank
    o3 = o2 + d_ssm
    o4 = o3 + LANES
    cq, ckv, u = z[:, :o1], z[:, o1:o2], z[:, o2:o3]
    k1, k2 = z[:, o3:o4], z[:, o4:o4 + LANES]
    krs = k1 * tab_ref[0] + k2 * tab_ref[1]
    kr_ref[...] = krs[:, :rope_dim]
    ckv_n = _rms(ckv, gkv_ref[...])
    ckv_ref[...] = ckv_n
    kv2 = _dot(ckv_n.astype(BF16), wkv_ref[...])
    nk = n_heads * LANES
    k_ref[...] = (kv2[:, :nk] + _rep_lanes(krs, n_heads)).astype(k_ref.dtype)
    v_ref[...] = kv2[:, nk:].astype(v_ref.dtype)
    qn = _rms(cq, gq_ref[...]).astype(BF16)
    qq = _dot(qn, wq_ref[...])
    q = qq[:, :nk] * _rep_lanes(tab_ref[2], n_heads) + qq[:, nk:] * _rep_lanes(tab_ref[3], n_heads)
    q_ref[...] = q.astype(q_ref.dtype)
    u_ref[...] = u


def _pre_call(x2, tab, gpre, gq, gkv, win, wq, wkv, *, tm, n_heads, q_rank, kv_rank, d_ssm, rope_dim,
              q_dtype, v_width):
    t, d = x2.shape
    ntab = tab.shape[1] // tm
    nk = n_heads * LANES
    row = lambda w: pl.BlockSpec((tm, w), lambda i: (i, 0))
    body = functools.partial(_pre_body, n_heads=n_heads, q_rank=q_rank, kv_rank=kv_rank, d_ssm=d_ssm,
                             rope_dim=rope_dim)
    return pl.pallas_call(
        body,
        grid=(t // tm,),
        in_specs=[row(d),
                  pl.BlockSpec((4, tm, LANES), lambda i: (0, i % ntab, 0)),
                  _const_spec(gpre.shape), _const_spec(gq.shape), _const_spec(gkv.shape),
                  _const_spec(win.shape), _const_spec(wq.shape), _const_spec(wkv.shape)],
        out_specs=[row(nk), row(nk), row(v_width), row(kv_rank), row(rope_dim), row(d_ssm)],
        out_shape=[jax.ShapeDtypeStruct((t, nk), q_dtype),
                   jax.ShapeDtypeStruct((t, nk), BF16),
                   jax.ShapeDtypeStruct((t, v_width), BF16),
                   jax.ShapeDtypeStruct((t, kv_rank), F32),
                   jax.ShapeDtypeStruct((t, rope_dim), F32),
                   jax.ShapeDtypeStruct((t, d_ssm), F32)],
        compiler_params=_params("parallel"),
        name="pre_proj",
    )(x2, tab, gpre, gq, gkv, win, wq, wkv)


def _attn_body(q_ref, k_ref, v_ref, o_ref, m_ref, l_ref, acc_ref, *, n_heads, tq, tk, dv):
    i = pl.program_id(1)
    j = pl.program_id(2)

    @pl.when(j == 0)
    def _init():
        m_ref[...] = jnp.full(m_ref.shape, NEG_BIG, F32)
        l_ref[...] = jnp.zeros(l_ref.shape, F32)
        acc_ref[...] = jnp.zeros(acc_ref.shape, F32)

    def step(masked):
        for h in range(n_heads):
            qh = q_ref[:, h * LANES:(h + 1) * LANES]
            kh = k_ref[:, h * LANES:(h + 1) * LANES]
            vh = v_ref[:, h * dv:(h + 1) * dv]
            s = _dot_nt(qh, kh)
            if masked:
                row = lax.broadcasted_iota(jnp.int32, (tq, tk), 0)
                col = lax.broadcasted_iota(jnp.int32, (tq, tk), 1)
                s = jnp.where(col <= row, s, NEG_BIG)
            m_prev = m_ref[h]
            m_next = jnp.maximum(m_prev, jnp.max(s, axis=1, keepdims=True))
            alpha = jnp.exp2(m_prev - m_next)
            p = jnp.exp2(s - _rep_lanes(m_next, tk // LANES))
            l_ref[h] = alpha * l_ref[h] + jnp.sum(p, axis=1, keepdims=True)
            m_ref[h] = m_next
            acc_ref[h] = acc_ref[h] * alpha[:, :dv] + _dot(p.astype(BF16), vh)

    @pl.when(j < i)
    def _off_diag():
        step(False)

    @pl.when(j == i)
    def _diag():
        step(True)
        for h in range(n_heads):
            o_ref[:, h * dv:(h + 1) * dv] = acc_ref[h] / l_ref[h][:, :dv]


def _attn_call(q, k, v, *, n_heads, dv, tq):
    b, l, nk = q.shape
    tk = tq
    body = functools.partial(_attn_body, n_heads=n_heads, tq=tq, tk=tk, dv=dv)
    return pl.pallas_call(
        body,
        grid=(b, l // tq, l // tk),
        in_specs=[pl.BlockSpec((None, tq, nk), lambda bb, i, j: (bb, i, 0)),
                  pl.BlockSpec((None, tk, nk), lambda bb, i, j: (bb, jnp.minimum(i, j), 0)),
                  pl.BlockSpec((None, tk, n_heads * dv), lambda bb, i, j: (bb, jnp.minimum(i, j), 0))],
        out_specs=pl.BlockSpec((None, tq, n_heads * dv), lambda bb, i, j: (bb, i, 0)),
        out_shape=jax.ShapeDtypeStruct((b, l, n_heads * dv), F32),
        scratch_shapes=[pltpu.VMEM((n_heads, tq, LANES), F32),
                        pltpu.VMEM((n_heads, tq, LANES), F32),
                        pltpu.VMEM((n_heads, tq, dv), F32)],
        compiler_params=_params("parallel", "parallel", "arbitrary"),
        name="mla_prompt_attn",
    )(q, k, v)


def _dec_attn_body(pt_ref, q_ref, ckv_ref, kr_ref, wukt_ref, wuvp_ref, *rest,
                   n_heads, s_len, rope_dim, pages):
    lat_refs = rest[:pages]
    pe_refs = rest[pages:2 * pages]
    o_ref, ql_s, qp_s, m_s, l_s, acc_s = rest[2 * pages:]
    j = pl.program_id(1)
    rows = n_heads * s_len

    @pl.when(j == 0)
    def _init():
        for h in range(n_heads):
            qs = q_ref[:, h * LANES:(h + 1) * LANES]
            ql_s[h * s_len:(h + 1) * s_len, :] = _dot(qs.astype(BF16), wukt_ref[h])
            qp_s[h * s_len:(h + 1) * s_len, :] = qs[:, :rope_dim]
        m_s[...] = jnp.full(m_s.shape, NEG_BIG, F32)
        l_s[...] = jnp.zeros(l_s.shape, F32)
        acc_s[...] = jnp.zeros(acc_s.shape, F32)

    ql = ql_s[...].astype(BF16)
    qp = qp_s[...].astype(BF16)

    def update(s, vals):
        m_prev = m_s[...]
        m_next = jnp.maximum(m_prev, jnp.max(s, axis=1, keepdims=True))
        alpha = jnp.exp2(m_prev - m_next)
        p = jnp.exp2(s - m_next)
        l_s[...] = alpha * l_s[...] + jnp.sum(p, axis=1, keepdims=True)
        m_s[...] = m_next
        acc_s[...] = acc_s[...] * alpha + _dot(p.astype(BF16), vals)

    lat = jnp.concatenate([r[...].astype(BF16) for r in lat_refs], axis=0)
    pe_t = jnp.concatenate([r[...].astype(BF16) for r in pe_refs], axis=1)
    update(_dot_nt(ql, lat) + _dot(qp, pe_t), lat)

    @pl.when(j == pl.num_programs(1) - 1)
    def _finish():
        pad = LANES - s_len
        new_lat = jnp.concatenate([ckv_ref[...], jnp.zeros((pad, ckv_ref.shape[1]), F32)], axis=0).astype(BF16)
        new_pe = jnp.concatenate([kr_ref[...], jnp.zeros((pad, rope_dim), F32)], axis=0).astype(BF16)
        s = _dot_nt(ql, new_lat) + _dot_nt(qp, new_pe)
        row = lax.broadcasted_iota(jnp.int32, (rows, LANES), 0)
        col = lax.broadcasted_iota(jnp.int32, (rows, LANES), 1)
        s = jnp.where(col <= row % s_len, s, NEG_BIG)
        update(s, new_lat)
        o_lat = (acc_s[...] / l_s[...]).astype(BF16)
        out = _dot(o_lat[0:s_len], wuvp_ref[0])
        for h in range(1, n_heads):
            out = out + _dot(o_lat[h * s_len:(h + 1) * s_len], wuvp_ref[h])
        o_ref[...] = out


def _dec_attn_call(page_table, q, ckv, kr, wukt, wuvp, cache_lat, cache_pe_t, *, layer, n_heads, dv, pages):
    db, s_len, nk = q.shape
    kv_rank = ckv.shape[-1]
    rope_dim = kr.shape[-1]
    n_pages = page_table.shape[1]
    page = cache_lat.shape[2]
    assert n_pages % pages == 0
    rows = n_heads * s_len
    body = functools.partial(_dec_attn_body, n_heads=n_heads, s_len=s_len, rope_dim=rope_dim, pages=pages)

    def page_spec(shape, r):
        return pl.BlockSpec((None, None) + shape, lambda b, j, pt: (layer, pt[b, j * pages + r], 0, 0))

    grid_spec = pltpu.PrefetchScalarGridSpec(
        num_scalar_prefetch=1,
        grid=(db, n_pages // pages),
        in_specs=[pl.BlockSpec((None, s_len, nk), lambda b, j, pt: (b, 0, 0)),
                  pl.BlockSpec((None, s_len, kv_rank), lambda b, j, pt: (b, 0, 0)),
                  pl.BlockSpec((None, s_len, rope_dim), lambda b, j, pt: (b, 0, 0)),
                  _const_spec(wukt.shape), _const_spec(wuvp.shape)]
                 + [page_spec((page, kv_rank), r) for r in range(pages)]
                 + [page_spec((rope_dim, page), r) for r in range(pages)],
        out_specs=pl.BlockSpec((None, s_len, n_heads * dv), lambda b, j, pt: (b, 0, 0)),
        scratch_shapes=[pltpu.VMEM((rows, kv_rank), F32),
                        pltpu.VMEM((rows, rope_dim), F32),
                        pltpu.VMEM((rows, 1), F32),
                        pltpu.VMEM((rows, 1), F32),
                        pltpu.VMEM((rows, kv_rank), F32)])
    return pl.pallas_call(
        body,
        grid_spec=grid_spec,
        out_shape=jax.ShapeDtypeStruct((db, s_len, n_heads * dv), F32),
        compiler_params=_params("parallel", "arbitrary"),
        name="mla_sample_attn",
    )(page_table, q, ckv, kr, wukt, wuvp, *([cache_lat] * pages), *([cache_pe_t] * pages))


def _s5_disc_body(are_ref, aim_ref, ldt_ref, bre_ref, bim_ref, abr_ref, abi_ref, bbr_ref, bbi_ref):
    a_re = are_ref[...]
    a_im = aim_ref[...]
    dt = jnp.exp(ldt_ref[...])
    mag = jnp.exp(dt * a_re)
    abr = mag * jnp.cos(dt * a_im)
    abi = mag * jnp.sin(dt * a_im)
    den = a_re * a_re + a_im * a_im
    nr, ni = abr - 1.0, abi
    fr = (nr * a_re + ni * a_im) / den
    fi = (ni * a_re - nr * a_im) / den
    abr_ref[...] = abr
    abi_ref[...] = abi
    b_re = bre_ref[...]
    b_im = bim_ref[...]
    bbr_ref[...] = fr[:, None, :] * b_re - fi[:, None, :] * b_im
    bbi_ref[...] = fr[:, None, :] * b_im + fi[:, None, :] * b_re


def _s5_disc_call(a_re, a_im, log_dt, b_re_t, b_im_t):
    g, n = a_re.shape
    c = b_re_t.shape[1]
    return pl.pallas_call(
        _s5_disc_body,
        out_shape=[jax.ShapeDtypeStruct((g, n), F32), jax.ShapeDtypeStruct((g, n), F32),
                   jax.ShapeDtypeStruct((g, c, n), F32), jax.ShapeDtypeStruct((g, c, n), F32)],
        name="s5_discretise",
    )(a_re, a_im, log_dt.reshape(g, 1), b_re_t, b_im_t)


def _s5_body(u_ref, h0r_ref, h0i_ref, ar_ref, ai_ref, bbr_ref, bbi_ref, ccr_ref, cci_ref, d_ref, wglu_ref,
             y_ref, hr_out, hi_out, bur, bui, hr_s, hi_s, *, nb, tc, lane_tiles, unroll):
    c = pl.program_id(0)
    d_ssm = u_ref.shape[-1]
    n_state = hr_s.shape[-1]
    halves = bbr_ref.shape[0]
    ch_half = d_ssm // halves
    st_half = n_state // halves

    @pl.when(c == 0)
    def _init():
        hr_s[...] = h0r_ref[...]
        hi_s[...] = h0i_ref[...]

    u = jnp.swapaxes(u_ref[...], 0, 1).reshape(tc * nb, d_ssm)
    ub = u.astype(BF16)
    tiles_half = st_half // LANES
    for hf in range(halves):
        ublk = ub[:, hf * ch_half:(hf + 1) * ch_half]
        br = _dot(ublk, bbr_ref[hf])
        bi = _dot(ublk, bbi_ref[hf])
        for k in range(tiles_half):
            bur[hf * tiles_half + k] = br[:, k * LANES:(k + 1) * LANES]
            bui[hf * tiles_half + k] = bi[:, k * LANES:(k + 1) * LANES]

    for k0 in range(0, n_state // LANES, lane_tiles):
        tiles = range(k0, k0 + lane_tiles)
        ar = [jnp.broadcast_to(ar_ref[:, k * LANES:(k + 1) * LANES], (nb, LANES)) for k in tiles]
        ai = [jnp.broadcast_to(ai_ref[:, k * LANES:(k + 1) * LANES], (nb, LANES)) for k in tiles]

        def body(t, carry, tiles=tiles, ar=ar, ai=ai):
            rows = pl.ds(pl.multiple_of(t * nb, nb), nb)
            out = []
            for n, k in enumerate(tiles):
                hr, hi = carry[2 * n], carry[2 * n + 1]
                nr = ar[n] * hr - ai[n] * hi + bur[k, rows, :]
                ni = ar[n] * hi + ai[n] * hr + bui[k, rows, :]
                bur[k, rows, :] = nr
                bui[k, rows, :] = ni
                out += [nr, ni]
            return tuple(out)

        init = []
        for k in tiles:
            init += [hr_s[:, k * LANES:(k + 1) * LANES], hi_s[:, k * LANES:(k + 1) * LANES]]
        fin = lax.fori_loop(0, tc, body, tuple(init), unroll=unroll)
        for n, k in enumerate(tiles):
            hr_s[:, k * LANES:(k + 1) * LANES] = fin[2 * n]
            hi_s[:, k * LANES:(k + 1) * LANES] = fin[2 * n + 1]

    hr_out[...] = hr_s[...]
    hi_out[...] = hi_s[...]

    ys = []
    for hf in range(halves):
        hr_hist = jnp.concatenate([bur[hf * tiles_half + k] for k in range(tiles_half)], axis=1).astype(BF16)
        hi_hist = jnp.concatenate([bui[hf * tiles_half + k] for k in range(tiles_half)], axis=1).astype(BF16)
        ys.append(_dot(hr_hist, ccr_ref[hf]) + _dot(hi_hist, cci_ref[hf]))
    y = jnp.concatenate(ys, axis=1) + d_ref[...] * u
    g = jax.nn.gelu(y)
    out = g * jax.nn.sigmoid(_dot(g.astype(BF16), wglu_ref[...]))
    y_ref[...] = jnp.swapaxes(out.reshape(tc, nb, d_ssm), 0, 1)


def _s5_call(u3, h0r, h0i, ar, ai, bbr, bbi, ccr, cci, d, wglu, *, tc):
    nb, l, d_ssm = u3.shape
    n_state = ar.shape[-1]
    lane_tiles = math.gcd(n_state // LANES, max(1, 64 // nb))
    body = functools.partial(_s5_body, nb=nb, tc=tc, lane_tiles=lane_tiles, unroll=min(tc, 8))
    return pl.pallas_call(
        body,
        grid=(l // tc,),
        in_specs=[pl.BlockSpec((nb, tc, d_ssm), lambda c: (0, c, 0)),
                  _const_spec(h0r.shape), _const_spec(h0i.shape),
                  _const_spec(ar.shape), _const_spec(ai.shape),
                  _const_spec(bbr.shape), _const_spec(bbi.shape),
                  _const_spec(ccr.shape), _const_spec(cci.shape),
                  _const_spec(d.shape), _const_spec(wglu.shape)],
        out_specs=[pl.BlockSpec((nb, tc, d_ssm), lambda c: (0, c, 0)),
                   pl.BlockSpec((nb, n_state), lambda c: (0, 0)),
                   pl.BlockSpec((nb, n_state), lambda c: (0, 0))],
        out_shape=[jax.ShapeDtypeStruct((nb, l, d_ssm), F32),
                   jax.ShapeDtypeStruct((nb, n_state), F32),
                   jax.ShapeDtypeStruct((nb, n_state), F32)],
        scratch_shapes=[pltpu.VMEM((n_state // LANES, nb * tc, LANES), F32),
                        pltpu.VMEM((n_state // LANES, nb * tc, LANES), F32),
                        pltpu.VMEM((nb, n_state), F32), pltpu.VMEM((nb, n_state), F32)],
        compiler_params=_params("arbitrary"),
        name="s5_scan_glu",
    )(u3, h0r, h0i, ar, ai, bbr, bbi, ccr, cci, d, wglu)


def _memkv_body(m_ref, g_ref, wk_ref, wv_ref, k_ref, v_ref):
    m = _rms(m_ref[...], g_ref[...]).astype(BF16)
    k_ref[...] = _dot(m, wk_ref[...]).reshape(k_ref.shape)
    v_ref[...] = _dot(m, wv_ref[...]).reshape(v_ref.shape)


def _memkv_call(mem2, g, wk, wv, *, tm, mem_heads):
    t, d = mem2.shape
    hd = wk.shape[1] // mem_heads
    out = pl.BlockSpec((tm * mem_heads, hd), lambda i: (i, 0))
    return pl.pallas_call(
        _memkv_body,
        grid=(t // tm,),
        in_specs=[pl.BlockSpec((tm, d), lambda i: (i, 0)),
                  _const_spec(g.shape), _const_spec(wk.shape), _const_spec(wv.shape)],
        out_specs=[out, out],
        out_shape=[jax.ShapeDtypeStruct((t * mem_heads, hd), F32)] * 2,
        compiler_params=_params("parallel"),
        name="mem_kv",
    )(mem2, g, wk, wv)


def _mix_mem_body(x_ref, a_ref, s_ref, mk_ref, mv_ref, gao_ref, gso_ref, woa_ref, wos_ref, gmp_ref,
                  gmem_ref, wq_ref, wo_ref, gmo_ref, o_ref, *, mem_heads, n_mem, seqs, mem_scale):
    x = x_ref[...]
    a = _rms(a_ref[...], gao_ref[...]).astype(BF16)
    s = _rms(s_ref[...], gso_ref[...]).astype(BF16)
    mix = _dot(a, woa_ref[...]) + _dot(s, wos_ref[...])
    x = x + _rms(mix, gmp_ref[...])
    h = _rms(x, gmem_ref[...]).astype(BF16)
    q = _dot(h, wq_ref[...])
    hd = q.shape[1] // mem_heads
    rows = q.shape[0] // seqs
    outs = []
    for si in range(seqs):
        heads = []
        for hh in range(mem_heads):
            pick = pl.ds((si * n_mem) * mem_heads + hh, n_mem, stride=mem_heads)
            kh = mk_ref[pick, :].astype(BF16)
            vh = mv_ref[pick, :].astype(BF16)
            qh = q[si * rows:(si + 1) * rows, hh * hd:(hh + 1) * hd].astype(BF16)
            sc = _dot_nt(qh, kh) * mem_scale
            e = jnp.exp(sc - jnp.max(sc, axis=1, keepdims=True))
            p = e / jnp.sum(e, axis=1, keepdims=True)
            heads.append(_dot(p.astype(BF16), vh))
        outs.append(jnp.concatenate(heads, axis=1))
    o = (jnp.concatenate(outs, axis=0) if seqs > 1 else outs[0]).astype(BF16)
    o_ref[...] = x + _rms(_dot(o, wo_ref[...]), gmo_ref[...])


def _mix_mem_call(x2, a2, s2, mk, mv, gao, gso, woa, wos, gmp, gmem, wq, wo, gmo, *, tm, rows_per_batch,
                  mem_heads, n_mem):
    t, d = x2.shape
    da, ds = a2.shape[1], s2.shape[1]
    hd = mk.shape[1]
    seqs = max(1, tm // rows_per_batch)
    tiles_per_batch = max(1, rows_per_batch // tm)
    body = functools.partial(_mix_mem_body, mem_heads=mem_heads, n_mem=n_mem, seqs=seqs, mem_scale=hd ** -0.5)
    row = lambda w: pl.BlockSpec((tm, w), lambda i: (i, 0))
    mem = pl.BlockSpec((seqs * n_mem * mem_heads, hd), lambda i: (i // tiles_per_batch, 0))
    consts = [gao, gso, woa, wos, gmp, gmem, wq, wo, gmo]
    return pl.pallas_call(
        body,
        grid=(t // tm,),
        in_specs=[row(d), row(da), row(ds), mem, mem] + [_const_spec(c.shape) for c in consts],
        out_specs=row(d),
        out_shape=jax.ShapeDtypeStruct((t, d), F32),
        compiler_params=_params("parallel"),
        name="mix_out_mem_attn",
    )(x2, a2, s2, mk, mv, *consts)


def _ffn_body(x_ref, cprev_ref, gpre_ref, wg_ref, wu_ref, wd_ref, cw_ref, cb_ref, gpost_ref,
              o_ref, cnew_ref, halo, work, act, *, nseq, lc, halo_rows):
    t = pl.program_id(1)
    n_chunks = wg_ref.shape[0]
    tm = nseq * lc
    fc = wg_ref.shape[2]
    keep = cprev_ref.shape[2]
    lo = halo_rows - keep

    @pl.when(t == 0)
    def _load_state():
        halo[...] = cprev_ref[...]

    x = x_ref[...]
    h = _rms(x, gpre_ref[...]).astype(BF16)

    for c in range(n_chunks):
        g = _dot(h, wg_ref[c]).reshape(nseq, lc, fc)
        up = _dot(h, wu_ref[c]).reshape(nseq, lc, fc)
        work[c, :, lo:halo_rows, :] = halo[c]
        work[c, :, halo_rows:halo_rows + lc, :] = g
        w = cw_ref[c]
        conv = w[0:1, :] * work[c, :, lo:lo + lc, :]
        for k in range(1, keep):
            conv = conv + w[k:k + 1, :] * work[c, :, lo + k:lo + k + lc, :]
        conv = conv + w[keep:keep + 1, :] * g
        gc = cb_ref[c] + conv
        act[:, c * fc:(c + 1) * fc] = (jax.nn.silu(gc) * up).reshape(tm, fc).astype(BF16)
        tail = work[c, :, lc + lo:lc + halo_rows, :]
        halo[c] = tail
        cnew_ref[c] = tail

    o_ref[...] = x + _rms(_dot(act[...], wd_ref[...]), gpost_ref[...])


def _ffn_call(x2, cprev, gpre, wg, wu, wd, cw, cb, gpost, *, nseq, lc, n_batch_blocks, tiles_per_batch):
    t, d = x2.shape
    n_chunks, _, fc = wg.shape
    keep = cprev.shape[2]
    halo_rows = 8
    tm = nseq * lc
    body = functools.partial(_ffn_body, nseq=nseq, lc=lc, halo_rows=halo_rows)
    state = pl.BlockSpec((n_chunks, nseq, keep, fc), lambda b, i: (0, b, 0, 0))
    return pl.pallas_call(
        body,
        grid=(n_batch_blocks, tiles_per_batch),
        in_specs=[pl.BlockSpec((tm, d), lambda b, i: (b * tiles_per_batch + i, 0)),
                  state,
                  _const_spec(gpre.shape), _const_spec(wg.shape), _const_spec(wu.shape),
                  _const_spec(wd.shape), _const_spec(cw.shape), _const_spec(cb.shape),
                  _const_spec(gpost.shape)],
        out_specs=[pl.BlockSpec((tm, d), lambda b, i: (b * tiles_per_batch + i, 0)), state],
        out_shape=[jax.ShapeDtypeStruct((t, d), F32),
                   jax.ShapeDtypeStruct(cprev.shape, F32)],
        scratch_shapes=[pltpu.VMEM((n_chunks, nseq, keep, fc), F32),
                        pltpu.VMEM((n_chunks, nseq, halo_rows + lc, fc), F32),
                        pltpu.VMEM((tm, n_chunks * fc), BF16)],
        compiler_params=_params("parallel", "arbitrary"),
        name="conv_ffn",
    )(x2, cprev, gpre, wg, wu, wd, cw, cb, gpost)


def _rope_tables(pos, rope_dim, nope_dim, q_scale):
    half = rope_dim // 2
    inv = ROPE_THETA ** (-jnp.arange(half, dtype=F32) * (2.0 / rope_dim))
    ang = pos.astype(F32)[:, None] * inv[None, :]
    cos, sin = jnp.cos(ang), jnp.sin(ang)
    n = pos.shape[0]
    pad = jnp.zeros((n, LANES - rope_dim), F32)
    ck = jnp.concatenate([cos, cos, pad], axis=1)
    sk = jnp.concatenate([sin, sin, pad], axis=1)
    ones = jnp.concatenate([jnp.ones((n, nope_dim), F32), jnp.zeros((n, LANES - rope_dim - nope_dim), F32)], axis=1)
    cq = jnp.concatenate([cos, cos, ones], axis=1) * q_scale
    return jnp.stack([ck, sk, cq, sk * q_scale])


def _rot_half_cols(w):
    half = w.shape[-1] // 2
    return jnp.concatenate([-w[..., half:], w[..., :half]], axis=-1)


def _pad_last(w, width):
    return jnp.pad(w, [(0, 0)] * (w.ndim - 1) + [(0, width - w.shape[-1])])


def _block_diag(blocks):
    g, r, c = blocks.shape
    eye = jnp.eye(g, dtype=blocks.dtype)
    return (blocks[:, :, None, :] * eye[:, None, :, None]).reshape(g * r, g * c)


def _layer_weights(l, w_in, q_norm, kv_norm, w_uq, w_uk, w_uv, ssm_a_re, ssm_a_im, ssm_log_dt, ssm_b_re,
                   ssm_b_im, ssm_c_re, ssm_c_im, ssm_d, ssm_w_glu, w_out, w_gate, w_up, w_down, ffn_conv_w,
                   ffn_conv_b, fc):
    q_rank = q_norm.shape[-1]
    kv_rank = kv_norm.shape[-1]
    n_heads = w_uq.shape[2]
    nope = w_uk.shape[3]
    rope_dim = w_uq.shape[3] - nope
    dv = w_uv.shape[3]
    d_ssm = ssm_d.shape[-1]
    win = w_in[l]
    o1, o2, o3 = q_rank, q_rank + kv_rank, q_rank + kv_rank + rope_dim
    w_kr = win[:, o2:o3]
    p = {}
    p["win"] = jnp.concatenate([win[:, :o2], win[:, o3:], _pad_last(w_kr, LANES),
                                _pad_last(_rot_half_cols(w_kr), LANES)], axis=1).astype(BF16)
    uq = w_uq[l]
    q_nope, q_pe = uq[..., :nope], uq[..., nope:]
    wq1 = _pad_last(jnp.concatenate([q_pe, q_nope], axis=-1), LANES).reshape(q_rank, n_heads * LANES)
    wq2 = _pad_last(_rot_half_cols(q_pe), LANES).reshape(q_rank, n_heads * LANES)
    p["wq"] = jnp.concatenate([wq1, wq2], axis=1).astype(BF16)
    uk = w_uk[l]
    wuk_slots = jnp.pad(uk, ((0, 0), (0, 0), (rope_dim, LANES - rope_dim - nope)))
    p["wkv"] = jnp.concatenate([wuk_slots.reshape(kv_rank, n_heads * LANES),
                                w_uv[l].reshape(kv_rank, n_heads * dv)], axis=1).astype(BF16)
    p["wukt"] = jnp.transpose(wuk_slots, (1, 2, 0)).astype(BF16)
    uv = jnp.transpose(w_uv[l], (1, 0, 2))
    eye = jnp.eye(n_heads, dtype=F32)
    p["wuvp"] = (uv[:, :, None, :] * eye[:, None, :, None]).reshape(n_heads, kv_rank, n_heads * dv).astype(BF16)
    g, n = ssm_a_re.shape[1:]
    abr, abi, bbr_t, bbi_t = _s5_disc_call(ssm_a_re[l], ssm_a_im[l], ssm_log_dt[l],
                                           jnp.transpose(ssm_b_re[l], (0, 2, 1)),
                                           jnp.transpose(ssm_b_im[l], (0, 2, 1)))
    halves = 2
    gh = g // halves
    split = lambda blocks: jnp.stack([_block_diag(blocks[i * gh:(i + 1) * gh]) for i in range(halves)])
    p["abr"] = abr.reshape(1, g * n)
    p["abi"] = abi.reshape(1, g * n)
    p["bbr"] = split(bbr_t).astype(BF16)
    p["bbi"] = split(bbi_t).astype(BF16)
    p["ccr"] = split(jnp.transpose(ssm_c_re[l], (0, 2, 1))).astype(BF16)
    p["cci"] = split(-jnp.transpose(ssm_c_im[l], (0, 2, 1))).astype(BF16)
    p["ssm_d"] = ssm_d[l].reshape(1, d_ssm)
    p["wglu"] = ssm_w_glu[l].astype(BF16)
    d_attn = n_heads * dv
    p["woa"] = w_out[l][:d_attn].astype(BF16)
    p["wos"] = w_out[l][d_attn:].astype(BF16)
    d_model, d_ff = w_gate.shape[1:]
    nch = d_ff // fc
    p["wg"] = jnp.transpose(w_gate[l].reshape(d_model, nch, fc), (1, 0, 2)).astype(BF16)
    p["wu"] = jnp.transpose(w_up[l].reshape(d_model, nch, fc), (1, 0, 2)).astype(BF16)
    p["wd"] = w_down[l].astype(BF16)
    conv_w = ffn_conv_w.shape[1]
    p["cw"] = jnp.transpose(ffn_conv_w[l].reshape(conv_w, nch, fc), (1, 0, 2))
    p["cb"] = ffn_conv_b[l].reshape(nch, 1, fc)
    p["dims"] = dict(q_rank=q_rank, kv_rank=kv_rank, n_heads=n_heads, nope=nope, rope_dim=rope_dim, dv=dv,
                     d_ssm=d_ssm, g=g, n=n, d_ff=d_ff, fc=fc, nch=nch, conv_w=conv_w)
    return p


def _conv_state_in(state, nch, fc):
    b, keep, _ = state.shape
    return jnp.transpose(state.reshape(b, keep, nch, fc), (2, 0, 1, 3))


def _conv_state_out(state):
    nch, b, keep, fc = state.shape
    return jnp.transpose(state, (1, 2, 0, 3)).reshape(b, keep, nch * fc)


def _row(v):
    return v.reshape(1, -1)


def kernel(x_prompt, x_sample, mem_prompt, cache_kv_latent, cache_k_rope, page_table, state_ssm_re, state_ssm_im, state_ffn_conv, cache_mem_k, cache_mem_v, norm_mix_pre, w_in, q_norm, kv_norm, w_uq, w_uk, w_uv, ssm_a_re, ssm_a_im, ssm_log_dt, ssm_b_re, ssm_b_im, ssm_c_re, ssm_c_im, ssm_d, ssm_w_glu, norm_attn_out, norm_ssm_out, w_out, norm_mix_post, norm_mem_pre, mem_norm, w_q_mem, w_k_mem, w_v_mem, w_o_mem, norm_mem_post, norm_ffn_pre, w_gate, w_up, ffn_conv_w, ffn_conv_b, w_down, norm_ffn_post):
    depth = w_in.shape[0]
    b, l, d_model = x_prompt.shape
    db, ls, _ = x_sample.shape
    n_mem = mem_prompt.shape[1]
    mem_heads = cache_mem_k.shape[3]
    d_mem = mem_heads * cache_mem_k.shape[4]
    past_len = page_table.shape[1] * cache_kv_latent.shape[2]
    fc = 256
    tm = min(256, l)
    tq = min(512, l)
    tc = min(128, l)
    pages = min(16, page_table.shape[1])

    xp = x_prompt.reshape(b * l, d_model)
    xs = x_sample.reshape(db * ls, d_model)
    outs = {k: [] for k in ("p_kv", "p_kr", "p_sr", "p_si", "p_cv", "p_mk", "p_mv",
                            "s_kv", "s_kr", "s_sr", "s_si", "s_cv")}
    for li in range(depth):
        p = _layer_weights(li, w_in, q_norm, kv_norm, w_uq, w_uk, w_uv, ssm_a_re, ssm_a_im, ssm_log_dt,
                           ssm_b_re, ssm_b_im, ssm_c_re, ssm_c_im, ssm_d, ssm_w_glu, w_out, w_gate, w_up,
                           w_down, ffn_conv_w, ffn_conv_b, fc)
        dm = p["dims"]
        n_heads, dv, rope_dim, nope = dm["n_heads"], dm["dv"], dm["rope_dim"], dm["nope"]
        g, n, d_ssm = dm["g"], dm["n"], dm["d_ssm"]
        q_scale = (nope + rope_dim) ** -0.5 * LOG2E
        pre_kw = dict(n_heads=n_heads, q_rank=dm["q_rank"], kv_rank=dm["kv_rank"], d_ssm=d_ssm,
                      rope_dim=rope_dim, v_width=n_heads * dv)
        gpre, gq, gkv = _row(norm_mix_pre[li]), _row(q_norm[li]), _row(kv_norm[li])
        mix_consts = (_row(norm_attn_out[li]), _row(norm_ssm_out[li]), p["woa"], p["wos"],
                      _row(norm_mix_post[li]), _row(norm_mem_pre[li]), w_q_mem[li].astype(BF16),
                      w_o_mem[li].astype(BF16), _row(norm_mem_post[li]))
        ffn_consts = (_row(norm_ffn_pre[li]), p["wg"], p["wu"], p["wd"], p["cw"], p["cb"],
                      _row(norm_ffn_post[li]))

        mk, mv = _memkv_call(mem_prompt.reshape(b * n_mem, d_model), _row(mem_norm[li]),
                             w_k_mem[li].astype(BF16), w_v_mem[li].astype(BF16), tm=min(512, b * n_mem),
                             mem_heads=mem_heads)
        tab_p = _rope_tables(jnp.arange(l, dtype=jnp.int32), rope_dim, nope, q_scale)
        q, k, v, ckv, kr, u = _pre_call(xp, tab_p, gpre, gq, gkv, p["win"], p["wq"], p["wkv"], tm=tm,
                                        q_dtype=BF16, **pre_kw)
        attn = _attn_call(q.reshape(b, l, -1), k.reshape(b, l, -1), v.reshape(b, l, -1),
                          n_heads=n_heads, dv=dv, tq=tq)
        zeros_state = jnp.zeros((b, g * n), F32)
        ssm, hr, hi = _s5_call(u.reshape(b, l, d_ssm), zeros_state, zeros_state, p["abr"], p["abi"],
                               p["bbr"], p["bbi"], p["ccr"], p["cci"], p["ssm_d"], p["wglu"], tc=tc)
        xp = _mix_mem_call(xp, attn.reshape(b * l, -1), ssm.reshape(b * l, -1),
                           mk, mv, *mix_consts, tm=tm, rows_per_batch=l, mem_heads=mem_heads, n_mem=n_mem)
        conv0 = jnp.zeros((dm["nch"], b, dm["conv_w"] - 1, fc), F32)
        xp, cv = _ffn_call(xp, conv0, *ffn_consts, nseq=1, lc=tm, n_batch_blocks=b, tiles_per_batch=l // tm)
        outs["p_kv"].append(ckv.reshape(b, l, -1))
        outs["p_kr"].append(kr.reshape(b, l, -1))
        outs["p_sr"].append(hr.reshape(b, g, n))
        outs["p_si"].append(hi.reshape(b, g, n))
        outs["p_cv"].append(_conv_state_out(cv))
        outs["p_mk"].append(mk.reshape(b, n_mem, mem_heads, -1))
        outs["p_mv"].append(mv.reshape(b, n_mem, mem_heads, -1))

        ts = db * ls
        pos_s = past_len + jnp.arange(ls, dtype=jnp.int32)
        tab_s = jnp.tile(_rope_tables(pos_s, rope_dim, nope, q_scale), (1, db, 1))
        q, _, _, ckv, kr, u = _pre_call(xs, tab_s, gpre, gq, gkv, p["win"], p["wq"], p["wkv"], tm=ts,
                                        q_dtype=F32, **pre_kw)
        attn = _dec_attn_call(page_table, q.reshape(db, ls, -1), ckv.reshape(db, ls, -1),
                              kr.reshape(db, ls, -1), p["wukt"], p["wuvp"], cache_kv_latent,
                              jnp.swapaxes(cache_k_rope, 2, 3), layer=li, n_heads=n_heads, dv=dv, pages=pages)
        ssm, hr, hi = _s5_call(u.reshape(db, ls, d_ssm), state_ssm_re[li].reshape(db, g * n),
                               state_ssm_im[li].reshape(db, g * n), p["abr"], p["abi"], p["bbr"], p["bbi"],
                               p["ccr"], p["cci"], p["ssm_d"], p["wglu"], tc=ls)
        xs = _mix_mem_call(xs, attn.reshape(ts, -1), ssm.reshape(ts, -1),
                           cache_mem_k[li].reshape(db * n_mem * mem_heads, -1),
                           cache_mem_v[li].reshape(db * n_mem * mem_heads, -1),
                           *mix_consts, tm=min(4, db) * ls, rows_per_batch=ls, mem_heads=mem_heads, n_mem=n_mem)
        xs, cv = _ffn_call(xs, _conv_state_in(state_ffn_conv[li], dm["nch"], fc), *ffn_consts,
                           nseq=db, lc=ls, n_batch_blocks=1, tiles_per_batch=1)
        outs["s_kv"].append(ckv.reshape(db, ls, -1))
        outs["s_kr"].append(kr.reshape(db, ls, -1))
        outs["s_sr"].append(hr.reshape(db, g, n))
        outs["s_si"].append(hi.reshape(db, g, n))
        outs["s_cv"].append(_conv_state_out(cv))

    st = lambda key: jnp.stack(outs[key])
    return (xp.reshape(b, l, d_model), xs.reshape(db, ls, d_model),
            st("p_kv"), st("p_kr"), st("p_sr"), st("p_si"), st("p_cv"), st("p_mk"), st("p_mv"),
            st("s_kv"), st("s_kr"), st("s_sr"), st("s_si"), st("s_cv"))
```

```python
import functools
import math

import jax
import jax.numpy as jnp
from jax import lax
from jax.experimental import pallas as pl
from jax.experimental.pallas import tpu as pltpu

F32 = jnp.float32
BF16 = jnp.bfloat16

EPS = 1e-6
ROPE_THETA = 10000.0
LANES = 128
NEG_BIG = -1e30
LOG2E = 1.4426950408889634
VMEM_LIMIT = 56 * 1024 * 1024


def _rms(x, g):
    y = x * lax.rsqrt(jnp.mean(x * x, axis=-1, keepdims=True) + EPS)
    return y * g


def _dot(a, b):
    return jnp.dot(a, b, preferred_element_type=F32)


def _dot_nt(a, b):
    return lax.dot_general(a, b, (((1,), (1,)), ((), ())), preferred_element_type=F32)


def _rep_lanes(x, n):
    return jnp.concatenate([x] * n, axis=1) if n > 1 else x


def _const_spec(shape):
    nd = len(shape)
    return pl.BlockSpec(shape, lambda *_: (0,) * nd, pipeline_mode=pl.Buffered(1))


def _params(*sem):
    return pltpu.CompilerParams(dimension_semantics=sem, vmem_limit_bytes=VMEM_LIMIT)


def _pre_body(x_ref, tab_ref, gpre_ref, gq_ref, gkv_ref, win_ref, wq_ref, wkv_ref,
              q_ref, k_ref, v_ref, ckv_ref, kr_ref, u_ref, *, n_heads, q_rank, kv_rank, d_ssm, rope_dim):
    x = x_ref[...]
    h = _rms(x, gpre_ref[...]).astype(BF16)
    z = _dot(h, win_ref[...])
    o1 = q_rank
    o2 = o1 + kv_rank
    o3 = o2 + d_ssm
    o4 = o3 + LANES
    cq, ckv, u = z[:, :o1], z[:, o1:o2], z[:, o2:o3]
    k1, k2 = z[:, o3:o4], z[:, o4:o4 + LANES]
    krs = k1 * tab_ref[0] + k2 * tab_ref[1]
    kr_ref[...] = krs[:, :rope_dim]
    ckv_n = _rms(ckv, gkv_ref[...])
    ckv_ref[...] = ckv_n
    kv2 = _dot(ckv_n.astype(BF16), wkv_ref[...])
    nk = n_heads * LANES
    k_ref[...] = (kv2[:, :nk] + _rep_lanes(krs, n_heads)).astype(k_ref.dtype)
    v_ref[...] = kv2[:, nk:].astype(v_ref.dtype)
    qn = _rms(cq, gq_ref[...]).astype(BF16)
    qq = _dot(qn, wq_ref[...])
    q = qq[:, :nk] * _rep_lanes(tab_ref[2], n_heads) + qq[:, nk:] * _rep_lanes(tab_ref[3], n_heads)
    q_ref[...] = q.astype(q_ref.dtype)
    u_ref[...] = u


def _pre_call(x2, tab, gpre, gq, gkv, win, wq, wkv, *, tm, n_heads, q_rank, kv_rank, d_ssm, rope_dim,
              q_dtype, v_width):
    t, d = x2.shape
    ntab = tab.shape[1] // tm
    nk = n_heads * LANES
    row = lambda w: pl.BlockSpec((tm, w), lambda i: (i, 0))
    body = functools.partial(_pre_body, n_heads=n_heads, q_rank=q_rank, kv_rank=kv_rank, d_ssm=d_ssm,
                             rope_dim=rope_dim)
    return pl.pallas_call(
        body,
        grid=(t // tm,),
        in_specs=[row(d),
                  pl.BlockSpec((4, tm, LANES), lambda i: (0, i % ntab, 0)),
                  _const_spec(gpre.shape), _const_spec(gq.shape), _const_spec(gkv.shape),
                  _const_spec(win.shape), _const_spec(wq.shape), _const_spec(wkv.shape)],
        out_specs=[row(nk), row(nk), row(v_width), row(kv_rank), row(rope_dim), row(d_ssm)],
        out_shape=[jax.ShapeDtypeStruct((t, nk), q_dtype),
                   jax.ShapeDtypeStruct((t, nk), BF16),
                   jax.ShapeDtypeStruct((t, v_width), BF16),
                   jax.ShapeDtypeStruct((t, kv_rank), F32),
                   jax.ShapeDtypeStruct((t, rope_dim), F32),
                   jax.ShapeDtypeStruct((t, d_ssm), F32)],
        compiler_params=_params("parallel"),
        name="pre_proj",
    )(x2, tab, gpre, gq, gkv, win, wq, wkv)


def _attn_body(q_ref, k_ref, v_ref, o_ref, m_ref, l_ref, acc_ref, *, n_heads, tq, tk, dv):
    i = pl.program_id(1)
    j = pl.program_id(2)

    @pl.when(j == 0)
    def _init():
        m_ref[...] = jnp.full(m_ref.shape, NEG_BIG, F32)
        l_ref[...] = jnp.zeros(l_ref.shape, F32)
        acc_ref[...] = jnp.zeros(acc_ref.shape, F32)

    def step(masked):
        for h in range(n_heads):
            qh = q_ref[:, h * LANES:(h + 1) * LANES]
            kh = k_ref[:, h * LANES:(h + 1) * LANES]
            vh = v_ref[:, h * dv:(h + 1) * dv]
            s = _dot_nt(qh, kh)
            if masked:
                row = lax.broadcasted_iota(jnp.int32, (tq, tk), 0)
                col = lax.broadcasted_iota(jnp.int32, (tq, tk), 1)
                s = jnp.where(col <= row, s, NEG_BIG)
            m_prev = m_ref[h]
            m_next = jnp.maximum(m_prev, jnp.max(s, axis=1, keepdims=True))
            alpha = jnp.exp2(m_prev - m_next)
            p = jnp.exp2(s - _rep_lanes(m_next, tk // LANES))
            l_ref[h] = alpha * l_ref[h] + jnp.sum(p, axis=1, keepdims=True)
            m_ref[h] = m_next
            acc_ref[h] = acc_ref[h] * alpha[:, :dv] + _dot(p.astype(BF16), vh)

    @pl.when(j < i)
    def _off_diag():
        step(False)

    @pl.when(j == i)
    def _diag():
        step(True)
        for h in range(n_heads):
            o_ref[:, h * dv:(h + 1) * dv] = acc_ref[h] / l_ref[h][:, :dv]


def _attn_call(q, k, v, *, n_heads, dv, tq):
    b, l, nk = q.shape
    tk = tq
    body = functools.partial(_attn_body, n_heads=n_heads, tq=tq, tk=tk, dv=dv)
    return pl.pallas_call(
        body,
        grid=(b, l // tq, l // tk),
        in_specs=[pl.BlockSpec((None, tq, nk), lambda bb, i, j: (bb, i, 0)),
                  pl.BlockSpec((None, tk, nk), lambda bb, i, j: (bb, jnp.minimum(i, j), 0)),
                  pl.BlockSpec((None, tk, n_heads * dv), lambda bb, i, j: (bb, jnp.minimum(i, j), 0))],
        out_specs=pl.BlockSpec((None, tq, n_heads * dv), lambda bb, i, j: (bb, i, 0)),
        out_shape=jax.ShapeDtypeStruct((b, l, n_heads * dv), F32),
        scratch_shapes=[pltpu.VMEM((n_heads, tq, LANES), F32),
                        pltpu.VMEM((n_heads, tq, LANES), F32),
                        pltpu.VMEM((n_heads, tq, dv), F32)],
        compiler_params=_params("parallel", "parallel", "arbitrary"),
        name="mla_prompt_attn",
    )(q, k, v)


def _dec_attn_body(pt_ref, q_ref, ckv_ref, kr_ref, wukt_ref, wuvp_ref, lat_hbm, pe_hbm, o_ref,
                   lat_buf, pe_buf, sems, ql_s, qp_s, m_s, l_s, acc_s,
                   *, n_heads, s_len, rope_dim, pages, page, groups, key_block):
    b = pl.program_id(0)
    nb = pl.num_programs(0)
    rows = n_heads * s_len

    def page_copies(group, slot, real):
        out = []
        for r in range(pages):
            pg = pt_ref[group * pages + r] if real else 0
            out.append(pltpu.make_async_copy(lat_hbm.at[pg], lat_buf.at[slot, pl.ds(r * page, page), :],
                                             sems.at[0, slot]))
            out.append(pltpu.make_async_copy(pe_hbm.at[pg], pe_buf.at[slot, :, pl.ds(r * page, page)],
                                             sems.at[1, slot]))
        return out

    def issue(group, slot):
        for c in page_copies(group, slot, True):
            c.start()

    def wait(slot):
        for c in page_copies(0, slot, False):
            c.wait()

    @pl.when(b == 0)
    def _prime():
        issue(0, 0)
        issue(1, 1)

    for h in range(n_heads):
        qs = q_ref[:, h * LANES:(h + 1) * LANES]
        ql_s[h * s_len:(h + 1) * s_len, :] = _dot(qs.astype(BF16), wukt_ref[h])
        qp_s[h * s_len:(h + 1) * s_len, :] = qs[:, :rope_dim]
    m_s[...] = jnp.full(m_s.shape, NEG_BIG, F32)
    l_s[...] = jnp.zeros(l_s.shape, F32)
    acc_s[...] = jnp.zeros(acc_s.shape, F32)
    ql = ql_s[...].astype(BF16)
    qp = qp_s[...].astype(BF16)

    def probs(s):
        m_b = jnp.max(s, axis=1, keepdims=True)
        p = jnp.exp2(s - m_b)
        return m_b, jnp.sum(p, axis=1, keepdims=True), p.astype(BF16)

    def merge(parts):
        m_prev = m_s[...]
        m_next = m_prev
        for m_b, _, _ in parts:
            m_next = jnp.maximum(m_next, m_b)
        w = jnp.exp2(m_prev - m_next)
        l = l_s[...] * w
        acc = acc_s[...] * w
        for m_b, l_b, o_b in parts:
            w = jnp.exp2(m_b - m_next)
            l = l + l_b * w
            acc = acc + o_b * w
        m_s[...] = m_next
        l_s[...] = l
        acc_s[...] = acc

    for j in range(groups):
        slot = j % 2
        wait(slot)
        lats, scores = [], []
        for kb in range(pages * page // key_block):
            keys = pl.ds(kb * key_block, key_block)
            lat = lat_buf[slot, keys, :].astype(BF16)
            pe_t = pe_buf[slot, :, keys].astype(BF16)
            lats.append(lat)
            scores.append(_dot_nt(ql, lat) + _dot(qp, pe_t))
        stats = [probs(s) for s in scores]
        merge([(m_b, l_b, _dot(p, lat)) for (m_b, l_b, p), lat in zip(stats, lats)])
        nxt = b * groups + j + 2
        if j + 2 < groups:
            issue(nxt, slot)
        else:
            @pl.when(b + 1 < nb)
            def _next_batch(nxt=nxt, slot=slot):
                issue(nxt, slot)

    pad = LANES - s_len
    new_lat = jnp.concatenate([ckv_ref[...], jnp.zeros((pad, ckv_ref.shape[1]), F32)], axis=0).astype(BF16)
    new_pe = jnp.concatenate([kr_ref[...], jnp.zeros((pad, rope_dim), F32)], axis=0).astype(BF16)
    s = _dot_nt(ql, new_lat) + _dot_nt(qp, new_pe)
    row = lax.broadcasted_iota(jnp.int32, (rows, LANES), 0)
    col = lax.broadcasted_iota(jnp.int32, (rows, LANES), 1)
    m_b, l_b, p = probs(jnp.where(col <= row % s_len, s, NEG_BIG))
    merge([(m_b, l_b, _dot(p, new_lat))])

    o_lat = (acc_s[...] / l_s[...]).astype(BF16)
    out = _dot(o_lat[0:s_len], wuvp_ref[0])
    for h in range(1, n_heads):
        out = out + _dot(o_lat[h * s_len:(h + 1) * s_len], wuvp_ref[h])
    o_ref[...] = out


def _dec_attn_call(page_table, q, ckv, kr, wukt, wuvp, cache_lat, cache_pe_t, *, n_heads, dv, pages):
    db, s_len, nk = q.shape
    kv_rank = ckv.shape[-1]
    rope_dim = kr.shape[-1]
    n_pages = page_table.shape[1]
    page = cache_lat.shape[1]
    groups = n_pages // pages
    key_block = min(4, pages) * page
    assert n_pages % pages == 0 and groups % 2 == 0 and (pages * page) % key_block == 0
    rows = n_heads * s_len
    body = functools.partial(_dec_attn_body, n_heads=n_heads, s_len=s_len, rope_dim=rope_dim, pages=pages,
                             page=page, groups=groups, key_block=key_block)
    grid_spec = pltpu.PrefetchScalarGridSpec(
        num_scalar_prefetch=1,
        grid=(db,),
        in_specs=[pl.BlockSpec((None, s_len, nk), lambda b, pt: (b, 0, 0)),
                  pl.BlockSpec((None, s_len, kv_rank), lambda b, pt: (b, 0, 0)),
                  pl.BlockSpec((None, s_len, rope_dim), lambda b, pt: (b, 0, 0)),
                  _const_spec(wukt.shape), _const_spec(wuvp.shape),
                  pl.BlockSpec(memory_space=pl.ANY), pl.BlockSpec(memory_space=pl.ANY)],
        out_specs=pl.BlockSpec((None, s_len, n_heads * dv), lambda b, pt: (b, 0, 0)),
        scratch_shapes=[pltpu.VMEM((2, pages * page, kv_rank), F32),
                        pltpu.VMEM((2, rope_dim, pages * page), F32),
                        pltpu.SemaphoreType.DMA((2, 2)),
                        pltpu.VMEM((rows, kv_rank), F32),
                        pltpu.VMEM((rows, rope_dim), F32),
                        pltpu.VMEM((rows, 1), F32),
                        pltpu.VMEM((rows, 1), F32),
                        pltpu.VMEM((rows, kv_rank), F32)])
    return pl.pallas_call(
        body,
        grid_spec=grid_spec,
        out_shape=jax.ShapeDtypeStruct((db, s_len, n_heads * dv), F32),
        compiler_params=_params("arbitrary"),
        name="mla_sample_attn",
    )(page_table.reshape(-1), q, ckv, kr, wukt, wuvp, cache_lat, cache_pe_t)


def _s5_disc_body(are_ref, aim_ref, ldt_ref, bre_ref, bim_ref, abr_ref, abi_ref, bbr_ref, bbi_ref):
    a_re = are_ref[...]
    a_im = aim_ref[...]
    dt = jnp.exp(ldt_ref[...])
    mag = jnp.exp(dt * a_re)
    abr = mag * jnp.cos(dt * a_im)
    abi = mag * jnp.sin(dt * a_im)
    den = a_re * a_re + a_im * a_im
    nr, ni = abr - 1.0, abi
    fr = (nr * a_re + ni * a_im) / den
    fi = (ni * a_re - nr * a_im) / den
    abr_ref[...] = abr
    abi_ref[...] = abi
    b_re = bre_ref[...]
    b_im = bim_ref[...]
    bbr_ref[...] = fr[:, None, :] * b_re - fi[:, None, :] * b_im
    bbi_ref[...] = fr[:, None, :] * b_im + fi[:, None, :] * b_re


def _s5_disc_call(a_re, a_im, log_dt, b_re_t, b_im_t):
    g, n = a_re.shape
    c = b_re_t.shape[1]
    return pl.pallas_call(
        _s5_disc_body,
        out_shape=[jax.ShapeDtypeStruct((g, n), F32), jax.ShapeDtypeStruct((g, n), F32),
                   jax.ShapeDtypeStruct((g, c, n), F32), jax.ShapeDtypeStruct((g, c, n), F32)],
        name="s5_discretise",
    )(a_re, a_im, log_dt.reshape(g, 1), b_re_t, b_im_t)


def _s5_body(u_ref, h0r_ref, h0i_ref, ar_ref, ai_ref, bbr_ref, bbi_ref, ccr_ref, cci_ref, d_ref, wglu_ref,
             y_ref, hr_out, hi_out, bur, bui, hr_s, hi_s, *, nb, tc, lane_tiles, unroll):
    c = pl.program_id(0)
    d_ssm = u_ref.shape[-1]
    n_state = hr_s.shape[-1]
    halves = bbr_ref.shape[0]
    ch_half = d_ssm // halves
    st_half = n_state // halves

    @pl.when(c == 0)
    def _init():
        hr_s[...] = h0r_ref[...]
        hi_s[...] = h0i_ref[...]

    u = jnp.swapaxes(u_ref[...], 0, 1).reshape(tc * nb, d_ssm)
    ub = u.astype(BF16)
    tiles_half = st_half // LANES
    for hf in range(halves):
        ublk = ub[:, hf * ch_half:(hf + 1) * ch_half]
        br = _dot(ublk, bbr_ref[hf])
        bi = _dot(ublk, bbi_ref[hf])
        for k in range(tiles_half):
            bur[hf * tiles_half + k] = br[:, k * LANES:(k + 1) * LANES]
            bui[hf * tiles_half + k] = bi[:, k * LANES:(k + 1) * LANES]

    for k0 in range(0, n_state // LANES, lane_tiles):
        tiles = range(k0, k0 + lane_tiles)
        ar = [jnp.broadcast_to(ar_ref[:, k * LANES:(k + 1) * LANES], (nb, LANES)) for k in tiles]
        ai = [jnp.broadcast_to(ai_ref[:, k * LANES:(k + 1) * LANES], (nb, LANES)) for k in tiles]

        def body(t, carry, tiles=tiles, ar=ar, ai=ai):
            rows = pl.ds(pl.multiple_of(t * nb, nb), nb)
            out = []
            for n, k in enumerate(tiles):
                hr, hi = carry[2 * n], carry[2 * n + 1]
                nr = ar[n] * hr - ai[n] * hi + bur[k, rows, :]
                ni = ar[n] * hi + ai[n] * hr + bui[k, rows, :]
                bur[k, rows, :] = nr
                bui[k, rows, :] = ni
                out += [nr, ni]
            return tuple(out)

        init = []
        for k in tiles:
            init += [hr_s[:, k * LANES:(k + 1) * LANES], hi_s[:, k * LANES:(k + 1) * LANES]]
        fin = lax.fori_loop(0, tc, body, tuple(init), unroll=unroll)
        for n, k in enumerate(tiles):
            hr_s[:, k * LANES:(k + 1) * LANES] = fin[2 * n]
            hi_s[:, k * LANES:(k + 1) * LANES] = fin[2 * n + 1]

    hr_out[...] = hr_s[...]
    hi_out[...] = hi_s[...]

    ys = []
    for hf in range(halves):
        hr_hist = jnp.concatenate([bur[hf * tiles_half + k] for k in range(tiles_half)], axis=1).astype(BF16)
        hi_hist = jnp.concatenate([bui[hf * tiles_half + k] for k in range(tiles_half)], axis=1).astype(BF16)
        ys.append(_dot(hr_hist, ccr_ref[hf]) + _dot(hi_hist, cci_ref[hf]))
    y = jnp.concatenate(ys, axis=1) + d_ref[...] * u
    g = jax.nn.gelu(y)
    out = g * jax.nn.sigmoid(_dot(g.astype(BF16), wglu_ref[...]))
    y_ref[...] = jnp.swapaxes(out.reshape(tc, nb, d_ssm), 0, 1)


def _s5_call(u3, h0r, h0i, ar, ai, bbr, bbi, ccr, cci, d, wglu, *, tc):
    nb, l, d_ssm = u3.shape
    n_state = ar.shape[-1]
    lane_tiles = math.gcd(n_state // LANES, max(1, 64 // nb))
    body = functools.partial(_s5_body, nb=nb, tc=tc, lane_tiles=lane_tiles, unroll=min(tc, 8))
    return pl.pallas_call(
        body,
        grid=(l // tc,),
        in_specs=[pl.BlockSpec((nb, tc, d_ssm), lambda c: (0, c, 0)),
                  _const_spec(h0r.shape), _const_spec(h0i.shape),
                  _const_spec(ar.shape), _const_spec(ai.shape),
                  _const_spec(bbr.shape), _const_spec(bbi.shape),
                  _const_spec(ccr.shape), _const_spec(cci.shape),
                  _const_spec(d.shape), _const_spec(wglu.shape)],
        out_specs=[pl.BlockSpec((nb, tc, d_ssm), lambda c: (0, c, 0)),
                   pl.BlockSpec((nb, n_state), lambda c: (0, 0)),
                   pl.BlockSpec((nb, n_state), lambda c: (0, 0))],
        out_shape=[jax.ShapeDtypeStruct((nb, l, d_ssm), F32),
                   jax.ShapeDtypeStruct((nb, n_state), F32),
                   jax.ShapeDtypeStruct((nb, n_state), F32)],
        scratch_shapes=[pltpu.VMEM((n_state // LANES, nb * tc, LANES), F32),
                        pltpu.VMEM((n_state // LANES, nb * tc, LANES), F32),
                        pltpu.VMEM((nb, n_state), F32), pltpu.VMEM((nb, n_state), F32)],
        compiler_params=_params("arbitrary"),
        name="s5_scan_glu",
    )(u3, h0r, h0i, ar, ai, bbr, bbi, ccr, cci, d, wglu)


def _memkv_body(m_ref, g_ref, wk_ref, wv_ref, k_ref, v_ref):
    m = _rms(m_ref[...], g_ref[...]).astype(BF16)
    k_ref[...] = _dot(m, wk_ref[...]).reshape(k_ref.shape)
    v_ref[...] = _dot(m, wv_ref[...]).reshape(v_ref.shape)


def _memkv_call(mem2, g, wk, wv, *, tm, mem_heads):
    t, d = mem2.shape
    hd = wk.shape[1] // mem_heads
    out = pl.BlockSpec((tm * mem_heads, hd), lambda i: (i, 0))
    return pl.pallas_call(
        _memkv_body,
        grid=(t // tm,),
        in_specs=[pl.BlockSpec((tm, d), lambda i: (i, 0)),
                  _const_spec(g.shape), _const_spec(wk.shape), _const_spec(wv.shape)],
        out_specs=[out, out],
        out_shape=[jax.ShapeDtypeStruct((t * mem_heads, hd), F32)] * 2,
        compiler_params=_params("parallel"),
        name="mem_kv",
    )(mem2, g, wk, wv)


def _mix_mem_body(x_ref, a_ref, s_ref, mk_ref, mv_ref, gao_ref, gso_ref, woa_ref, wos_ref, gmp_ref,
                  gmem_ref, wq_ref, wo_ref, gmo_ref, o_ref, *, mem_heads, n_mem, seqs, sub, mem_scale):
    def head_rows(ref, si, hh):
        return ref[pl.ds((si * n_mem) * mem_heads + hh, n_mem, stride=mem_heads), :].astype(BF16)

    mem = [[(head_rows(mk_ref, si, hh), head_rows(mv_ref, si, hh)) for hh in range(mem_heads)]
           for si in range(seqs)]
    tile = x_ref.shape[0] // sub
    for t in range(sub):
        r = pl.ds(t * tile, tile)
        x = x_ref[r, :]
        a = _rms(a_ref[r, :], gao_ref[...]).astype(BF16)
        s = _rms(s_ref[r, :], gso_ref[...]).astype(BF16)
        mix = _dot(a, woa_ref[...]) + _dot(s, wos_ref[...])
        x = x + _rms(mix, gmp_ref[...])
        h = _rms(x, gmem_ref[...]).astype(BF16)
        q = _dot(h, wq_ref[...])
        hd = q.shape[1] // mem_heads
        rows = tile // seqs
        outs = []
        for si in range(seqs):
            heads = []
            for hh in range(mem_heads):
                kh, vh = mem[si][hh]
                qh = q[si * rows:(si + 1) * rows, hh * hd:(hh + 1) * hd].astype(BF16)
                sc = _dot_nt(qh, kh) * mem_scale
                e = jnp.exp(sc - jnp.max(sc, axis=1, keepdims=True))
                p = e / jnp.sum(e, axis=1, keepdims=True)
                heads.append(_dot(p.astype(BF16), vh))
            outs.append(jnp.concatenate(heads, axis=1))
        o = (jnp.concatenate(outs, axis=0) if seqs > 1 else outs[0]).astype(BF16)
        o_ref[r, :] = x + _rms(_dot(o, wo_ref[...]), gmo_ref[...])


def _mix_mem_call(x2, a2, s2, mk, mv, gao, gso, woa, wos, gmp, gmem, wq, wo, gmo, *, tm, rows_per_batch,
                  mem_heads, n_mem):
    t, d = x2.shape
    da, ds = a2.shape[1], s2.shape[1]
    hd = mk.shape[1]
    seqs = max(1, tm // rows_per_batch)
    tiles_per_batch = max(1, rows_per_batch // tm)
    sub = 2 if (seqs == 1 and tm % 512 == 0) else 1
    body = functools.partial(_mix_mem_body, mem_heads=mem_heads, n_mem=n_mem, seqs=seqs, sub=sub,
                             mem_scale=hd ** -0.5)
    row = lambda w: pl.BlockSpec((tm, w), lambda i: (i, 0))
    mem = pl.BlockSpec((seqs * n_mem * mem_heads, hd), lambda i: (i // tiles_per_batch, 0))
    consts = [gao, gso, woa, wos, gmp, gmem, wq, wo, gmo]
    return pl.pallas_call(
        body,
        grid=(t // tm,),
        in_specs=[row(d), row(da), row(ds), mem, mem] + [_const_spec(c.shape) for c in consts],
        out_specs=row(d),
        out_shape=jax.ShapeDtypeStruct((t, d), F32),
        compiler_params=_params("parallel"),
        name="mix_out_mem_attn",
    )(x2, a2, s2, mk, mv, *consts)


def _ffn_body(x_ref, cprev_ref, gpre_ref, wg_ref, wu_ref, wd_ref, cw_ref, cb_ref, gpost_ref,
              o_ref, cnew_ref, halo, work, act, *, nseq, lc, halo_rows):
    t = pl.program_id(1)
    n_chunks = wg_ref.shape[0]
    tm = nseq * lc
    fc = wg_ref.shape[2]
    keep = cprev_ref.shape[2]
    lo = halo_rows - keep

    @pl.when(t == 0)
    def _load_state():
        halo[...] = cprev_ref[...]

    x = x_ref[...]
    h = _rms(x, gpre_ref[...]).astype(BF16)

    for c in range(n_chunks):
        g = _dot(h, wg_ref[c]).reshape(nseq, lc, fc)
        up = _dot(h, wu_ref[c]).reshape(nseq, lc, fc)
        work[c, :, lo:halo_rows, :] = halo[c]
        work[c, :, halo_rows:halo_rows + lc, :] = g
        w = cw_ref[c]
        conv = w[0:1, :] * work[c, :, lo:lo + lc, :]
        for k in range(1, keep):
            conv = conv + w[k:k + 1, :] * work[c, :, lo + k:lo + k + lc, :]
        conv = conv + w[keep:keep + 1, :] * g
        gc = cb_ref[c] + conv
        act[:, c * fc:(c + 1) * fc] = (jax.nn.silu(gc) * up).reshape(tm, fc).astype(BF16)
        tail = work[c, :, lc + lo:lc + halo_rows, :]
        halo[c] = tail
        cnew_ref[c] = tail

    o_ref[...] = x + _rms(_dot(act[...], wd_ref[...]), gpost_ref[...])


def _ffn_call(x2, cprev, gpre, wg, wu, wd, cw, cb, gpost, *, nseq, lc, n_batch_blocks, tiles_per_batch):
    t, d = x2.shape
    n_chunks, _, fc = wg.shape
    keep = cprev.shape[2]
    halo_rows = 8
    tm = nseq * lc
    body = functools.partial(_ffn_body, nseq=nseq, lc=lc, halo_rows=halo_rows)
    state = pl.BlockSpec((n_chunks, nseq, keep, fc), lambda b, i: (0, b, 0, 0))
    return pl.pallas_call(
        body,
        grid=(n_batch_blocks, tiles_per_batch),
        in_specs=[pl.BlockSpec((tm, d), lambda b, i: (b * tiles_per_batch + i, 0)),
                  state,
                  _const_spec(gpre.shape), _const_spec(wg.shape), _const_spec(wu.shape),
                  _const_spec(wd.shape), _const_spec(cw.shape), _const_spec(cb.shape),
                  _const_spec(gpost.shape)],
        out_specs=[pl.BlockSpec((tm, d), lambda b, i: (b * tiles_per_batch + i, 0)), state],
        out_shape=[jax.ShapeDtypeStruct((t, d), F32),
                   jax.ShapeDtypeStruct(cprev.shape, F32)],
        scratch_shapes=[pltpu.VMEM((n_chunks, nseq, keep, fc), F32),
                        pltpu.VMEM((n_chunks, nseq, halo_rows + lc, fc), F32),
                        pltpu.VMEM((tm, n_chunks * fc), BF16)],
        compiler_params=_params("parallel", "arbitrary"),
        name="conv_ffn",
    )(x2, cprev, gpre, wg, wu, wd, cw, cb, gpost)


def _rope_tables(pos, rope_dim, nope_dim, q_scale):
    half = rope_dim // 2
    inv = ROPE_THETA ** (-jnp.arange(half, dtype=F32) * (2.0 / rope_dim))
    ang = pos.astype(F32)[:, None] * inv[None, :]
    cos, sin = jnp.cos(ang), jnp.sin(ang)
    n = pos.shape[0]
    pad = jnp.zeros((n, LANES - rope_dim), F32)
    ck = jnp.concatenate([cos, cos, pad], axis=1)
    sk = jnp.concatenate([sin, sin, pad], axis=1)
    ones = jnp.concatenate([jnp.ones((n, nope_dim), F32), jnp.zeros((n, LANES - rope_dim - nope_dim), F32)], axis=1)
    cq = jnp.concatenate([cos, cos, ones], axis=1) * q_scale
    return jnp.stack([ck, sk, cq, sk * q_scale])


def _rot_half_cols(w):
    half = w.shape[-1] // 2
    return jnp.concatenate([-w[..., half:], w[..., :half]], axis=-1)


def _pad_last(w, width):
    return jnp.pad(w, [(0, 0)] * (w.ndim - 1) + [(0, width - w.shape[-1])])


def _block_diag(blocks):
    g, r, c = blocks.shape
    eye = jnp.eye(g, dtype=blocks.dtype)
    return (blocks[:, :, None, :] * eye[:, None, :, None]).reshape(g * r, g * c)


def _layer_weights(l, w_in, q_norm, kv_norm, w_uq, w_uk, w_uv, ssm_a_re, ssm_a_im, ssm_log_dt, ssm_b_re,
                   ssm_b_im, ssm_c_re, ssm_c_im, ssm_d, ssm_w_glu, w_out, w_gate, w_up, w_down, ffn_conv_w,
                   ffn_conv_b, fc):
    q_rank = q_norm.shape[-1]
    kv_rank = kv_norm.shape[-1]
    n_heads = w_uq.shape[2]
    nope = w_uk.shape[3]
    rope_dim = w_uq.shape[3] - nope
    dv = w_uv.shape[3]
    d_ssm = ssm_d.shape[-1]
    win = w_in[l]
    o1, o2, o3 = q_rank, q_rank + kv_rank, q_rank + kv_rank + rope_dim
    w_kr = win[:, o2:o3]
    p = {}
    p["win"] = jnp.concatenate([win[:, :o2], win[:, o3:], _pad_last(w_kr, LANES),
                                _pad_last(_rot_half_cols(w_kr), LANES)], axis=1).astype(BF16)
    uq = w_uq[l]
    q_nope, q_pe = uq[..., :nope], uq[..., nope:]
    wq1 = _pad_last(jnp.concatenate([q_pe, q_nope], axis=-1), LANES).reshape(q_rank, n_heads * LANES)
    wq2 = _pad_last(_rot_half_cols(q_pe), LANES).reshape(q_rank, n_heads * LANES)
    p["wq"] = jnp.concatenate([wq1, wq2], axis=1).astype(BF16)
    uk = w_uk[l]
    wuk_slots = jnp.pad(uk, ((0, 0), (0, 0), (rope_dim, LANES - rope_dim - nope)))
    p["wkv"] = jnp.concatenate([wuk_slots.reshape(kv_rank, n_heads * LANES),
                                w_uv[l].reshape(kv_rank, n_heads * dv)], axis=1).astype(BF16)
    p["wukt"] = jnp.transpose(wuk_slots, (1, 2, 0)).astype(BF16)
    uv = jnp.transpose(w_uv[l], (1, 0, 2))
    eye = jnp.eye(n_heads, dtype=F32)
    p["wuvp"] = (uv[:, :, None, :] * eye[:, None, :, None]).reshape(n_heads, kv_rank, n_heads * dv).astype(BF16)
    g, n = ssm_a_re.shape[1:]
    abr, abi, bbr_t, bbi_t = _s5_disc_call(ssm_a_re[l], ssm_a_im[l], ssm_log_dt[l],
                                           jnp.transpose(ssm_b_re[l], (0, 2, 1)),
                                           jnp.transpose(ssm_b_im[l], (0, 2, 1)))
    halves = 2
    gh = g // halves
    split = lambda blocks: jnp.stack([_block_diag(blocks[i * gh:(i + 1) * gh]) for i in range(halves)])
    p["abr"] = abr.reshape(1, g * n)
    p["abi"] = abi.reshape(1, g * n)
    p["bbr"] = split(bbr_t).astype(BF16)
    p["bbi"] = split(bbi_t).astype(BF16)
    p["ccr"] = split(jnp.transpose(ssm_c_re[l], (0, 2, 1))).astype(BF16)
    p["cci"] = split(-jnp.transpose(ssm_c_im[l], (0, 2, 1))).astype(BF16)
    p["ssm_d"] = ssm_d[l].reshape(1, d_ssm)
    p["wglu"] = ssm_w_glu[l].astype(BF16)
    d_attn = n_heads * dv
    p["woa"] = w_out[l][:d_attn].astype(BF16)
    p["wos"] = w_out[l][d_attn:].astype(BF16)
    d_model, d_ff = w_gate.shape[1:]
    nch = d_ff // fc
    p["wg"] = jnp.transpose(w_gate[l].reshape(d_model, nch, fc), (1, 0, 2)).astype(BF16)
    p["wu"] = jnp.transpose(w_up[l].reshape(d_model, nch, fc), (1, 0, 2)).astype(BF16)
    p["wd"] = w_down[l].astype(BF16)
    conv_w = ffn_conv_w.shape[1]
    p["cw"] = jnp.transpose(ffn_conv_w[l].reshape(conv_w, nch, fc), (1, 0, 2))
    p["cb"] = ffn_conv_b[l].reshape(nch, 1, fc)
    p["dims"] = dict(q_rank=q_rank, kv_rank=kv_rank, n_heads=n_heads, nope=nope, rope_dim=rope_dim, dv=dv,
                     d_ssm=d_ssm, g=g, n=n, d_ff=d_ff, fc=fc, nch=nch, conv_w=conv_w)
    return p


def _conv_state_in(state, nch, fc):
    b, keep, _ = state.shape
    return jnp.transpose(state.reshape(b, keep, nch, fc), (2, 0, 1, 3))


def _conv_state_out(state):
    nch, b, keep, fc = state.shape
    return jnp.transpose(state, (1, 2, 0, 3)).reshape(b, keep, nch * fc)


def _row(v):
    return v.reshape(1, -1)


def kernel(x_prompt, x_sample, mem_prompt, cache_kv_latent, cache_k_rope, page_table, state_ssm_re, state_ssm_im, state_ffn_conv, cache_mem_k, cache_mem_v, norm_mix_pre, w_in, q_norm, kv_norm, w_uq, w_uk, w_uv, ssm_a_re, ssm_a_im, ssm_log_dt, ssm_b_re, ssm_b_im, ssm_c_re, ssm_c_im, ssm_d, ssm_w_glu, norm_attn_out, norm_ssm_out, w_out, norm_mix_post, norm_mem_pre, mem_norm, w_q_mem, w_k_mem, w_v_mem, w_o_mem, norm_mem_post, norm_ffn_pre, w_gate, w_up, ffn_conv_w, ffn_conv_b, w_down, norm_ffn_post):
    depth = w_in.shape[0]
    b, l, d_model = x_prompt.shape
    db, ls, _ = x_sample.shape
    n_mem = mem_prompt.shape[1]
    mem_heads = cache_mem_k.shape[3]
    past_len = page_table.shape[1] * cache_kv_latent.shape[2]
    fc = 256
    tm = min(256, l)
    tm_wide = min(512, l)
    tq = min(512, l)
    tc = min(128, l)
    pages = min(32, page_table.shape[1] // 2)

    xp = x_prompt.reshape(b * l, d_model)
    xs = x_sample.reshape(db * ls, d_model)
    outs = {k: [] for k in ("p_kv", "p_kr", "p_sr", "p_si", "p_cv", "p_mk", "p_mv",
                            "s_kv", "s_kr", "s_sr", "s_si", "s_cv")}
    for li in range(depth):
        p = _layer_weights(li, w_in, q_norm, kv_norm, w_uq, w_uk, w_uv, ssm_a_re, ssm_a_im, ssm_log_dt,
                           ssm_b_re, ssm_b_im, ssm_c_re, ssm_c_im, ssm_d, ssm_w_glu, w_out, w_gate, w_up,
                           w_down, ffn_conv_w, ffn_conv_b, fc)
        dm = p["dims"]
        n_heads, dv, rope_dim, nope = dm["n_heads"], dm["dv"], dm["rope_dim"], dm["nope"]
        g, n, d_ssm = dm["g"], dm["n"], dm["d_ssm"]
        q_scale = (nope + rope_dim) ** -0.5 * LOG2E
        pre_kw = dict(n_heads=n_heads, q_rank=dm["q_rank"], kv_rank=dm["kv_rank"], d_ssm=d_ssm,
                      rope_dim=rope_dim, v_width=n_heads * dv)
        gpre, gq, gkv = _row(norm_mix_pre[li]), _row(q_norm[li]), _row(kv_norm[li])
        mix_consts = (_row(norm_attn_out[li]), _row(norm_ssm_out[li]), p["woa"], p["wos"],
                      _row(norm_mix_post[li]), _row(norm_mem_pre[li]), w_q_mem[li].astype(BF16),
                      w_o_mem[li].astype(BF16), _row(norm_mem_post[li]))
        ffn_consts = (_row(norm_ffn_pre[li]), p["wg"], p["wu"], p["wd"], p["cw"], p["cb"],
                      _row(norm_ffn_post[li]))

        mk, mv = _memkv_call(mem_prompt.reshape(b * n_mem, d_model), _row(mem_norm[li]),
                             w_k_mem[li].astype(BF16), w_v_mem[li].astype(BF16), tm=min(512, b * n_mem),
                             mem_heads=mem_heads)
        tab_p = _rope_tables(jnp.arange(l, dtype=jnp.int32), rope_dim, nope, q_scale)
        q, k, v, ckv, kr, u = _pre_call(xp, tab_p, gpre, gq, gkv, p["win"], p["wq"], p["wkv"], tm=tm_wide,
                                        q_dtype=BF16, **pre_kw)
        attn = _attn_call(q.reshape(b, l, -1), k.reshape(b, l, -1), v.reshape(b, l, -1),
                          n_heads=n_heads, dv=dv, tq=tq)
        zeros_state = jnp.zeros((b, g * n), F32)
        ssm, hr, hi = _s5_call(u.reshape(b, l, d_ssm), zeros_state, zeros_state, p["abr"], p["abi"],
                               p["bbr"], p["bbi"], p["ccr"], p["cci"], p["ssm_d"], p["wglu"], tc=tc)
        xp = _mix_mem_call(xp, attn.reshape(b * l, -1), ssm.reshape(b * l, -1),
                           mk, mv, *mix_consts, tm=tm_wide, rows_per_batch=l, mem_heads=mem_heads, n_mem=n_mem)
        conv0 = jnp.zeros((dm["nch"], b, dm["conv_w"] - 1, fc), F32)
        xp, cv = _ffn_call(xp, conv0, *ffn_consts, nseq=1, lc=tm, n_batch_blocks=b, tiles_per_batch=l // tm)
        outs["p_kv"].append(ckv.reshape(b, l, -1))
        outs["p_kr"].append(kr.reshape(b, l, -1))
        outs["p_sr"].append(hr.reshape(b, g, n))
        outs["p_si"].append(hi.reshape(b, g, n))
        outs["p_cv"].append(_conv_state_out(cv))
        outs["p_mk"].append(mk.reshape(b, n_mem, mem_heads, -1))
        outs["p_mv"].append(mv.reshape(b, n_mem, mem_heads, -1))

        ts = db * ls
        pos_s = past_len + jnp.arange(ls, dtype=jnp.int32)
        tab_s = jnp.tile(_rope_tables(pos_s, rope_dim, nope, q_scale), (1, db, 1))
        q, _, _, ckv, kr, u = _pre_call(xs, tab_s, gpre, gq, gkv, p["win"], p["wq"], p["wkv"], tm=ts,
                                        q_dtype=F32, **pre_kw)
        attn = _dec_attn_call(page_table, q.reshape(db, ls, -1), ckv.reshape(db, ls, -1),
                              kr.reshape(db, ls, -1), p["wukt"], p["wuvp"], cache_kv_latent[li],
                              jnp.swapaxes(cache_k_rope[li], 1, 2), n_heads=n_heads, dv=dv, pages=pages)
        ssm, hr, hi = _s5_call(u.reshape(db, ls, d_ssm), state_ssm_re[li].reshape(db, g * n),
                               state_ssm_im[li].reshape(db, g * n), p["abr"], p["abi"], p["bbr"], p["bbi"],
                               p["ccr"], p["cci"], p["ssm_d"], p["wglu"], tc=ls)
        xs = _mix_mem_call(xs, attn.reshape(ts, -1), ssm.reshape(ts, -1),
                           cache_mem_k[li].reshape(db * n_mem * mem_heads, -1),
                           cache_mem_v[li].reshape(db * n_mem * mem_heads, -1),
                           *mix_consts, tm=min(4, db) * ls, rows_per_batch=ls, mem_heads=mem_heads, n_mem=n_mem)
        xs, cv = _ffn_call(xs, _conv_state_in(state_ffn_conv[li], dm["nch"], fc), *ffn_consts,
                           nseq=db, lc=ls, n_batch_blocks=1, tiles_per_batch=1)
        outs["s_kv"].append(ckv.reshape(db, ls, -1))
        outs["s_kr"].append(kr.reshape(db, ls, -1))
        outs["s_sr"].append(hr.reshape(db, g, n))
        outs["s_si"].append(hi.reshape(db, g, n))
        outs["s_cv"].append(_conv_state_out(cv))

    st = lambda key: jnp.stack(outs[key])
    return (xp.reshape(b, l, d_model), xs.reshape(db, ls, d_model),
            st("p_kv"), st("p_kr"), st("p_sr"), st("p_si"), st("p_cv"), st("p_mk"), st("p_mv"),
            st("s_kv"), st("s_kr"), st("s_sr"), st("s_si"), st("s_cv"))
```

```python
import functools
import math

import jax
import jax.numpy as jnp
from jax import lax
from jax.experimental import pallas as pl
from jax.experimental.pallas import tpu as pltpu

F32 = jnp.float32
BF16 = jnp.bfloat16

EPS = 1e-6
ROPE_THETA = 10000.0
LANES = 128
NEG_BIG = -1e30
LOG2E = 1.4426950408889634
VMEM_LIMIT = 56 * 1024 * 1024


def _rms(x, g):
    y = x * lax.rsqrt(jnp.mean(x * x, axis=-1, keepdims=True) + EPS)
    return y * g


def _dot(a, b):
    return jnp.dot(a, b, preferred_element_type=F32)


def _dot_nt(a, b):
    return lax.dot_general(a, b, (((1,), (1,)), ((), ())), preferred_element_type=F32)


def _rep_lanes(x, n):
    return jnp.concatenate([x] * n, axis=1) if n > 1 else x


def _const_spec(shape):
    nd = len(shape)
    return pl.BlockSpec(shape, lambda *_: (0,) * nd, pipeline_mode=pl.Buffered(1))


def _params(*sem):
    return pltpu.CompilerParams(dimension_semantics=sem, vmem_limit_bytes=VMEM_LIMIT)


def _pre_body(x_ref, tab_ref, gpre_ref, gq_ref, gkv_ref, win_ref, wq_ref, wkv_ref,
              q_ref, k_ref, v_ref, ckv_ref, kr_ref, u_ref, *, n_heads, q_rank, kv_rank, d_ssm, rope_dim):
    x = x_ref[...]
    h = _rms(x, gpre_ref[...]).astype(BF16)
    z = _dot(h, win_ref[...])
    o1 = q_rank
    o2 = o1 + kv_rank
    o3 = o2 + d_ssm
    o4 = o3 + LANES
    cq, ckv, u = z[:, :o1], z[:, o1:o2], z[:, o2:o3]
    k1, k2 = z[:, o3:o4], z[:, o4:o4 + LANES]
    krs = k1 * tab_ref[0] + k2 * tab_ref[1]
    kr_ref[...] = krs[:, :rope_dim]
    ckv_n = _rms(ckv, gkv_ref[...])
    ckv_ref[...] = ckv_n
    kv2 = _dot(ckv_n.astype(BF16), wkv_ref[...])
    nk = n_heads * LANES
    k_ref[...] = (kv2[:, :nk] + _rep_lanes(krs, n_heads)).astype(k_ref.dtype)
    v_ref[...] = kv2[:, nk:].astype(v_ref.dtype)
    qn = _rms(cq, gq_ref[...]).astype(BF16)
    qq = _dot(qn, wq_ref[...])
    q = qq[:, :nk] * _rep_lanes(tab_ref[2], n_heads) + qq[:, nk:] * _rep_lanes(tab_ref[3], n_heads)
    q_ref[...] = q.astype(q_ref.dtype)
    u_ref[...] = u


def _pre_call(x2, tab, gpre, gq, gkv, win, wq, wkv, *, tm, n_heads, q_rank, kv_rank, d_ssm, rope_dim,
              q_dtype, v_width):
    t, d = x2.shape
    ntab = tab.shape[1] // tm
    nk = n_heads * LANES
    row = lambda w: pl.BlockSpec((tm, w), lambda i: (i, 0))
    body = functools.partial(_pre_body, n_heads=n_heads, q_rank=q_rank, kv_rank=kv_rank, d_ssm=d_ssm,
                             rope_dim=rope_dim)
    return pl.pallas_call(
        body,
        grid=(t // tm,),
        in_specs=[row(d),
                  pl.BlockSpec((4, tm, LANES), lambda i: (0, i % ntab, 0)),
                  _const_spec(gpre.shape), _const_spec(gq.shape), _const_spec(gkv.shape),
                  _const_spec(win.shape), _const_spec(wq.shape), _const_spec(wkv.shape)],
        out_specs=[row(nk), row(nk), row(v_width), row(kv_rank), row(rope_dim), row(d_ssm)],
        out_shape=[jax.ShapeDtypeStruct((t, nk), q_dtype),
                   jax.ShapeDtypeStruct((t, nk), BF16),
                   jax.ShapeDtypeStruct((t, v_width), BF16),
                   jax.ShapeDtypeStruct((t, kv_rank), F32),
                   jax.ShapeDtypeStruct((t, rope_dim), F32),
                   jax.ShapeDtypeStruct((t, d_ssm), F32)],
        compiler_params=_params("parallel"),
        name="pre_proj",
    )(x2, tab, gpre, gq, gkv, win, wq, wkv)


def _attn_body(q_ref, k_ref, v_ref, o_ref, m_ref, l_ref, acc_ref, *, n_heads, tq, tk, dv):
    i = pl.program_id(1)
    j = pl.program_id(2)

    @pl.when(j == 0)
    def _init():
        m_ref[...] = jnp.full(m_ref.shape, NEG_BIG, F32)
        l_ref[...] = jnp.zeros(l_ref.shape, F32)
        acc_ref[...] = jnp.zeros(acc_ref.shape, F32)

    def scores(h):
        return _dot_nt(q_ref[:, h * LANES:(h + 1) * LANES], k_ref[:, h * LANES:(h + 1) * LANES])

    def step(masked):
        if masked:
            row = lax.broadcasted_iota(jnp.int32, (tq, tk), 0)
            col = lax.broadcasted_iota(jnp.int32, (tq, tk), 1)
            bias = jnp.where(col <= row, 0.0, NEG_BIG)
        ahead = not masked
        s_next = scores(0) if ahead else None
        for h in range(n_heads):
            vh = v_ref[:, h * dv:(h + 1) * dv]
            s = s_next if ahead else scores(h)
            if ahead and h + 1 < n_heads:
                s_next = scores(h + 1)
            if masked:
                s = s + bias
            m_prev = m_ref[h]
            m_next = jnp.maximum(m_prev, jnp.max(s, axis=1, keepdims=True))
            alpha = jnp.exp2(m_prev - m_next)
            p = jnp.exp2(s - _rep_lanes(m_next, tk // LANES))
            l_ref[h] = alpha * l_ref[h] + jnp.sum(p, axis=1, keepdims=True)
            m_ref[h] = m_next
            acc_ref[h] = acc_ref[h] * alpha[:, :dv] + _dot(p.astype(BF16), vh)

    @pl.when(j < i)
    def _off_diag():
        step(False)

    @pl.when(j == i)
    def _diag():
        step(True)
        for h in range(n_heads):
            o_ref[:, h * dv:(h + 1) * dv] = acc_ref[h] / l_ref[h][:, :dv]


def _attn_call(q, k, v, *, n_heads, dv, tq):
    b, l, nk = q.shape
    tk = tq
    body = functools.partial(_attn_body, n_heads=n_heads, tq=tq, tk=tk, dv=dv)
    return pl.pallas_call(
        body,
        grid=(b, l // tq, l // tk),
        in_specs=[pl.BlockSpec((None, tq, nk), lambda bb, i, j: (bb, i, 0)),
                  pl.BlockSpec((None, tk, nk), lambda bb, i, j: (bb, jnp.minimum(i, j), 0)),
                  pl.BlockSpec((None, tk, n_heads * dv), lambda bb, i, j: (bb, jnp.minimum(i, j), 0))],
        out_specs=pl.BlockSpec((None, tq, n_heads * dv), lambda bb, i, j: (bb, i, 0)),
        out_shape=jax.ShapeDtypeStruct((b, l, n_heads * dv), F32),
        scratch_shapes=[pltpu.VMEM((n_heads, tq, LANES), F32),
                        pltpu.VMEM((n_heads, tq, LANES), F32),
                        pltpu.VMEM((n_heads, tq, dv), F32)],
        compiler_params=_params("parallel", "parallel", "arbitrary"),
        name="mla_prompt_attn",
    )(q, k, v)


def _dec_attn_body(pt_ref, q_ref, ckv_ref, kr_ref, wukt_ref, wuvp_ref, lat_hbm, pe_hbm, o_ref,
                   lat_buf, pe_buf, sems, ql_s, qp_s, m_s, l_s, acc_s,
                   *, n_heads, s_len, rope_dim, pages, page, groups, key_block):
    b = pl.program_id(0)
    nb = pl.num_programs(0)
    rows = n_heads * s_len

    def page_copies(group, slot, real):
        out = []
        for r in range(pages):
            pg = pt_ref[group * pages + r] if real else 0
            out.append(pltpu.make_async_copy(lat_hbm.at[pg], lat_buf.at[slot, pl.ds(r * page, page), :],
                                             sems.at[0, slot]))
            out.append(pltpu.make_async_copy(pe_hbm.at[pg], pe_buf.at[slot, :, pl.ds(r * page, page)],
                                             sems.at[1, slot]))
        return out

    def issue(group, slot):
        for c in page_copies(group, slot, True):
            c.start()

    def wait(slot):
        for c in page_copies(0, slot, False):
            c.wait()

    @pl.when(b == 0)
    def _prime():
        issue(0, 0)
        issue(1, 1)

    for h in range(n_heads):
        qs = q_ref[:, h * LANES:(h + 1) * LANES]
        ql_s[h * s_len:(h + 1) * s_len, :] = _dot(qs.astype(BF16), wukt_ref[h])
        qp_s[h * s_len:(h + 1) * s_len, :] = qs[:, :rope_dim]
    m_s[...] = jnp.full(m_s.shape, NEG_BIG, F32)
    l_s[...] = jnp.zeros(l_s.shape, F32)
    acc_s[...] = jnp.zeros(acc_s.shape, F32)
    ql = ql_s[...].astype(BF16)
    qp = qp_s[...].astype(BF16)

    def probs(s):
        m_b = jnp.max(s, axis=1, keepdims=True)
        p = jnp.exp2(s - m_b)
        return m_b, jnp.sum(p, axis=1, keepdims=True), p.astype(BF16)

    def merge(parts):
        m_prev = m_s[...]
        m_next = m_prev
        for m_b, _, _ in parts:
            m_next = jnp.maximum(m_next, m_b)
        w = jnp.exp2(m_prev - m_next)
        l = l_s[...] * w
        acc = acc_s[...] * w
        for m_b, l_b, o_b in parts:
            w = jnp.exp2(m_b - m_next)
            l = l + l_b * w
            acc = acc + o_b * w
        m_s[...] = m_next
        l_s[...] = l
        acc_s[...] = acc

    for j in range(groups):
        slot = j % 2
        wait(slot)
        lats, scores = [], []
        for kb in range(pages * page // key_block):
            keys = pl.ds(kb * key_block, key_block)
            lat = lat_buf[slot, keys, :].astype(BF16)
            pe_t = pe_buf[slot, :, keys].astype(BF16)
            lats.append(lat)
            scores.append(_dot_nt(ql, lat) + _dot(qp, pe_t))
        stats = [probs(s) for s in scores]
        merge([(m_b, l_b, _dot(p, lat)) for (m_b, l_b, p), lat in zip(stats, lats)])
        nxt = b * groups + j + 2
        if j + 2 < groups:
            issue(nxt, slot)
        else:
            @pl.when(b + 1 < nb)
            def _next_batch(nxt=nxt, slot=slot):
                issue(nxt, slot)

    pad = LANES - s_len
    new_lat = jnp.concatenate([ckv_ref[...], jnp.zeros((pad, ckv_ref.shape[1]), F32)], axis=0).astype(BF16)
    new_pe = jnp.concatenate([kr_ref[...], jnp.zeros((pad, rope_dim), F32)], axis=0).astype(BF16)
    s = _dot_nt(ql, new_lat) + _dot_nt(qp, new_pe)
    row = lax.broadcasted_iota(jnp.int32, (rows, LANES), 0)
    col = lax.broadcasted_iota(jnp.int32, (rows, LANES), 1)
    m_b, l_b, p = probs(jnp.where(col <= row % s_len, s, NEG_BIG))
    merge([(m_b, l_b, _dot(p, new_lat))])

    o_lat = (acc_s[...] / l_s[...]).astype(BF16)
    out = _dot(o_lat[0:s_len], wuvp_ref[0])
    for h in range(1, n_heads):
        out = out + _dot(o_lat[h * s_len:(h + 1) * s_len], wuvp_ref[h])
    o_ref[...] = out


def _dec_attn_call(page_table, q, ckv, kr, wukt, wuvp, cache_lat, cache_pe_t, *, n_heads, dv, pages):
    db, s_len, nk = q.shape
    kv_rank = ckv.shape[-1]
    rope_dim = kr.shape[-1]
    n_pages = page_table.shape[1]
    page = cache_lat.shape[1]
    groups = n_pages // pages
    key_block = min(4, pages) * page
    assert n_pages % pages == 0 and groups % 2 == 0 and (pages * page) % key_block == 0
    rows = n_heads * s_len
    body = functools.partial(_dec_attn_body, n_heads=n_heads, s_len=s_len, rope_dim=rope_dim, pages=pages,
                             page=page, groups=groups, key_block=key_block)
    grid_spec = pltpu.PrefetchScalarGridSpec(
        num_scalar_prefetch=1,
        grid=(db,),
        in_specs=[pl.BlockSpec((None, s_len, nk), lambda b, pt: (b, 0, 0)),
                  pl.BlockSpec((None, s_len, kv_rank), lambda b, pt: (b, 0, 0)),
                  pl.BlockSpec((None, s_len, rope_dim), lambda b, pt: (b, 0, 0)),
                  _const_spec(wukt.shape), _const_spec(wuvp.shape),
                  pl.BlockSpec(memory_space=pl.ANY), pl.BlockSpec(memory_space=pl.ANY)],
        out_specs=pl.BlockSpec((None, s_len, n_heads * dv), lambda b, pt: (b, 0, 0)),
        scratch_shapes=[pltpu.VMEM((2, pages * page, kv_rank), F32),
                        pltpu.VMEM((2, rope_dim, pages * page), F32),
                        pltpu.SemaphoreType.DMA((2, 2)),
                        pltpu.VMEM((rows, kv_rank), F32),
                        pltpu.VMEM((rows, rope_dim), F32),
                        pltpu.VMEM((rows, 1), F32),
                        pltpu.VMEM((rows, 1), F32),
                        pltpu.VMEM((rows, kv_rank), F32)])
    return pl.pallas_call(
        body,
        grid_spec=grid_spec,
        out_shape=jax.ShapeDtypeStruct((db, s_len, n_heads * dv), F32),
        compiler_params=_params("arbitrary"),
        name="mla_sample_attn",
    )(page_table.reshape(-1), q, ckv, kr, wukt, wuvp, cache_lat, cache_pe_t)


def _s5_disc_body(are_ref, aim_ref, ldt_ref, bre_ref, bim_ref, cre_ref, cim_ref,
                  abr_ref, abi_ref, bbr_ref, bbi_ref, ccr_ref, cci_ref):
    a_re = are_ref[...]
    a_im = aim_ref[...]
    dt = jnp.exp(ldt_ref[...])
    mag = jnp.exp(dt * a_re)
    abr = mag * jnp.cos(dt * a_im)
    abi = mag * jnp.sin(dt * a_im)
    den = a_re * a_re + a_im * a_im
    nr, ni = abr - 1.0, abi
    fr = (nr * a_re + ni * a_im) / den
    fi = (ni * a_re - nr * a_im) / den
    abr_ref[...] = abr
    abi_ref[...] = abi
    b_re = bre_ref[...]
    b_im = bim_ref[...]
    bbr = fr[:, None, :] * b_re - fi[:, None, :] * b_im
    bbi = fr[:, None, :] * b_im + fi[:, None, :] * b_re
    g, c, n = b_re.shape
    gh = g // bbr_ref.shape[0]
    for ref, blocks in ((bbr_ref, bbr), (bbi_ref, bbi), (ccr_ref, cre_ref[...]), (cci_ref, -cim_ref[...])):
        ref[...] = jnp.zeros(ref.shape, ref.dtype)
        for gi in range(g):
            hf, k = divmod(gi, gh)
            ref[hf, k * c:(k + 1) * c, k * n:(k + 1) * n] = blocks[gi].astype(ref.dtype)


def _s5_disc_call(a_re, a_im, log_dt, b_re_t, b_im_t, c_re, c_im, *, halves):
    g, n = a_re.shape
    c = b_re_t.shape[1]
    gh = g // halves
    op = jax.ShapeDtypeStruct((halves, gh * c, gh * n), BF16)
    return pl.pallas_call(
        _s5_disc_body,
        out_shape=[jax.ShapeDtypeStruct((g, n), F32), jax.ShapeDtypeStruct((g, n), F32), op, op, op, op],
        name="s5_discretise",
    )(a_re, a_im, log_dt.reshape(g, 1), b_re_t, b_im_t, c_re, c_im)


def _s5_body(u_ref, h0r_ref, h0i_ref, ar_ref, ai_ref, bbr_ref, bbi_ref, ccr_ref, cci_ref, d_ref, wglu_ref,
             y_ref, hr_out, hi_out, bur, bui, hr_s, hi_s, *, nb, tc, lane_tiles, unroll):
    c = pl.program_id(0)
    d_ssm = u_ref.shape[-1]
    n_state = hr_s.shape[-1]
    halves = bbr_ref.shape[0]
    ch_half = d_ssm // halves
    st_half = n_state // halves

    @pl.when(c == 0)
    def _init():
        hr_s[...] = h0r_ref[...]
        hi_s[...] = h0i_ref[...]

    u = jnp.swapaxes(u_ref[...], 0, 1).reshape(tc * nb, d_ssm)
    ub = u.astype(BF16)
    tiles_half = st_half // LANES
    for hf in range(halves):
        ublk = ub[:, hf * ch_half:(hf + 1) * ch_half]
        br = _dot(ublk, bbr_ref[hf])
        bi = _dot(ublk, bbi_ref[hf])
        for k in range(tiles_half):
            bur[hf * tiles_half + k] = br[:, k * LANES:(k + 1) * LANES]
            bui[hf * tiles_half + k] = bi[:, k * LANES:(k + 1) * LANES]

    for k0 in range(0, n_state // LANES, lane_tiles):
        tiles = range(k0, k0 + lane_tiles)
        ar = [jnp.broadcast_to(ar_ref[:, k * LANES:(k + 1) * LANES], (nb, LANES)) for k in tiles]
        ai = [jnp.broadcast_to(ai_ref[:, k * LANES:(k + 1) * LANES], (nb, LANES)) for k in tiles]

        def body(t, carry, tiles=tiles, ar=ar, ai=ai):
            rows = pl.ds(pl.multiple_of(t * nb, nb), nb)
            out = []
            for n, k in enumerate(tiles):
                hr, hi = carry[2 * n], carry[2 * n + 1]
                nr = ar[n] * hr - ai[n] * hi + bur[k, rows, :]
                ni = ar[n] * hi + ai[n] * hr + bui[k, rows, :]
                bur[k, rows, :] = nr
                bui[k, rows, :] = ni
                out += [nr, ni]
            return tuple(out)

        init = []
        for k in tiles:
            init += [hr_s[:, k * LANES:(k + 1) * LANES], hi_s[:, k * LANES:(k + 1) * LANES]]
        fin = lax.fori_loop(0, tc, body, tuple(init), unroll=unroll)
        for n, k in enumerate(tiles):
            hr_s[:, k * LANES:(k + 1) * LANES] = fin[2 * n]
            hi_s[:, k * LANES:(k + 1) * LANES] = fin[2 * n + 1]

    hr_out[...] = hr_s[...]
    hi_out[...] = hi_s[...]

    ys = []
    for hf in range(halves):
        hr_hist = jnp.concatenate([bur[hf * tiles_half + k] for k in range(tiles_half)], axis=1).astype(BF16)
        hi_hist = jnp.concatenate([bui[hf * tiles_half + k] for k in range(tiles_half)], axis=1).astype(BF16)
        ys.append(_dot_nt(hr_hist, ccr_ref[hf]) + _dot_nt(hi_hist, cci_ref[hf]))
    y = jnp.concatenate(ys, axis=1) + d_ref[...] * u
    g = jax.nn.gelu(y)
    out = g * jax.nn.sigmoid(_dot(g.astype(BF16), wglu_ref[...]))
    y_ref[...] = jnp.swapaxes(out.reshape(tc, nb, d_ssm), 0, 1)


def _s5_call(u3, h0r, h0i, ar, ai, bbr, bbi, ccr, cci, d, wglu, *, tc):
    nb, l, d_ssm = u3.shape
    n_state = ar.shape[-1]
    lane_tiles = math.gcd(n_state // LANES, max(1, 64 // nb))
    body = functools.partial(_s5_body, nb=nb, tc=tc, lane_tiles=lane_tiles, unroll=min(tc, 8))
    return pl.pallas_call(
        body,
        grid=(l // tc,),
        in_specs=[pl.BlockSpec((nb, tc, d_ssm), lambda c: (0, c, 0)),
                  _const_spec(h0r.shape), _const_spec(h0i.shape),
                  _const_spec(ar.shape), _const_spec(ai.shape),
                  _const_spec(bbr.shape), _const_spec(bbi.shape),
                  _const_spec(ccr.shape), _const_spec(cci.shape),
                  _const_spec(d.shape), _const_spec(wglu.shape)],
        out_specs=[pl.BlockSpec((nb, tc, d_ssm), lambda c: (0, c, 0)),
                   pl.BlockSpec((nb, n_state), lambda c: (0, 0)),
                   pl.BlockSpec((nb, n_state), lambda c: (0, 0))],
        out_shape=[jax.ShapeDtypeStruct((nb, l, d_ssm), F32),
                   jax.ShapeDtypeStruct((nb, n_state), F32),
                   jax.ShapeDtypeStruct((nb, n_state), F32)],
        scratch_shapes=[pltpu.VMEM((n_state // LANES, nb * tc, LANES), F32),
                        pltpu.VMEM((n_state // LANES, nb * tc, LANES), F32),
                        pltpu.VMEM((nb, n_state), F32), pltpu.VMEM((nb, n_state), F32)],
        compiler_params=_params("arbitrary"),
        name="s5_scan_glu",
    )(u3, h0r, h0i, ar, ai, bbr, bbi, ccr, cci, d, wglu)


def _memkv_body(m_ref, g_ref, wk_ref, wv_ref, k_ref, v_ref):
    m = _rms(m_ref[...], g_ref[...]).astype(BF16)
    k_ref[...] = _dot(m, wk_ref[...]).reshape(k_ref.shape)
    v_ref[...] = _dot(m, wv_ref[...]).reshape(v_ref.shape)


def _memkv_call(mem2, g, wk, wv, *, tm, mem_heads):
    t, d = mem2.shape
    hd = wk.shape[1] // mem_heads
    out = pl.BlockSpec((tm * mem_heads, hd), lambda i: (i, 0))
    return pl.pallas_call(
        _memkv_body,
        grid=(t // tm,),
        in_specs=[pl.BlockSpec((tm, d), lambda i: (i, 0)),
                  _const_spec(g.shape), _const_spec(wk.shape), _const_spec(wv.shape)],
        out_specs=[out, out],
        out_shape=[jax.ShapeDtypeStruct((t * mem_heads, hd), F32)] * 2,
        compiler_params=_params("parallel"),
        name="mem_kv",
    )(mem2, g, wk, wv)


def _mix_mem_body(x_ref, a_ref, s_ref, mk_ref, mv_ref, gao_ref, gso_ref, woa_ref, wos_ref, gmp_ref,
                  gmem_ref, wq_ref, wo_ref, gmo_ref, o_ref, *, mem_heads, n_mem, seqs, sub, mem_scale):
    def head_rows(ref, si, hh):
        return ref[pl.ds((si * n_mem) * mem_heads + hh, n_mem, stride=mem_heads), :].astype(BF16)

    mem = [[(head_rows(mk_ref, si, hh), head_rows(mv_ref, si, hh)) for hh in range(mem_heads)]
           for si in range(seqs)]
    tile = x_ref.shape[0] // sub
    rows = tile // seqs
    hd = wq_ref.shape[1] // mem_heads
    tiles = [pl.ds(t * tile, tile) for t in range(sub)]
    a = [_rms(a_ref[r, :], gao_ref[...]).astype(BF16) for r in tiles]
    s = [_rms(s_ref[r, :], gso_ref[...]).astype(BF16) for r in tiles]
    mix = [_dot(a[t], woa_ref[...]) + _dot(s[t], wos_ref[...]) for t in range(sub)]
    x = [x_ref[tiles[t], :] + _rms(mix[t], gmp_ref[...]) for t in range(sub)]
    h = [_rms(x[t], gmem_ref[...]).astype(BF16) for t in range(sub)]
    q = [_dot(h[t], wq_ref[...]) for t in range(sub)]
    problems = [(t, si, hh) for t in range(sub) for si in range(seqs) for hh in range(mem_heads)]
    scores = {}
    for t, si, hh in problems:
        qh = q[t][si * rows:(si + 1) * rows, hh * hd:(hh + 1) * hd].astype(BF16)
        scores[t, si, hh] = _dot_nt(qh, mem[si][hh][0]) * mem_scale
    probs = {}
    for key in problems:
        e = jnp.exp(scores[key] - jnp.max(scores[key], axis=1, keepdims=True))
        probs[key] = (e / jnp.sum(e, axis=1, keepdims=True)).astype(BF16)
    outs = {key: _dot(probs[key], mem[key[1]][key[2]][1]) for key in problems}
    for t in range(sub):
        per_seq = [jnp.concatenate([outs[t, si, hh] for hh in range(mem_heads)], axis=1) for si in range(seqs)]
        o = (jnp.concatenate(per_seq, axis=0) if seqs > 1 else per_seq[0]).astype(BF16)
        o_ref[tiles[t], :] = x[t] + _rms(_dot(o, wo_ref[...]), gmo_ref[...])


def _mix_mem_call(x2, a2, s2, mk, mv, gao, gso, woa, wos, gmp, gmem, wq, wo, gmo, *, tm, rows_per_batch,
                  mem_heads, n_mem):
    t, d = x2.shape
    da, ds = a2.shape[1], s2.shape[1]
    hd = mk.shape[1]
    seqs = max(1, tm // rows_per_batch)
    tiles_per_batch = max(1, rows_per_batch // tm)
    sub = 2 if (seqs == 1 and tm % 512 == 0) else 1
    body = functools.partial(_mix_mem_body, mem_heads=mem_heads, n_mem=n_mem, seqs=seqs, sub=sub,
                             mem_scale=hd ** -0.5)
    row = lambda w: pl.BlockSpec((tm, w), lambda i: (i, 0))
    mem = pl.BlockSpec((seqs * n_mem * mem_heads, hd), lambda i: (i // tiles_per_batch, 0))
    consts = [gao, gso, woa, wos, gmp, gmem, wq, wo, gmo]
    return pl.pallas_call(
        body,
        grid=(t // tm,),
        in_specs=[row(d), row(da), row(ds), mem, mem] + [_const_spec(c.shape) for c in consts],
        out_specs=row(d),
        out_shape=jax.ShapeDtypeStruct((t, d), F32),
        compiler_params=_params("parallel"),
        name="mix_out_mem_attn",
    )(x2, a2, s2, mk, mv, *consts)


def _ffn_body(x_ref, cprev_ref, gpre_ref, wg_ref, wu_ref, wd_ref, cw_ref, cb_ref, gpost_ref,
              o_ref, cnew_ref, halo, work, act, *, nseq, lc, fc, halo_rows):
    t = pl.program_id(1)
    d_ff = wg_ref.shape[1]
    tm = nseq * lc
    keep = cprev_ref.shape[1]
    lo = halo_rows - keep

    @pl.when(t == 0)
    def _load_state():
        halo[...] = cprev_ref[...]

    x = x_ref[...]
    h = _rms(x, gpre_ref[...]).astype(BF16)

    for c in range(d_ff // fc):
        cols = slice(c * fc, (c + 1) * fc)
        g = _dot(h, wg_ref[:, cols]).reshape(nseq, lc, fc)
        up = _dot(h, wu_ref[:, cols]).reshape(nseq, lc, fc)
        work[c, :, lo:halo_rows, :] = halo[:, :, cols]
        work[c, :, halo_rows:halo_rows + lc, :] = g
        w = cw_ref[:, cols]
        conv = w[0:1, :] * work[c, :, lo:lo + lc, :]
        for k in range(1, keep):
            conv = conv + w[k:k + 1, :] * work[c, :, lo + k:lo + k + lc, :]
        conv = conv + w[keep:keep + 1, :] * g
        gc = cb_ref[:, cols] + conv
        act[:, cols] = (jax.nn.silu(gc) * up).reshape(tm, fc).astype(BF16)
        tail = work[c, :, lc + lo:lc + halo_rows, :]
        halo[:, :, cols] = tail
        cnew_ref[:, :, cols] = tail

    o_ref[...] = x + _rms(_dot(act[...], wd_ref[...]), gpost_ref[...])


def _ffn_call(x2, cprev, gpre, wg, wu, wd, cw, cb, gpost, *, nseq, lc, fc, n_batch_blocks, tiles_per_batch):
    t, d = x2.shape
    d_ff = wg.shape[1]
    n_chunks = d_ff // fc
    keep = cprev.shape[1]
    halo_rows = 8
    tm = nseq * lc
    body = functools.partial(_ffn_body, nseq=nseq, lc=lc, fc=fc, halo_rows=halo_rows)
    state = pl.BlockSpec((nseq, keep, d_ff), lambda b, i: (b, 0, 0))
    return pl.pallas_call(
        body,
        grid=(n_batch_blocks, tiles_per_batch),
        in_specs=[pl.BlockSpec((tm, d), lambda b, i: (b * tiles_per_batch + i, 0)),
                  state,
                  _const_spec(gpre.shape), _const_spec(wg.shape), _const_spec(wu.shape),
                  _const_spec(wd.shape), _const_spec(cw.shape), _const_spec(cb.shape),
                  _const_spec(gpost.shape)],
        out_specs=[pl.BlockSpec((tm, d), lambda b, i: (b * tiles_per_batch + i, 0)), state],
        out_shape=[jax.ShapeDtypeStruct((t, d), F32),
                   jax.ShapeDtypeStruct(cprev.shape, F32)],
        scratch_shapes=[pltpu.VMEM((nseq, keep, d_ff), F32),
                        pltpu.VMEM((n_chunks, nseq, halo_rows + lc, fc), F32),
                        pltpu.VMEM((tm, d_ff), BF16)],
        compiler_params=_params("parallel", "arbitrary"),
        name="conv_ffn",
    )(x2, cprev, gpre, wg, wu, wd, cw, cb, gpost)


def _rope_tables(pos, rope_dim, nope_dim, q_scale):
    half = rope_dim // 2
    inv = ROPE_THETA ** (-jnp.arange(half, dtype=F32) * (2.0 / rope_dim))
    ang = pos.astype(F32)[:, None] * inv[None, :]
    cos, sin = jnp.cos(ang), jnp.sin(ang)
    n = pos.shape[0]
    pad = jnp.zeros((n, LANES - rope_dim), F32)
    ck = jnp.concatenate([cos, cos, pad], axis=1)
    sk = jnp.concatenate([sin, sin, pad], axis=1)
    ones = jnp.concatenate([jnp.ones((n, nope_dim), F32), jnp.zeros((n, LANES - rope_dim - nope_dim), F32)], axis=1)
    cq = jnp.concatenate([cos, cos, ones], axis=1) * q_scale
    return jnp.stack([ck, sk, cq, sk * q_scale])


def _rot_half_cols(w):
    half = w.shape[-1] // 2
    return jnp.concatenate([-w[..., half:], w[..., :half]], axis=-1)


def _pad_last(w, width):
    return jnp.pad(w, [(0, 0)] * (w.ndim - 1) + [(0, width - w.shape[-1])])


def _layer_weights(l, w_in, q_norm, kv_norm, w_uq, w_uk, w_uv, ssm_a_re, ssm_a_im, ssm_log_dt, ssm_b_re,
                   ssm_b_im, ssm_c_re, ssm_c_im, ssm_d, ssm_w_glu, w_out, w_gate, w_up, w_down, ffn_conv_w,
                   ffn_conv_b, fc):
    q_rank = q_norm.shape[-1]
    kv_rank = kv_norm.shape[-1]
    n_heads = w_uq.shape[2]
    nope = w_uk.shape[3]
    rope_dim = w_uq.shape[3] - nope
    dv = w_uv.shape[3]
    d_ssm = ssm_d.shape[-1]
    win = w_in[l]
    o1, o2, o3 = q_rank, q_rank + kv_rank, q_rank + kv_rank + rope_dim
    w_kr = win[:, o2:o3]
    p = {}
    p["win"] = jnp.concatenate([win[:, :o2], win[:, o3:], _pad_last(w_kr, LANES),
                                _pad_last(_rot_half_cols(w_kr), LANES)], axis=1).astype(BF16)
    uq = w_uq[l]
    q_nope, q_pe = uq[..., :nope], uq[..., nope:]
    wq1 = _pad_last(jnp.concatenate([q_pe, q_nope], axis=-1), LANES).reshape(q_rank, n_heads * LANES)
    wq2 = _pad_last(_rot_half_cols(q_pe), LANES).reshape(q_rank, n_heads * LANES)
    p["wq"] = jnp.concatenate([wq1, wq2], axis=1).astype(BF16)
    uk = w_uk[l]
    wuk_slots = jnp.pad(uk, ((0, 0), (0, 0), (rope_dim, LANES - rope_dim - nope)))
    p["wkv"] = jnp.concatenate([wuk_slots.reshape(kv_rank, n_heads * LANES),
                                w_uv[l].reshape(kv_rank, n_heads * dv)], axis=1).astype(BF16)
    p["wukt"] = jnp.transpose(wuk_slots, (1, 2, 0)).astype(BF16)
    uv = jnp.transpose(w_uv[l], (1, 0, 2))
    eye = jnp.eye(n_heads, dtype=F32)
    p["wuvp"] = (uv[:, :, None, :] * eye[:, None, :, None]).reshape(n_heads, kv_rank, n_heads * dv).astype(BF16)
    g, n = ssm_a_re.shape[1:]
    abr, abi, p["bbr"], p["bbi"], p["ccr"], p["cci"] = _s5_disc_call(
        ssm_a_re[l], ssm_a_im[l], ssm_log_dt[l], jnp.transpose(ssm_b_re[l], (0, 2, 1)),
        jnp.transpose(ssm_b_im[l], (0, 2, 1)), ssm_c_re[l], ssm_c_im[l], halves=2)
    p["abr"] = abr.reshape(1, g * n)
    p["abi"] = abi.reshape(1, g * n)
    p["ssm_d"] = ssm_d[l].reshape(1, d_ssm)
    p["wglu"] = ssm_w_glu[l].astype(BF16)
    d_attn = n_heads * dv
    p["woa"] = w_out[l][:d_attn].astype(BF16)
    p["wos"] = w_out[l][d_attn:].astype(BF16)
    d_model, d_ff = w_gate.shape[1:]
    nch = d_ff // fc
    p["wg"] = w_gate[l].astype(BF16)
    p["wu"] = w_up[l].astype(BF16)
    p["wd"] = w_down[l].astype(BF16)
    conv_w = ffn_conv_w.shape[1]
    p["cw"] = ffn_conv_w[l]
    p["cb"] = ffn_conv_b[l].reshape(1, d_ff)
    p["dims"] = dict(q_rank=q_rank, kv_rank=kv_rank, n_heads=n_heads, nope=nope, rope_dim=rope_dim, dv=dv,
                     d_ssm=d_ssm, g=g, n=n, d_ff=d_ff, fc=fc, nch=nch, conv_w=conv_w)
    return p


def _row(v):
    return v.reshape(1, -1)


def kernel(x_prompt, x_sample, mem_prompt, cache_kv_latent, cache_k_rope, page_table, state_ssm_re, state_ssm_im, state_ffn_conv, cache_mem_k, cache_mem_v, norm_mix_pre, w_in, q_norm, kv_norm, w_uq, w_uk, w_uv, ssm_a_re, ssm_a_im, ssm_log_dt, ssm_b_re, ssm_b_im, ssm_c_re, ssm_c_im, ssm_d, ssm_w_glu, norm_attn_out, norm_ssm_out, w_out, norm_mix_post, norm_mem_pre, mem_norm, w_q_mem, w_k_mem, w_v_mem, w_o_mem, norm_mem_post, norm_ffn_pre, w_gate, w_up, ffn_conv_w, ffn_conv_b, w_down, norm_ffn_post):
    depth = w_in.shape[0]
    b, l, d_model = x_prompt.shape
    db, ls, _ = x_sample.shape
    n_mem = mem_prompt.shape[1]
    mem_heads = cache_mem_k.shape[3]
    past_len = page_table.shape[1] * cache_kv_latent.shape[2]
    fc = 256
    tm = min(256, l)
    tm_wide = min(512, l)
    tq = min(512, l)
    tc = min(128, l)
    pages = min(32, page_table.shape[1] // 2)

    xp = x_prompt.reshape(b * l, d_model)
    xs = x_sample.reshape(db * ls, d_model)
    outs = {k: [] for k in ("p_kv", "p_kr", "p_sr", "p_si", "p_cv", "p_mk", "p_mv",
                            "s_kv", "s_kr", "s_sr", "s_si", "s_cv")}
    for li in range(depth):
        p = _layer_weights(li, w_in, q_norm, kv_norm, w_uq, w_uk, w_uv, ssm_a_re, ssm_a_im, ssm_log_dt,
                           ssm_b_re, ssm_b_im, ssm_c_re, ssm_c_im, ssm_d, ssm_w_glu, w_out, w_gate, w_up,
                           w_down, ffn_conv_w, ffn_conv_b, fc)
        dm = p["dims"]
        n_heads, dv, rope_dim, nope = dm["n_heads"], dm["dv"], dm["rope_dim"], dm["nope"]
        g, n, d_ssm = dm["g"], dm["n"], dm["d_ssm"]
        q_scale = (nope + rope_dim) ** -0.5 * LOG2E
        pre_kw = dict(n_heads=n_heads, q_rank=dm["q_rank"], kv_rank=dm["kv_rank"], d_ssm=d_ssm,
                      rope_dim=rope_dim, v_width=n_heads * dv)
        gpre, gq, gkv = _row(norm_mix_pre[li]), _row(q_norm[li]), _row(kv_norm[li])
        mix_consts = (_row(norm_attn_out[li]), _row(norm_ssm_out[li]), p["woa"], p["wos"],
                      _row(norm_mix_post[li]), _row(norm_mem_pre[li]), w_q_mem[li].astype(BF16),
                      w_o_mem[li].astype(BF16), _row(norm_mem_post[li]))
        ffn_consts = (_row(norm_ffn_pre[li]), p["wg"], p["wu"], p["wd"], p["cw"], p["cb"],
                      _row(norm_ffn_post[li]))

        mk, mv = _memkv_call(mem_prompt.reshape(b * n_mem, d_model), _row(mem_norm[li]),
                             w_k_mem[li].astype(BF16), w_v_mem[li].astype(BF16), tm=min(512, b * n_mem),
                             mem_heads=mem_heads)
        tab_p = _rope_tables(jnp.arange(l, dtype=jnp.int32), rope_dim, nope, q_scale)
        q, k, v, ckv, kr, u = _pre_call(xp, tab_p, gpre, gq, gkv, p["win"], p["wq"], p["wkv"], tm=tm_wide,
                                        q_dtype=BF16, **pre_kw)
        attn = _attn_call(q.reshape(b, l, -1), k.reshape(b, l, -1), v.reshape(b, l, -1),
                          n_heads=n_heads, dv=dv, tq=tq)
        zeros_state = jnp.zeros((b, g * n), F32)
        ssm, hr, hi = _s5_call(u.reshape(b, l, d_ssm), zeros_state, zeros_state, p["abr"], p["abi"],
                               p["bbr"], p["bbi"], p["ccr"], p["cci"], p["ssm_d"], p["wglu"], tc=tc)
        xp = _mix_mem_call(xp, attn.reshape(b * l, -1), ssm.reshape(b * l, -1),
                           mk, mv, *mix_consts, tm=tm_wide, rows_per_batch=l, mem_heads=mem_heads, n_mem=n_mem)
        conv0 = jnp.zeros((b, dm["conv_w"] - 1, dm["d_ff"]), F32)
        xp, cv = _ffn_call(xp, conv0, *ffn_consts, nseq=1, lc=tm, fc=fc, n_batch_blocks=b,
                           tiles_per_batch=l // tm)
        outs["p_kv"].append(ckv.reshape(b, l, -1))
        outs["p_kr"].append(kr.reshape(b, l, -1))
        outs["p_sr"].append(hr.reshape(b, g, n))
        outs["p_si"].append(hi.reshape(b, g, n))
        outs["p_cv"].append(cv)
        outs["p_mk"].append(mk.reshape(b, n_mem, mem_heads, -1))
        outs["p_mv"].append(mv.reshape(b, n_mem, mem_heads, -1))

        ts = db * ls
        pos_s = past_len + jnp.arange(ls, dtype=jnp.int32)
        tab_s = jnp.tile(_rope_tables(pos_s, rope_dim, nope, q_scale), (1, db, 1))
        q, _, _, ckv, kr, u = _pre_call(xs, tab_s, gpre, gq, gkv, p["win"], p["wq"], p["wkv"], tm=ts,
                                        q_dtype=F32, **pre_kw)
        attn = _dec_attn_call(page_table, q.reshape(db, ls, -1), ckv.reshape(db, ls, -1),
                              kr.reshape(db, ls, -1), p["wukt"], p["wuvp"], cache_kv_latent[li],
                              jnp.swapaxes(cache_k_rope[li], 1, 2), n_heads=n_heads, dv=dv, pages=pages)
        ssm, hr, hi = _s5_call(u.reshape(db, ls, d_ssm), state_ssm_re[li].reshape(db, g * n),
                               state_ssm_im[li].reshape(db, g * n), p["abr"], p["abi"], p["bbr"], p["bbi"],
                               p["ccr"], p["cci"], p["ssm_d"], p["wglu"], tc=ls)
        xs = _mix_mem_call(xs, attn.reshape(ts, -1), ssm.reshape(ts, -1),
                           cache_mem_k[li].reshape(db * n_mem * mem_heads, -1),
                           cache_mem_v[li].reshape(db * n_mem * mem_heads, -1),
                           *mix_consts, tm=min(4, db) * ls, rows_per_batch=ls, mem_heads=mem_heads, n_mem=n_mem)
        xs, cv = _ffn_call(xs, state_ffn_conv[li], *ffn_consts, nseq=db, lc=ls, fc=fc, n_batch_blocks=1,
                           tiles_per_batch=1)
        outs["s_kv"].append(ckv.reshape(db, ls, -1))
        outs["s_kr"].append(kr.reshape(db, ls, -1))
        outs["s_sr"].append(hr.reshape(db, g, n))
        outs["s_si"].append(hi.reshape(db, g, n))
        outs["s_cv"].append(cv)

    st = lambda key: jnp.stack(outs[key])
    return (xp.reshape(b, l, d_model), xs.reshape(db, ls, d_model),
            st("p_kv"), st("p_kr"), st("p_sr"), st("p_si"), st("p_cv"), st("p_mk"), st("p_mv"),
            st("s_kv"), st("s_kr"), st("s_sr"), st("s_si"), st("s_cv"))
```

```python
import functools
import math

import jax
import jax.numpy as jnp
from jax import lax
from jax.experimental import pallas as pl
from jax.experimental.pallas import tpu as pltpu

F32 = jnp.float32
BF16 = jnp.bfloat16

EPS = 1e-6
ROPE_THETA = 10000.0
LANES = 128
NEG_BIG = -1e30
LOG2E = 1.4426950408889634
VMEM_LIMIT = 56 * 1024 * 1024


def _rms(x, g):
    y = x * lax.rsqrt(jnp.mean(x * x, axis=-1, keepdims=True) + EPS)
    return y * g


def _dot(a, b):
    return jnp.dot(a, b, preferred_element_type=F32)


def _dot_nt(a, b):
    return lax.dot_general(a, b, (((1,), (1,)), ((), ())), preferred_element_type=F32)


def _rep_lanes(x, n):
    return jnp.concatenate([x] * n, axis=1) if n > 1 else x


def _const_spec(shape):
    nd = len(shape)
    return pl.BlockSpec(shape, lambda *_: (0,) * nd, pipeline_mode=pl.Buffered(1))


def _params(*sem):
    return pltpu.CompilerParams(dimension_semantics=sem, vmem_limit_bytes=VMEM_LIMIT)


def _pre_body(x_ref, tab_ref, gpre_ref, gq_ref, gkv_ref, win_ref, wq_ref, wkv_ref,
              q_ref, k_ref, v_ref, ckv_ref, kr_ref, u_ref, *, n_heads, q_rank, kv_rank, d_ssm, rope_dim):
    x = x_ref[...]
    h = _rms(x, gpre_ref[...]).astype(BF16)
    z = _dot(h, win_ref[...])
    o1 = q_rank
    o2 = o1 + kv_rank
    o3 = o2 + d_ssm
    o4 = o3 + LANES
    cq, ckv, u = z[:, :o1], z[:, o1:o2], z[:, o2:o3]
    k1, k2 = z[:, o3:o4], z[:, o4:o4 + LANES]
    krs = k1 * tab_ref[0] + k2 * tab_ref[1]
    kr_ref[...] = krs[:, :rope_dim]
    ckv_n = _rms(ckv, gkv_ref[...])
    ckv_ref[...] = ckv_n
    kv2 = _dot(ckv_n.astype(BF16), wkv_ref[...])
    nk = n_heads * LANES
    k_ref[...] = (kv2[:, :nk] + _rep_lanes(krs, n_heads)).astype(k_ref.dtype)
    v_ref[...] = kv2[:, nk:].astype(v_ref.dtype)
    qn = _rms(cq, gq_ref[...]).astype(BF16)
    qq = _dot(qn, wq_ref[...])
    q = qq[:, :nk] * _rep_lanes(tab_ref[2], n_heads) + qq[:, nk:] * _rep_lanes(tab_ref[3], n_heads)
    q_ref[...] = q.astype(q_ref.dtype)
    u_ref[...] = u


def _pre_call(x2, tab, gpre, gq, gkv, win, wq, wkv, *, tm, n_heads, q_rank, kv_rank, d_ssm, rope_dim,
              q_dtype, v_width):
    t, d = x2.shape
    ntab = tab.shape[1] // tm
    nk = n_heads * LANES
    row = lambda w: pl.BlockSpec((tm, w), lambda i: (i, 0))
    body = functools.partial(_pre_body, n_heads=n_heads, q_rank=q_rank, kv_rank=kv_rank, d_ssm=d_ssm,
                             rope_dim=rope_dim)
    return pl.pallas_call(
        body,
        grid=(t // tm,),
        in_specs=[row(d),
                  pl.BlockSpec((4, tm, LANES), lambda i: (0, i % ntab, 0)),
                  _const_spec(gpre.shape), _const_spec(gq.shape), _const_spec(gkv.shape),
                  _const_spec(win.shape), _const_spec(wq.shape), _const_spec(wkv.shape)],
        out_specs=[row(nk), row(nk), row(v_width), row(kv_rank), row(rope_dim), row(d_ssm)],
        out_shape=[jax.ShapeDtypeStruct((t, nk), q_dtype),
                   jax.ShapeDtypeStruct((t, nk), BF16),
                   jax.ShapeDtypeStruct((t, v_width), BF16),
                   jax.ShapeDtypeStruct((t, kv_rank), F32),
                   jax.ShapeDtypeStruct((t, rope_dim), F32),
                   jax.ShapeDtypeStruct((t, d_ssm), F32)],
        compiler_params=_params("parallel"),
        name="pre_proj",
    )(x2, tab, gpre, gq, gkv, win, wq, wkv)


def _attn_body(qi_ref, kj_ref, q_ref, k_ref, v_ref, o_ref, m_ref, l_ref, acc_ref, *, n_heads, tq, tk, dv):
    pair = pl.program_id(1)
    i = qi_ref[pair]
    j = kj_ref[pair]

    @pl.when(j == 0)
    def _init():
        m_ref[...] = jnp.full(m_ref.shape, NEG_BIG, F32)
        l_ref[...] = jnp.zeros(l_ref.shape, F32)
        acc_ref[...] = jnp.zeros(acc_ref.shape, F32)

    def step(blocks, ahead):
        items = [(h, blk) for h in range(n_heads) for blk in blocks]

        def scores(n):
            h, (r0, nr, nc, _) = items[n]
            return _dot_nt(q_ref[r0:r0 + nr, h * LANES:(h + 1) * LANES], k_ref[0:nc, h * LANES:(h + 1) * LANES])

        s_next = scores(0) if ahead else None
        for n, (h, (r0, nr, nc, bias)) in enumerate(items):
            rows = slice(r0, r0 + nr)
            s = s_next if ahead else scores(n)
            if ahead and n + 1 < len(items):
                s_next = scores(n + 1)
            if bias is not None:
                s = s + bias
            m_prev = m_ref[h, rows, :]
            m_next = jnp.maximum(m_prev, jnp.max(s, axis=1, keepdims=True))
            alpha = jnp.exp2(m_prev - m_next)
            p = jnp.exp2(s - _rep_lanes(m_next, nc // LANES))
            l_ref[h, rows, :] = alpha * l_ref[h, rows, :] + jnp.sum(p, axis=1, keepdims=True)
            m_ref[h, rows, :] = m_next
            acc_ref[h, rows, :] = (acc_ref[h, rows, :] * alpha[:, :dv]
                                   + _dot(p.astype(BF16), v_ref[0:nc, h * dv:(h + 1) * dv]))

    @pl.when(j < i)
    def _off_diag():
        step([(0, tq, tk, None)], ahead=True)

    @pl.when(j == i)
    def _diag():
        row = lax.broadcasted_iota(jnp.int32, (tq, tk), 0)
        col = lax.broadcasted_iota(jnp.int32, (tq, tk), 1)
        step([(0, tq, tk, jnp.where(col <= row, 0.0, NEG_BIG))], ahead=False)
        for h in range(n_heads):
            o_ref[:, h * dv:(h + 1) * dv] = acc_ref[h] / l_ref[h][:, :dv]


def _attn_call(q, k, v, *, n_heads, dv, tq):
    b, l, nk = q.shape
    tk = tq
    nq = l // tq
    qi = jnp.asarray([i for i in range(nq) for _ in range(i + 1)], jnp.int32)
    kj = jnp.asarray([j for i in range(nq) for j in range(i + 1)], jnp.int32)
    body = functools.partial(_attn_body, n_heads=n_heads, tq=tq, tk=tk, dv=dv)
    grid_spec = pltpu.PrefetchScalarGridSpec(
        num_scalar_prefetch=2,
        grid=(b, qi.shape[0]),
        in_specs=[pl.BlockSpec((None, tq, nk), lambda bb, p, qi, kj: (bb, qi[p], 0)),
                  pl.BlockSpec((None, tk, nk), lambda bb, p, qi, kj: (bb, kj[p], 0)),
                  pl.BlockSpec((None, tk, n_heads * dv), lambda bb, p, qi, kj: (bb, kj[p], 0))],
        out_specs=pl.BlockSpec((None, tq, n_heads * dv), lambda bb, p, qi, kj: (bb, qi[p], 0)),
        scratch_shapes=[pltpu.VMEM((n_heads, tq, LANES), F32),
                        pltpu.VMEM((n_heads, tq, LANES), F32),
                        pltpu.VMEM((n_heads, tq, dv), F32)])
    return pl.pallas_call(
        body,
        grid_spec=grid_spec,
        out_shape=jax.ShapeDtypeStruct((b, l, n_heads * dv), F32),
        compiler_params=_params("parallel", "arbitrary"),
        name="mla_prompt_attn",
    )(qi, kj, q, k, v)


def _dec_attn_body(pt_ref, q_ref, ckv_ref, kr_ref, wukt_ref, wuvp_ref, lat_hbm, pe_hbm, o_ref,
                   lat_buf, pe_buf, sems, ql_s, qp_s, m_s, l_s, acc_s,
                   *, n_heads, s_len, rope_dim, pages, page, groups, key_block):
    slots = lat_buf.shape[0]
    b = pl.program_id(0)
    nb = pl.num_programs(0)
    rows = n_heads * s_len

    def page_copies(group, slot, real):
        out = []
        for r in range(pages):
            pg = pt_ref[group * pages + r] if real else 0
            out.append(pltpu.make_async_copy(lat_hbm.at[pg], lat_buf.at[slot, pl.ds(r * page, page), :],
                                             sems.at[0, slot]))
            out.append(pltpu.make_async_copy(pe_hbm.at[pg], pe_buf.at[slot, :, pl.ds(r * page, page)],
                                             sems.at[1, slot]))
        return out

    def issue(group, slot):
        for c in page_copies(group, slot, True):
            c.start()

    def wait(slot):
        for c in page_copies(0, slot, False):
            c.wait()

    @pl.when(b == 0)
    def _prime():
        for g in range(slots):
            issue(g, g)

    for h in range(n_heads):
        qs = q_ref[:, h * LANES:(h + 1) * LANES]
        ql_s[h * s_len:(h + 1) * s_len, :] = _dot(qs.astype(BF16), wukt_ref[h])
        qp_s[h * s_len:(h + 1) * s_len, :] = qs[:, :rope_dim]
    m_s[...] = jnp.full(m_s.shape, NEG_BIG, F32)
    l_s[...] = jnp.zeros(l_s.shape, F32)
    acc_s[...] = jnp.zeros(acc_s.shape, F32)
    ql = ql_s[...].astype(BF16)
    qp = qp_s[...].astype(BF16)

    def probs(s):
        m_b = jnp.max(s, axis=1, keepdims=True)
        p = jnp.exp2(s - m_b)
        return m_b, jnp.sum(p, axis=1, keepdims=True), p.astype(BF16)

    def merge(parts):
        m_prev = m_s[...]
        m_next = m_prev
        for m_b, _, _ in parts:
            m_next = jnp.maximum(m_next, m_b)
        w = jnp.exp2(m_prev - m_next)
        l = l_s[...] * w
        acc = acc_s[...] * w
        for m_b, l_b, o_b in parts:
            w = jnp.exp2(m_b - m_next)
            l = l + l_b * w
            acc = acc + o_b * w
        m_s[...] = m_next
        l_s[...] = l
        acc_s[...] = acc

    for j in range(groups):
        slot = j % slots
        wait(slot)
        lats, scores = [], []
        for kb in range(pages * page // key_block):
            keys = pl.ds(kb * key_block, key_block)
            lat = lat_buf[slot, keys, :].astype(BF16)
            pe_t = pe_buf[slot, :, keys].astype(BF16)
            lats.append(lat)
            scores.append(_dot_nt(ql, lat) + _dot(qp, pe_t))
        stats = [probs(s) for s in scores]
        merge([(m_b, l_b, _dot(p, lat)) for (m_b, l_b, p), lat in zip(stats, lats)])
        nxt = b * groups + j + slots
        if j + slots < groups:
            issue(nxt, slot)
        else:
            @pl.when(b + 1 < nb)
            def _next_batch(nxt=nxt, slot=slot):
                issue(nxt, slot)

    pad = LANES - s_len
    new_lat = jnp.concatenate([ckv_ref[...], jnp.zeros((pad, ckv_ref.shape[1]), F32)], axis=0).astype(BF16)
    new_pe = jnp.concatenate([kr_ref[...], jnp.zeros((pad, rope_dim), F32)], axis=0).astype(BF16)
    s = _dot_nt(ql, new_lat) + _dot_nt(qp, new_pe)
    row = lax.broadcasted_iota(jnp.int32, (rows, LANES), 0)
    col = lax.broadcasted_iota(jnp.int32, (rows, LANES), 1)
    m_b, l_b, p = probs(jnp.where(col <= row % s_len, s, NEG_BIG))
    merge([(m_b, l_b, _dot(p, new_lat))])

    o_lat = (acc_s[...] / l_s[...]).astype(BF16)
    out = _dot(o_lat[0:s_len], wuvp_ref[0])
    for h in range(1, n_heads):
        out = out + _dot(o_lat[h * s_len:(h + 1) * s_len], wuvp_ref[h])
    o_ref[...] = out


def _dec_attn_call(page_table, q, ckv, kr, wukt, wuvp, cache_lat, cache_pe_t, *, n_heads, dv, pages):
    db, s_len, nk = q.shape
    kv_rank = ckv.shape[-1]
    rope_dim = kr.shape[-1]
    n_pages = page_table.shape[1]
    page = cache_lat.shape[1]
    groups = n_pages // pages
    key_block = min(4, pages) * page
    slots = math.gcd(groups, 4)
    assert n_pages % pages == 0 and slots >= 2 and (pages * page) % key_block == 0
    rows = n_heads * s_len
    body = functools.partial(_dec_attn_body, n_heads=n_heads, s_len=s_len, rope_dim=rope_dim, pages=pages,
                             page=page, groups=groups, key_block=key_block)
    grid_spec = pltpu.PrefetchScalarGridSpec(
        num_scalar_prefetch=1,
        grid=(db,),
        in_specs=[pl.BlockSpec((None, s_len, nk), lambda b, pt: (b, 0, 0)),
                  pl.BlockSpec((None, s_len, kv_rank), lambda b, pt: (b, 0, 0)),
                  pl.BlockSpec((None, s_len, rope_dim), lambda b, pt: (b, 0, 0)),
                  _const_spec(wukt.shape), _const_spec(wuvp.shape),
                  pl.BlockSpec(memory_space=pl.ANY), pl.BlockSpec(memory_space=pl.ANY)],
        out_specs=pl.BlockSpec((None, s_len, n_heads * dv), lambda b, pt: (b, 0, 0)),
        scratch_shapes=[pltpu.VMEM((slots, pages * page, kv_rank), F32),
                        pltpu.VMEM((slots, rope_dim, pages * page), F32),
                        pltpu.SemaphoreType.DMA((2, slots)),
                        pltpu.VMEM((rows, kv_rank), F32),
                        pltpu.VMEM((rows, rope_dim), F32),
                        pltpu.VMEM((rows, 1), F32),
                        pltpu.VMEM((rows, 1), F32),
                        pltpu.VMEM((rows, kv_rank), F32)])
    return pl.pallas_call(
        body,
        grid_spec=grid_spec,
        out_shape=jax.ShapeDtypeStruct((db, s_len, n_heads * dv), F32),
        compiler_params=_params("arbitrary"),
        name="mla_sample_attn",
    )(page_table.reshape(-1), q, ckv, kr, wukt, wuvp, cache_lat, cache_pe_t)


def _s5_disc_body(are_ref, aim_ref, ldt_ref, bre_ref, bim_ref, cre_ref, cim_ref,
                  abr_ref, abi_ref, bbr_ref, bbi_ref, ccr_ref, cci_ref):
    a_re = are_ref[...]
    a_im = aim_ref[...]
    dt = jnp.exp(ldt_ref[...])
    mag = jnp.exp(dt * a_re)
    abr = mag * jnp.cos(dt * a_im)
    abi = mag * jnp.sin(dt * a_im)
    den = a_re * a_re + a_im * a_im
    nr, ni = abr - 1.0, abi
    fr = (nr * a_re + ni * a_im) / den
    fi = (ni * a_re - nr * a_im) / den
    abr_ref[...] = abr
    abi_ref[...] = abi
    b_re = bre_ref[...]
    b_im = bim_ref[...]
    bbr = fr[:, None, :] * b_re - fi[:, None, :] * b_im
    bbi = fr[:, None, :] * b_im + fi[:, None, :] * b_re
    g, c, n = b_re.shape
    gh = g // bbr_ref.shape[0]
    for ref, blocks in ((bbr_ref, bbr), (bbi_ref, bbi), (ccr_ref, cre_ref[...]), (cci_ref, -cim_ref[...])):
        ref[...] = jnp.zeros(ref.shape, ref.dtype)
        for gi in range(g):
            hf, k = divmod(gi, gh)
            ref[hf, k * c:(k + 1) * c, k * n:(k + 1) * n] = blocks[gi].astype(ref.dtype)


def _s5_disc_call(a_re, a_im, log_dt, b_re_t, b_im_t, c_re, c_im, *, halves):
    g, n = a_re.shape
    c = b_re_t.shape[1]
    gh = g // halves
    op = jax.ShapeDtypeStruct((halves, gh * c, gh * n), BF16)
    return pl.pallas_call(
        _s5_disc_body,
        out_shape=[jax.ShapeDtypeStruct((g, n), F32), jax.ShapeDtypeStruct((g, n), F32), op, op, op, op],
        name="s5_discretise",
    )(a_re, a_im, log_dt.reshape(g, 1), b_re_t, b_im_t, c_re, c_im)


def _s5_body(u_ref, h0r_ref, h0i_ref, ar_ref, ai_ref, bbr_ref, bbi_ref, ccr_ref, cci_ref, d_ref, wglu_ref,
             y_ref, hr_out, hi_out, bur, bui, hr_s, hi_s, *, nb, tc, lane_tiles, unroll):
    c = pl.program_id(0)
    d_ssm = u_ref.shape[-1]
    n_state = hr_s.shape[-1]
    halves = bbr_ref.shape[0]
    ch_half = d_ssm // halves
    st_half = n_state // halves

    @pl.when(c == 0)
    def _init():
        hr_s[...] = h0r_ref[...]
        hi_s[...] = h0i_ref[...]

    u = jnp.swapaxes(u_ref[...], 0, 1).reshape(tc * nb, d_ssm)
    ub = u.astype(BF16)
    tiles_half = st_half // LANES
    for hf in range(halves):
        ublk = ub[:, hf * ch_half:(hf + 1) * ch_half]
        br = _dot(ublk, bbr_ref[hf])
        bi = _dot(ublk, bbi_ref[hf])
        for k in range(tiles_half):
            bur[hf * tiles_half + k] = br[:, k * LANES:(k + 1) * LANES]
            bui[hf * tiles_half + k] = bi[:, k * LANES:(k + 1) * LANES]

    for k0 in range(0, n_state // LANES, lane_tiles):
        tiles = range(k0, k0 + lane_tiles)
        ar = [jnp.broadcast_to(ar_ref[:, k * LANES:(k + 1) * LANES], (nb, LANES)) for k in tiles]
        ai = [jnp.broadcast_to(ai_ref[:, k * LANES:(k + 1) * LANES], (nb, LANES)) for k in tiles]

        def body(t, carry, tiles=tiles, ar=ar, ai=ai):
            rows = pl.ds(pl.multiple_of(t * nb, nb), nb)
            out = []
            for n, k in enumerate(tiles):
                hr, hi = carry[2 * n], carry[2 * n + 1]
                nr = ar[n] * hr - ai[n] * hi + bur[k, rows, :]
                ni = ar[n] * hi + ai[n] * hr + bui[k, rows, :]
                bur[k, rows, :] = nr
                bui[k, rows, :] = ni
                out += [nr, ni]
            return tuple(out)

        init = []
        for k in tiles:
            init += [hr_s[:, k * LANES:(k + 1) * LANES], hi_s[:, k * LANES:(k + 1) * LANES]]
        fin = lax.fori_loop(0, tc, body, tuple(init), unroll=unroll)
        for n, k in enumerate(tiles):
            hr_s[:, k * LANES:(k + 1) * LANES] = fin[2 * n]
            hi_s[:, k * LANES:(k + 1) * LANES] = fin[2 * n + 1]

    hr_out[...] = hr_s[...]
    hi_out[...] = hi_s[...]

    ys = []
    for hf in range(halves):
        hr_hist = jnp.concatenate([bur[hf * tiles_half + k] for k in range(tiles_half)], axis=1).astype(BF16)
        hi_hist = jnp.concatenate([bui[hf * tiles_half + k] for k in range(tiles_half)], axis=1).astype(BF16)
        ys.append(_dot_nt(hr_hist, ccr_ref[hf]) + _dot_nt(hi_hist, cci_ref[hf]))
    y = jnp.concatenate(ys, axis=1) + d_ref[...] * u
    g = jax.nn.gelu(y)
    out = g * jax.nn.sigmoid(_dot(g.astype(BF16), wglu_ref[...]))
    y_ref[...] = jnp.swapaxes(out.reshape(tc, nb, d_ssm), 0, 1)


def _s5_call(u3, h0r, h0i, ar, ai, bbr, bbi, ccr, cci, d, wglu, *, tc):
    nb, l, d_ssm = u3.shape
    n_state = ar.shape[-1]
    lane_tiles = math.gcd(n_state // LANES, max(1, 32 // nb))
    body = functools.partial(_s5_body, nb=nb, tc=tc, lane_tiles=lane_tiles, unroll=tc)
    return pl.pallas_call(
        body,
        grid=(l // tc,),
        in_specs=[pl.BlockSpec((nb, tc, d_ssm), lambda c: (0, c, 0)),
                  _const_spec(h0r.shape), _const_spec(h0i.shape),
                  _const_spec(ar.shape), _const_spec(ai.shape),
                  _const_spec(bbr.shape), _const_spec(bbi.shape),
                  _const_spec(ccr.shape), _const_spec(cci.shape),
                  _const_spec(d.shape), _const_spec(wglu.shape)],
        out_specs=[pl.BlockSpec((nb, tc, d_ssm), lambda c: (0, c, 0)),
                   pl.BlockSpec((nb, n_state), lambda c: (0, 0)),
                   pl.BlockSpec((nb, n_state), lambda c: (0, 0))],
        out_shape=[jax.ShapeDtypeStruct((nb, l, d_ssm), F32),
                   jax.ShapeDtypeStruct((nb, n_state), F32),
                   jax.ShapeDtypeStruct((nb, n_state), F32)],
        scratch_shapes=[pltpu.VMEM((n_state // LANES, nb * tc, LANES), F32),
                        pltpu.VMEM((n_state // LANES, nb * tc, LANES), F32),
                        pltpu.VMEM((nb, n_state), F32), pltpu.VMEM((nb, n_state), F32)],
        compiler_params=_params("arbitrary"),
        name="s5_scan_glu",
    )(u3, h0r, h0i, ar, ai, bbr, bbi, ccr, cci, d, wglu)


def _memkv_body(m_ref, g_ref, wk_ref, wv_ref, k_ref, v_ref):
    m = _rms(m_ref[...], g_ref[...]).astype(BF16)
    k_ref[...] = _dot(m, wk_ref[...]).reshape(k_ref.shape)
    v_ref[...] = _dot(m, wv_ref[...]).reshape(v_ref.shape)


def _memkv_call(mem2, g, wk, wv, *, tm, mem_heads):
    t, d = mem2.shape
    hd = wk.shape[1] // mem_heads
    out = pl.BlockSpec((tm * mem_heads, hd), lambda i: (i, 0))
    return pl.pallas_call(
        _memkv_body,
        grid=(t // tm,),
        in_specs=[pl.BlockSpec((tm, d), lambda i: (i, 0)),
                  _const_spec(g.shape), _const_spec(wk.shape), _const_spec(wv.shape)],
        out_specs=[out, out],
        out_shape=[jax.ShapeDtypeStruct((t * mem_heads, hd), F32)] * 2,
        compiler_params=_params("parallel"),
        name="mem_kv",
    )(mem2, g, wk, wv)


def _mix_mem_body(x_ref, a_ref, s_ref, mk_ref, mv_ref, gao_ref, gso_ref, woa_ref, wos_ref, gmp_ref,
                  gmem_ref, wq_ref, wo_ref, gmo_ref, o_ref, *, mem_heads, n_mem, seqs, sub, mem_scale):
    def head_rows(ref, si, hh):
        return ref[pl.ds((si * n_mem) * mem_heads + hh, n_mem, stride=mem_heads), :].astype(BF16)

    mem = [[(head_rows(mk_ref, si, hh), head_rows(mv_ref, si, hh)) for hh in range(mem_heads)]
           for si in range(seqs)]
    tile = x_ref.shape[0] // sub
    rows = tile // seqs
    hd = wq_ref.shape[1] // mem_heads
    tiles = [pl.ds(t * tile, tile) for t in range(sub)]
    a = [_rms(a_ref[r, :], gao_ref[...]).astype(BF16) for r in tiles]
    s = [_rms(s_ref[r, :], gso_ref[...]).astype(BF16) for r in tiles]
    mix = [_dot(a[t], woa_ref[...]) + _dot(s[t], wos_ref[...]) for t in range(sub)]
    x = [x_ref[tiles[t], :] + _rms(mix[t], gmp_ref[...]) for t in range(sub)]
    h = [_rms(x[t], gmem_ref[...]).astype(BF16) for t in range(sub)]
    q = [_dot(h[t], wq_ref[...]) for t in range(sub)]
    problems = [(t, si, hh) for t in range(sub) for si in range(seqs) for hh in range(mem_heads)]
    scores = {}
    for t, si, hh in problems:
        qh = q[t][si * rows:(si + 1) * rows, hh * hd:(hh + 1) * hd].astype(BF16)
        scores[t, si, hh] = _dot_nt(qh, mem[si][hh][0]) * mem_scale
    probs = {}
    for key in problems:
        e = jnp.exp(scores[key] - jnp.max(scores[key], axis=1, keepdims=True))
        probs[key] = (e / jnp.sum(e, axis=1, keepdims=True)).astype(BF16)
    outs = {key: _dot(probs[key], mem[key[1]][key[2]][1]) for key in problems}
    for t in range(sub):
        per_seq = [jnp.concatenate([outs[t, si, hh] for hh in range(mem_heads)], axis=1) for si in range(seqs)]
        o = (jnp.concatenate(per_seq, axis=0) if seqs > 1 else per_seq[0]).astype(BF16)
        o_ref[tiles[t], :] = x[t] + _rms(_dot(o, wo_ref[...]), gmo_ref[...])


def _mix_mem_call(x2, a2, s2, mk, mv, gao, gso, woa, wos, gmp, gmem, wq, wo, gmo, *, tm, rows_per_batch,
                  mem_heads, n_mem):
    t, d = x2.shape
    da, ds = a2.shape[1], s2.shape[1]
    hd = mk.shape[1]
    seqs = max(1, tm // rows_per_batch)
    tiles_per_batch = max(1, rows_per_batch // tm)
    sub = 2 if (seqs == 1 and tm % 512 == 0) else 1
    body = functools.partial(_mix_mem_body, mem_heads=mem_heads, n_mem=n_mem, seqs=seqs, sub=sub,
                             mem_scale=hd ** -0.5)
    row = lambda w: pl.BlockSpec((tm, w), lambda i: (i, 0))
    mem = pl.BlockSpec((seqs * n_mem * mem_heads, hd), lambda i: (i // tiles_per_batch, 0))
    consts = [gao, gso, woa, wos, gmp, gmem, wq, wo, gmo]
    return pl.pallas_call(
        body,
        grid=(t // tm,),
        in_specs=[row(d), row(da), row(ds), mem, mem] + [_const_spec(c.shape) for c in consts],
        out_specs=row(d),
        out_shape=jax.ShapeDtypeStruct((t, d), F32),
        compiler_params=_params("parallel"),
        name="mix_out_mem_attn",
    )(x2, a2, s2, mk, mv, *consts)


def _ffn_body(x_ref, cprev_ref, gpre_ref, wg_ref, wu_ref, wd_ref, cw_ref, cb_ref, gpost_ref,
              o_ref, cnew_ref, halo, work, act, *, nseq, lc, fc, halo_rows):
    t = pl.program_id(1)
    d_ff = wg_ref.shape[1]
    tm = nseq * lc
    keep = cprev_ref.shape[1]
    lo = halo_rows - keep

    @pl.when(t == 0)
    def _load_state():
        halo[...] = cprev_ref[...]

    x = x_ref[...]
    h = _rms(x, gpre_ref[...]).astype(BF16)

    for c in range(d_ff // fc):
        cols = slice(c * fc, (c + 1) * fc)
        g = _dot(h, wg_ref[:, cols]).reshape(nseq, lc, fc)
        up = _dot(h, wu_ref[:, cols]).reshape(nseq, lc, fc)
        work[c, :, lo:halo_rows, :] = halo[:, :, cols]
        work[c, :, halo_rows:halo_rows + lc, :] = g
        w = cw_ref[:, cols]
        conv = w[0:1, :] * work[c, :, lo:lo + lc, :]
        for k in range(1, keep):
            conv = conv + w[k:k + 1, :] * work[c, :, lo + k:lo + k + lc, :]
        conv = conv + w[keep:keep + 1, :] * g
        gc = cb_ref[:, cols] + conv
        act[:, cols] = (jax.nn.silu(gc) * up).reshape(tm, fc).astype(BF16)
        tail = work[c, :, lc + lo:lc + halo_rows, :]
        halo[:, :, cols] = tail
        cnew_ref[:, :, cols] = tail

    o_ref[...] = x + _rms(_dot(act[...], wd_ref[...]), gpost_ref[...])


def _ffn_call(x2, cprev, gpre, wg, wu, wd, cw, cb, gpost, *, nseq, lc, fc, n_batch_blocks, tiles_per_batch):
    t, d = x2.shape
    d_ff = wg.shape[1]
    n_chunks = d_ff // fc
    keep = cprev.shape[1]
    halo_rows = 8
    tm = nseq * lc
    body = functools.partial(_ffn_body, nseq=nseq, lc=lc, fc=fc, halo_rows=halo_rows)
    state = pl.BlockSpec((nseq, keep, d_ff), lambda b, i: (b, 0, 0))
    return pl.pallas_call(
        body,
        grid=(n_batch_blocks, tiles_per_batch),
        in_specs=[pl.BlockSpec((tm, d), lambda b, i: (b * tiles_per_batch + i, 0)),
                  state,
                  _const_spec(gpre.shape), _const_spec(wg.shape), _const_spec(wu.shape),
                  _const_spec(wd.shape), _const_spec(cw.shape), _const_spec(cb.shape),
                  _const_spec(gpost.shape)],
        out_specs=[pl.BlockSpec((tm, d), lambda b, i: (b * tiles_per_batch + i, 0)), state],
        out_shape=[jax.ShapeDtypeStruct((t, d), F32),
                   jax.ShapeDtypeStruct(cprev.shape, F32)],
        scratch_shapes=[pltpu.VMEM((nseq, keep, d_ff), F32),
                        pltpu.VMEM((n_chunks, nseq, halo_rows + lc, fc), F32),
                        pltpu.VMEM((tm, d_ff), BF16)],
        compiler_params=_params("parallel", "arbitrary"),
        name="conv_ffn",
    )(x2, cprev, gpre, wg, wu, wd, cw, cb, gpost)


def _rope_tables(pos, rope_dim, nope_dim, q_scale):
    half = rope_dim // 2
    inv = ROPE_THETA ** (-jnp.arange(half, dtype=F32) * (2.0 / rope_dim))
    ang = pos.astype(F32)[:, None] * inv[None, :]
    cos, sin = jnp.cos(ang), jnp.sin(ang)
    n = pos.shape[0]
    pad = jnp.zeros((n, LANES - rope_dim), F32)
    ck = jnp.concatenate([cos, cos, pad], axis=1)
    sk = jnp.concatenate([sin, sin, pad], axis=1)
    ones = jnp.concatenate([jnp.ones((n, nope_dim), F32), jnp.zeros((n, LANES - rope_dim - nope_dim), F32)], axis=1)
    cq = jnp.concatenate([cos, cos, ones], axis=1) * q_scale
    return jnp.stack([ck, sk, cq, sk * q_scale])


def _rot_half_cols(w):
    half = w.shape[-1] // 2
    return jnp.concatenate([-w[..., half:], w[..., :half]], axis=-1)


def _pad_last(w, width):
    return jnp.pad(w, [(0, 0)] * (w.ndim - 1) + [(0, width - w.shape[-1])])


def _layer_weights(l, w_in, q_norm, kv_norm, w_uq, w_uk, w_uv, ssm_a_re, ssm_a_im, ssm_log_dt, ssm_b_re,
                   ssm_b_im, ssm_c_re, ssm_c_im, ssm_d, ssm_w_glu, w_out, w_gate, w_up, w_down, ffn_conv_w,
                   ffn_conv_b, fc):
    q_rank = q_norm.shape[-1]
    kv_rank = kv_norm.shape[-1]
    n_heads = w_uq.shape[2]
    nope = w_uk.shape[3]
    rope_dim = w_uq.shape[3] - nope
    dv = w_uv.shape[3]
    d_ssm = ssm_d.shape[-1]
    win = w_in[l]
    o1, o2, o3 = q_rank, q_rank + kv_rank, q_rank + kv_rank + rope_dim
    w_kr = win[:, o2:o3]
    p = {}
    p["win"] = jnp.concatenate([win[:, :o2], win[:, o3:], _pad_last(w_kr, LANES),
                                _pad_last(_rot_half_cols(w_kr), LANES)], axis=1).astype(BF16)
    uq = w_uq[l]
    q_nope, q_pe = uq[..., :nope], uq[..., nope:]
    wq1 = _pad_last(jnp.concatenate([q_pe, q_nope], axis=-1), LANES).reshape(q_rank, n_heads * LANES)
    wq2 = _pad_last(_rot_half_cols(q_pe), LANES).reshape(q_rank, n_heads * LANES)
    p["wq"] = jnp.concatenate([wq1, wq2], axis=1).astype(BF16)
    uk = w_uk[l]
    wuk_slots = jnp.pad(uk, ((0, 0), (0, 0), (rope_dim, LANES - rope_dim - nope)))
    p["wkv"] = jnp.concatenate([wuk_slots.reshape(kv_rank, n_heads * LANES),
                                w_uv[l].reshape(kv_rank, n_heads * dv)], axis=1).astype(BF16)
    p["wukt"] = jnp.transpose(wuk_slots, (1, 2, 0)).astype(BF16)
    uv = jnp.transpose(w_uv[l], (1, 0, 2))
    eye = jnp.eye(n_heads, dtype=F32)
    p["wuvp"] = (uv[:, :, None, :] * eye[:, None, :, None]).reshape(n_heads, kv_rank, n_heads * dv).astype(BF16)
    g, n = ssm_a_re.shape[1:]
    abr, abi, p["bbr"], p["bbi"], p["ccr"], p["cci"] = _s5_disc_call(
        ssm_a_re[l], ssm_a_im[l], ssm_log_dt[l], jnp.transpose(ssm_b_re[l], (0, 2, 1)),
        jnp.transpose(ssm_b_im[l], (0, 2, 1)), ssm_c_re[l], ssm_c_im[l], halves=2)
    p["abr"] = abr.reshape(1, g * n)
    p["abi"] = abi.reshape(1, g * n)
    p["ssm_d"] = ssm_d[l].reshape(1, d_ssm)
    p["wglu"] = ssm_w_glu[l].astype(BF16)
    d_attn = n_heads * dv
    p["woa"] = w_out[l][:d_attn].astype(BF16)
    p["wos"] = w_out[l][d_attn:].astype(BF16)
    d_model, d_ff = w_gate.shape[1:]
    nch = d_ff // fc
    p["wg"] = w_gate[l].astype(BF16)
    p["wu"] = w_up[l].astype(BF16)
    p["wd"] = w_down[l].astype(BF16)
    conv_w = ffn_conv_w.shape[1]
    p["cw"] = ffn_conv_w[l]
    p["cb"] = ffn_conv_b[l].reshape(1, d_ff)
    p["dims"] = dict(q_rank=q_rank, kv_rank=kv_rank, n_heads=n_heads, nope=nope, rope_dim=rope_dim, dv=dv,
                     d_ssm=d_ssm, g=g, n=n, d_ff=d_ff, fc=fc, nch=nch, conv_w=conv_w)
    return p


def _row(v):
    return v.reshape(1, -1)


def kernel(x_prompt, x_sample, mem_prompt, cache_kv_latent, cache_k_rope, page_table, state_ssm_re, state_ssm_im, state_ffn_conv, cache_mem_k, cache_mem_v, norm_mix_pre, w_in, q_norm, kv_norm, w_uq, w_uk, w_uv, ssm_a_re, ssm_a_im, ssm_log_dt, ssm_b_re, ssm_b_im, ssm_c_re, ssm_c_im, ssm_d, ssm_w_glu, norm_attn_out, norm_ssm_out, w_out, norm_mix_post, norm_mem_pre, mem_norm, w_q_mem, w_k_mem, w_v_mem, w_o_mem, norm_mem_post, norm_ffn_pre, w_gate, w_up, ffn_conv_w, ffn_conv_b, w_down, norm_ffn_post):
    depth = w_in.shape[0]
    b, l, d_model = x_prompt.shape
    db, ls, _ = x_sample.shape
    n_mem = mem_prompt.shape[1]
    mem_heads = cache_mem_k.shape[3]
    past_len = page_table.shape[1] * cache_kv_latent.shape[2]
    fc = 256
    tm = min(512, l)
    tm_wide = min(512, l)
    tq = min(512, l)
    tc = min(128, l)
    pages = min(32, page_table.shape[1] // 2)

    xp = x_prompt.reshape(b * l, d_model)
    xs = x_sample.reshape(db * ls, d_model)
    outs = {k: [] for k in ("p_kv", "p_kr", "p_sr", "p_si", "p_cv", "p_mk", "p_mv",
                            "s_kv", "s_kr", "s_sr", "s_si", "s_cv")}
    for li in range(depth):
        p = _layer_weights(li, w_in, q_norm, kv_norm, w_uq, w_uk, w_uv, ssm_a_re, ssm_a_im, ssm_log_dt,
                           ssm_b_re, ssm_b_im, ssm_c_re, ssm_c_im, ssm_d, ssm_w_glu, w_out, w_gate, w_up,
                           w_down, ffn_conv_w, ffn_conv_b, fc)
        dm = p["dims"]
        n_heads, dv, rope_dim, nope = dm["n_heads"], dm["dv"], dm["rope_dim"], dm["nope"]
        g, n, d_ssm = dm["g"], dm["n"], dm["d_ssm"]
        q_scale = (nope + rope_dim) ** -0.5 * LOG2E
        pre_kw = dict(n_heads=n_heads, q_rank=dm["q_rank"], kv_rank=dm["kv_rank"], d_ssm=d_ssm,
                      rope_dim=rope_dim, v_width=n_heads * dv)
        gpre, gq, gkv = _row(norm_mix_pre[li]), _row(q_norm[li]), _row(kv_norm[li])
        mix_consts = (_row(norm_attn_out[li]), _row(norm_ssm_out[li]), p["woa"], p["wos"],
                      _row(norm_mix_post[li]), _row(norm_mem_pre[li]), w_q_mem[li].astype(BF16),
                      w_o_mem[li].astype(BF16), _row(norm_mem_post[li]))
        ffn_consts = (_row(norm_ffn_pre[li]), p["wg"], p["wu"], p["wd"], p["cw"], p["cb"],
                      _row(norm_ffn_post[li]))

        mk, mv = _memkv_call(mem_prompt.reshape(b * n_mem, d_model), _row(mem_norm[li]),
                             w_k_mem[li].astype(BF16), w_v_mem[li].astype(BF16), tm=min(512, b * n_mem),
                             mem_heads=mem_heads)
        tab_p = _rope_tables(jnp.arange(l, dtype=jnp.int32), rope_dim, nope, q_scale)
        q, k, v, ckv, kr, u = _pre_call(xp, tab_p, gpre, gq, gkv, p["win"], p["wq"], p["wkv"], tm=tm_wide,
                                        q_dtype=BF16, **pre_kw)
        attn = _attn_call(q.reshape(b, l, -1), k.reshape(b, l, -1), v.reshape(b, l, -1),
                          n_heads=n_heads, dv=dv, tq=tq)
        zeros_state = jnp.zeros((b, g * n), F32)
        ssm, hr, hi = _s5_call(u.reshape(b, l, d_ssm), zeros_state, zeros_state, p["abr"], p["abi"],
                               p["bbr"], p["bbi"], p["ccr"], p["cci"], p["ssm_d"], p["wglu"], tc=tc)
        xp = _mix_mem_call(xp, attn.reshape(b * l, -1), ssm.reshape(b * l, -1),
                           mk, mv, *mix_consts, tm=tm_wide, rows_per_batch=l, mem_heads=mem_heads, n_mem=n_mem)
        conv0 = jnp.zeros((b, dm["conv_w"] - 1, dm["d_ff"]), F32)
        xp, cv = _ffn_call(xp, conv0, *ffn_consts, nseq=1, lc=tm, fc=fc, n_batch_blocks=b,
                           tiles_per_batch=l // tm)
        outs["p_kv"].append(ckv.reshape(b, l, -1))
        outs["p_kr"].append(kr.reshape(b, l, -1))
        outs["p_sr"].append(hr.reshape(b, g, n))
        outs["p_si"].append(hi.reshape(b, g, n))
        outs["p_cv"].append(cv)
        outs["p_mk"].append(mk.reshape(b, n_mem, mem_heads, -1))
        outs["p_mv"].append(mv.reshape(b, n_mem, mem_heads, -1))

        ts = db * ls
        pos_s = past_len + jnp.arange(ls, dtype=jnp.int32)
        tab_s = jnp.tile(_rope_tables(pos_s, rope_dim, nope, q_scale), (1, db, 1))
        q, _, _, ckv, kr, u = _pre_call(xs, tab_s, gpre, gq, gkv, p["win"], p["wq"], p["wkv"], tm=ts,
                                        q_dtype=F32, **pre_kw)
        attn = _dec_attn_call(page_table, q.reshape(db, ls, -1), ckv.reshape(db, ls, -1),
                              kr.reshape(db, ls, -1), p["wukt"], p["wuvp"], cache_kv_latent[li],
                              jnp.swapaxes(cache_k_rope[li], 1, 2), n_heads=n_heads, dv=dv, pages=pages)
        ssm, hr, hi = _s5_call(u.reshape(db, ls, d_ssm), state_ssm_re[li].reshape(db, g * n),
                               state_ssm_im[li].reshape(db, g * n), p["abr"], p["abi"], p["bbr"], p["bbi"],
                               p["ccr"], p["cci"], p["ssm_d"], p["wglu"], tc=ls)
        xs = _mix_mem_call(xs, attn.reshape(ts, -1), ssm.reshape(ts, -1),
                           cache_mem_k[li].reshape(db * n_mem * mem_heads, -1),
                           cache_mem_v[li].reshape(db * n_mem * mem_heads, -1),
                           *mix_consts, tm=min(4, db) * ls, rows_per_batch=ls, mem_heads=mem_heads, n_mem=n_mem)
        xs, cv = _ffn_call(xs, state_ffn_conv[li], *ffn_consts, nseq=db, lc=ls, fc=fc, n_batch_blocks=1,
                           tiles_per_batch=1)
        outs["s_kv"].append(ckv.reshape(db, ls, -1))
        outs["s_kr"].append(kr.reshape(db, ls, -1))
        outs["s_sr"].append(hr.reshape(db, g, n))
        outs["s_si"].append(hi.reshape(db, g, n))
        outs["s_cv"].append(cv)

    st = lambda key: jnp.stack(outs[key])
    return (xp.reshape(b, l, d_model), xs.reshape(db, ls, d_model),
            st("p_kv"), st("p_kr"), st("p_sr"), st("p_si"), st("p_cv"), st("p_mk"), st("p_mv"),
            st("s_kv"), st("s_kr"), st("s_sr"), st("s_si"), st("s_cv"))
```

```python
import functools
import math

import jax
import jax.numpy as jnp
from jax import lax
from jax.experimental import pallas as pl
from jax.experimental.pallas import tpu as pltpu

F32 = jnp.float32
BF16 = jnp.bfloat16

EPS = 1e-6
ROPE_THETA = 10000.0
LANES = 128
NEG_BIG = -1e30
LOG2E = 1.4426950408889634
VMEM_LIMIT = 56 * 1024 * 1024


def _rms(x, g):
    y = x * lax.rsqrt(jnp.mean(x * x, axis=-1, keepdims=True) + EPS)
    return y * g


def _dot(a, b):
    return jnp.dot(a, b, preferred_element_type=F32)


def _dot_nt(a, b):
    return lax.dot_general(a, b, (((1,), (1,)), ((), ())), preferred_element_type=F32)


def _rep_lanes(x, n):
    return jnp.concatenate([x] * n, axis=1) if n > 1 else x


def _const_spec(shape):
    nd = len(shape)
    return pl.BlockSpec(shape, lambda *_: (0,) * nd, pipeline_mode=pl.Buffered(1))


def _params(*sem):
    return pltpu.CompilerParams(dimension_semantics=sem, vmem_limit_bytes=VMEM_LIMIT)


def _pre_body(x_ref, tab_ref, gpre_ref, gq_ref, gkv_ref, win_ref, wq_ref, wkv_ref, vone_ref,
              q_ref, k_ref, v_ref, ckv_ref, kr_ref, u_ref, *, n_heads, q_rank, kv_rank, d_ssm, rope_dim):
    x = x_ref[...]
    h = _rms(x, gpre_ref[...]).astype(BF16)
    z = _dot(h, win_ref[...])
    o1 = q_rank
    o2 = o1 + kv_rank
    o3 = o2 + d_ssm
    o4 = o3 + LANES
    cq, ckv, u = z[:, :o1], z[:, o1:o2], z[:, o2:o3]
    k1, k2 = z[:, o3:o4], z[:, o4:o4 + LANES]
    krs = k1 * tab_ref[0] + k2 * tab_ref[1]
    kr_ref[...] = krs[:, :rope_dim]
    ckv_n = _rms(ckv, gkv_ref[...])
    ckv_ref[...] = ckv_n
    kv2 = _dot(ckv_n.astype(BF16), wkv_ref[...])
    nk = n_heads * LANES
    k_ref[...] = (kv2[:, :nk] + _rep_lanes(krs, n_heads)).astype(k_ref.dtype)
    v_ref[...] = (kv2[:, nk:] + vone_ref[...]).astype(v_ref.dtype)
    qn = _rms(cq, gq_ref[...]).astype(BF16)
    qq = _dot(qn, wq_ref[...])
    q = qq[:, :nk] * _rep_lanes(tab_ref[2], n_heads) + qq[:, nk:] * _rep_lanes(tab_ref[3], n_heads)
    q_ref[...] = q.astype(q_ref.dtype)
    u_ref[...] = u


def _pre_call(x2, tab, gpre, gq, gkv, win, wq, wkv, vone, *, tm, n_heads, q_rank, kv_rank, d_ssm, rope_dim,
              q_dtype):
    t, d = x2.shape
    ntab = tab.shape[1] // tm
    nk = n_heads * LANES
    row = lambda w: pl.BlockSpec((tm, w), lambda i: (i, 0))
    body = functools.partial(_pre_body, n_heads=n_heads, q_rank=q_rank, kv_rank=kv_rank, d_ssm=d_ssm,
                             rope_dim=rope_dim)
    return pl.pallas_call(
        body,
        grid=(t // tm,),
        in_specs=[row(d),
                  pl.BlockSpec((4, tm, LANES), lambda i: (0, i % ntab, 0)),
                  _const_spec(gpre.shape), _const_spec(gq.shape), _const_spec(gkv.shape),
                  _const_spec(win.shape), _const_spec(wq.shape), _const_spec(wkv.shape),
                  _const_spec(vone.shape)],
        out_specs=[row(nk), row(nk), row(nk), row(kv_rank), row(rope_dim), row(d_ssm)],
        out_shape=[jax.ShapeDtypeStruct((t, nk), q_dtype),
                   jax.ShapeDtypeStruct((t, nk), BF16),
                   jax.ShapeDtypeStruct((t, nk), BF16),
                   jax.ShapeDtypeStruct((t, kv_rank), F32),
                   jax.ShapeDtypeStruct((t, rope_dim), F32),
                   jax.ShapeDtypeStruct((t, d_ssm), F32)],
        compiler_params=_params("parallel"),
        name="pre_proj",
    )(x2, tab, gpre, gq, gkv, win, wq, wkv, vone)


def _attn_body(qi_ref, kj_ref, q_ref, k_ref, v_ref, o_ref, m_ref, acc_ref, *, n_heads, tq, tk, dv):
    pair = pl.program_id(1)
    i = qi_ref[pair]
    j = kj_ref[pair]

    @pl.when(j == 0)
    def _init():
        m_ref[...] = jnp.full(m_ref.shape, NEG_BIG, F32)
        acc_ref[...] = jnp.zeros(acc_ref.shape, F32)

    def scores(h):
        return _dot_nt(q_ref[:, h * LANES:(h + 1) * LANES], k_ref[:, h * LANES:(h + 1) * LANES])

    def step(masked):
        ahead = not masked
        s_next = scores(0) if ahead else None
        for h in range(n_heads):
            s = s_next if ahead else scores(h)
            if ahead and h + 1 < n_heads:
                s_next = scores(h + 1)
            if masked:
                row = lax.broadcasted_iota(jnp.int32, (tq, tk), 0)
                col = lax.broadcasted_iota(jnp.int32, (tq, tk), 1)
                s = jnp.where(col <= row, s, NEG_BIG)
            m_prev = m_ref[h]
            m_next = jnp.maximum(m_prev, jnp.max(s, axis=1, keepdims=True))
            alpha = jnp.exp2(m_prev - m_next)
            p = jnp.exp2(s - _rep_lanes(m_next, tk // LANES))
            m_ref[h] = m_next
            acc_ref[h] = acc_ref[h] * alpha + _dot(p.astype(BF16), v_ref[:, h * LANES:(h + 1) * LANES])

    @pl.when(j < i)
    def _off_diag():
        step(False)

    @pl.when(j == i)
    def _diag():
        step(True)
        for h in range(n_heads):
            acc = acc_ref[h]
            o_ref[:, h * dv:(h + 1) * dv] = acc[:, :dv] / acc[:, dv:dv + 1]


def _attn_call(q, k, v, *, n_heads, dv, tq):
    b, l, nk = q.shape
    tk = tq
    nq = l // tq
    qi = jnp.asarray([i for i in range(nq) for _ in range(i + 1)], jnp.int32)
    kj = jnp.asarray([j for i in range(nq) for j in range(i + 1)], jnp.int32)
    body = functools.partial(_attn_body, n_heads=n_heads, tq=tq, tk=tk, dv=dv)
    grid_spec = pltpu.PrefetchScalarGridSpec(
        num_scalar_prefetch=2,
        grid=(b, qi.shape[0]),
        in_specs=[pl.BlockSpec((None, tq, nk), lambda bb, p, qi, kj: (bb, qi[p], 0)),
                  pl.BlockSpec((None, tk, nk), lambda bb, p, qi, kj: (bb, kj[p], 0)),
                  pl.BlockSpec((None, tk, nk), lambda bb, p, qi, kj: (bb, kj[p], 0))],
        out_specs=pl.BlockSpec((None, tq, n_heads * dv), lambda bb, p, qi, kj: (bb, qi[p], 0)),
        scratch_shapes=[pltpu.VMEM((n_heads, tq, LANES), F32),
                        pltpu.VMEM((n_heads, tq, LANES), F32)])
    return pl.pallas_call(
        body,
        grid_spec=grid_spec,
        out_shape=jax.ShapeDtypeStruct((b, l, n_heads * dv), F32),
        compiler_params=_params("parallel", "arbitrary"),
        name="mla_prompt_attn",
    )(qi, kj, q, k, v)


def _dec_attn_body(pt_ref, q_ref, ckv_ref, kr_ref, wukt_ref, wuvp_ref, lat_hbm, pe_hbm, o_ref,
                   lat_buf, pe_buf, sems, ql_s, qp_s, m_s, l_s, acc_s,
                   *, n_heads, s_len, rope_dim, pages, page, groups, key_block):
    b = pl.program_id(0)
    nb = pl.num_programs(0)
    rows = n_heads * s_len

    def page_copies(group, slot, real):
        out = []
        for r in range(pages):
            pg = pt_ref[group * pages + r] if real else 0
            out.append(pltpu.make_async_copy(lat_hbm.at[pg], lat_buf.at[slot, pl.ds(r * page, page), :],
                                             sems.at[0, slot]))
            out.append(pltpu.make_async_copy(pe_hbm.at[pg], pe_buf.at[slot, :, pl.ds(r * page, page)],
                                             sems.at[1, slot]))
        return out

    def issue(group, slot):
        for c in page_copies(group, slot, True):
            c.start()

    def wait(slot):
        for c in page_copies(0, slot, False):
            c.wait()

    @pl.when(b == 0)
    def _prime():
        for g in range(groups):
            issue(g, g)

    for h in range(n_heads):
        qs = q_ref[:, h * LANES:(h + 1) * LANES]
        ql_s[h * s_len:(h + 1) * s_len, :] = _dot(qs.astype(BF16), wukt_ref[h])
        qp_s[h * s_len:(h + 1) * s_len, :] = qs[:, :rope_dim]
    m_s[...] = jnp.full(m_s.shape, NEG_BIG, F32)
    l_s[...] = jnp.zeros(l_s.shape, F32)
    acc_s[...] = jnp.zeros(acc_s.shape, F32)
    ql = ql_s[...].astype(BF16)
    qp = qp_s[...].astype(BF16)

    def probs(s):
        m_b = jnp.max(s, axis=1, keepdims=True)
        p = jnp.exp2(s - m_b)
        return m_b, jnp.sum(p, axis=1, keepdims=True), p.astype(BF16)

    def merge(parts):
        m_prev = m_s[...]
        m_next = m_prev
        for m_b, _, _ in parts:
            m_next = jnp.maximum(m_next, m_b)
        w = jnp.exp2(m_prev - m_next)
        l = l_s[...] * w
        acc = acc_s[...] * w
        for m_b, l_b, o_b in parts:
            w = jnp.exp2(m_b - m_next)
            l = l + l_b * w
            acc = acc + o_b * w
        m_s[...] = m_next
        l_s[...] = l
        acc_s[...] = acc

    for j in range(groups):
        wait(j)
    lats, scores = [], []
    for j in range(groups):
        for kb in range(pages * page // key_block):
            keys = pl.ds(kb * key_block, key_block)
            lat = lat_buf[j, keys, :].astype(BF16)
            pe_t = pe_buf[j, :, keys].astype(BF16)
            lats.append(lat)
            scores.append(_dot_nt(ql, lat) + _dot(qp, pe_t))
    stats = [probs(s) for s in scores]
    merge([(m_b, l_b, _dot(p, lat)) for (m_b, l_b, p), lat in zip(stats, lats)])

    pad = LANES - s_len
    new_lat = jnp.concatenate([ckv_ref[...], jnp.zeros((pad, ckv_ref.shape[1]), F32)], axis=0).astype(BF16)
    new_pe = jnp.concatenate([kr_ref[...], jnp.zeros((pad, rope_dim), F32)], axis=0).astype(BF16)
    s = _dot_nt(ql, new_lat) + _dot_nt(qp, new_pe)
    row = lax.broadcasted_iota(jnp.int32, (rows, LANES), 0)
    col = lax.broadcasted_iota(jnp.int32, (rows, LANES), 1)
    m_b, l_b, p = probs(jnp.where(col <= row % s_len, s, NEG_BIG))
    merge([(m_b, l_b, _dot(p, new_lat))])

    o_lat = (acc_s[...] / l_s[...]).astype(BF16)
    out = _dot(o_lat[0:s_len], wuvp_ref[0])
    for h in range(1, n_heads):
        out = out + _dot(o_lat[h * s_len:(h + 1) * s_len], wuvp_ref[h])
    o_ref[...] = out

    @pl.when(b + 1 < nb)
    def _refill():
        for j in range(groups):
            issue((b + 1) * groups + j, j)


def _dec_attn_call(page_table, q, ckv, kr, wukt, wuvp, cache_lat, cache_pe_t, *, n_heads, dv, pages):
    db, s_len, nk = q.shape
    kv_rank = ckv.shape[-1]
    rope_dim = kr.shape[-1]
    n_pages = page_table.shape[1]
    page = cache_lat.shape[1]
    groups = n_pages // pages
    key_block = min(4, pages) * page
    slots = groups
    assert n_pages % pages == 0 and (pages * page) % key_block == 0
    rows = n_heads * s_len
    body = functools.partial(_dec_attn_body, n_heads=n_heads, s_len=s_len, rope_dim=rope_dim, pages=pages,
                             page=page, groups=groups, key_block=key_block)
    grid_spec = pltpu.PrefetchScalarGridSpec(
        num_scalar_prefetch=1,
        grid=(db,),
        in_specs=[pl.BlockSpec((None, s_len, nk), lambda b, pt: (b, 0, 0)),
                  pl.BlockSpec((None, s_len, kv_rank), lambda b, pt: (b, 0, 0)),
                  pl.BlockSpec((None, s_len, rope_dim), lambda b, pt: (b, 0, 0)),
                  _const_spec(wukt.shape), _const_spec(wuvp.shape),
                  pl.BlockSpec(memory_space=pl.ANY), pl.BlockSpec(memory_space=pl.ANY)],
        out_specs=pl.BlockSpec((None, s_len, n_heads * dv), lambda b, pt: (b, 0, 0)),
        scratch_shapes=[pltpu.VMEM((slots, pages * page, kv_rank), F32),
                        pltpu.VMEM((slots, rope_dim, pages * page), F32),
                        pltpu.SemaphoreType.DMA((2, slots)),
                        pltpu.VMEM((rows, kv_rank), F32),
                        pltpu.VMEM((rows, rope_dim), F32),
                        pltpu.VMEM((rows, 1), F32),
                        pltpu.VMEM((rows, 1), F32),
                        pltpu.VMEM((rows, kv_rank), F32)])
    return pl.pallas_call(
        body,
        grid_spec=grid_spec,
        out_shape=jax.ShapeDtypeStruct((db, s_len, n_heads * dv), F32),
        compiler_params=_params("arbitrary"),
        name="mla_sample_attn",
    )(page_table.reshape(-1), q, ckv, kr, wukt, wuvp, cache_lat, cache_pe_t)


def _s5_disc_body(are_ref, aim_ref, ldt_ref, bre_ref, bim_ref, cre_ref, cim_ref,
                  abr_ref, abi_ref, bbr_ref, bbi_ref, ccr_ref, cci_ref):
    a_re = are_ref[...]
    a_im = aim_ref[...]
    dt = jnp.exp(ldt_ref[...])
    mag = jnp.exp(dt * a_re)
    abr = mag * jnp.cos(dt * a_im)
    abi = mag * jnp.sin(dt * a_im)
    den = a_re * a_re + a_im * a_im
    nr, ni = abr - 1.0, abi
    fr = (nr * a_re + ni * a_im) / den
    fi = (ni * a_re - nr * a_im) / den
    abr_ref[...] = abr
    abi_ref[...] = abi
    b_re = bre_ref[...]
    b_im = bim_ref[...]
    bbr = fr[:, None, :] * b_re - fi[:, None, :] * b_im
    bbi = fr[:, None, :] * b_im + fi[:, None, :] * b_re
    g, c, n = b_re.shape
    gh = g // bbr_ref.shape[0]
    for ref, blocks in ((bbr_ref, bbr), (bbi_ref, bbi), (ccr_ref, cre_ref[...]), (cci_ref, -cim_ref[...])):
        ref[...] = jnp.zeros(ref.shape, ref.dtype)
        for gi in range(g):
            hf, k = divmod(gi, gh)
            ref[hf, k * c:(k + 1) * c, k * n:(k + 1) * n] = blocks[gi].astype(ref.dtype)


def _s5_disc_call(a_re, a_im, log_dt, b_re_t, b_im_t, c_re, c_im, *, halves):
    g, n = a_re.shape
    c = b_re_t.shape[1]
    gh = g // halves
    op = jax.ShapeDtypeStruct((halves, gh * c, gh * n), BF16)
    return pl.pallas_call(
        _s5_disc_body,
        out_shape=[jax.ShapeDtypeStruct((g, n), F32), jax.ShapeDtypeStruct((g, n), F32), op, op, op, op],
        name="s5_discretise",
    )(a_re, a_im, log_dt.reshape(g, 1), b_re_t, b_im_t, c_re, c_im)


def _s5_body(u_ref, h0r_ref, h0i_ref, ar_ref, ai_ref, bbr_ref, bbi_ref, ccr_ref, cci_ref, d_ref, wglu_ref,
             y_ref, hr_out, hi_out, bur, bui, hr_s, hi_s, *, nb, tc, lane_tiles, unroll):
    c = pl.program_id(0)
    d_ssm = u_ref.shape[-1]
    n_state = hr_s.shape[-1]
    halves = bbr_ref.shape[0]
    ch_half = d_ssm // halves
    st_half = n_state // halves

    @pl.when(c == 0)
    def _init():
        hr_s[...] = h0r_ref[...]
        hi_s[...] = h0i_ref[...]

    u = jnp.swapaxes(u_ref[...], 0, 1).reshape(tc * nb, d_ssm)
    ub = u.astype(BF16)
    tiles_half = st_half // LANES
    for hf in range(halves):
        ublk = ub[:, hf * ch_half:(hf + 1) * ch_half]
        br = _dot(ublk, bbr_ref[hf])
        bi = _dot(ublk, bbi_ref[hf])
        for k in range(tiles_half):
            bur[hf * tiles_half + k] = br[:, k * LANES:(k + 1) * LANES]
            bui[hf * tiles_half + k] = bi[:, k * LANES:(k + 1) * LANES]

    ys = []
    for k0 in range(0, n_state // LANES, lane_tiles):
        tiles = range(k0, k0 + lane_tiles)
        ar = [jnp.broadcast_to(ar_ref[:, k * LANES:(k + 1) * LANES], (nb, LANES)) for k in tiles]
        ai = [jnp.broadcast_to(ai_ref[:, k * LANES:(k + 1) * LANES], (nb, LANES)) for k in tiles]

        def body(t, carry, tiles=tiles, ar=ar, ai=ai):
            rows = pl.ds(pl.multiple_of(t * nb, nb), nb)
            out = []
            for n, k in enumerate(tiles):
                hr, hi = carry[2 * n], carry[2 * n + 1]
                nr = ar[n] * hr - ai[n] * hi + bur[k, rows, :]
                ni = ar[n] * hi + ai[n] * hr + bui[k, rows, :]
                bur[k, rows, :] = nr
                bui[k, rows, :] = ni
                out += [nr, ni]
            return tuple(out)

        init = []
        for k in tiles:
            init += [hr_s[:, k * LANES:(k + 1) * LANES], hi_s[:, k * LANES:(k + 1) * LANES]]
        fin = lax.fori_loop(0, tc, body, tuple(init), unroll=unroll)
        for n, k in enumerate(tiles):
            hr_s[:, k * LANES:(k + 1) * LANES] = fin[2 * n]
            hi_s[:, k * LANES:(k + 1) * LANES] = fin[2 * n + 1]
        if (k0 + lane_tiles) % tiles_half == 0:
            hf = k0 // tiles_half
            hr_hist = jnp.concatenate([bur[hf * tiles_half + k] for k in range(tiles_half)], axis=1).astype(BF16)
            hi_hist = jnp.concatenate([bui[hf * tiles_half + k] for k in range(tiles_half)], axis=1).astype(BF16)
            ys.append(_dot_nt(hr_hist, ccr_ref[hf]) + _dot_nt(hi_hist, cci_ref[hf]))

    hr_out[...] = hr_s[...]
    hi_out[...] = hi_s[...]
    y = jnp.concatenate(ys, axis=1) + d_ref[...] * u
    g = jax.nn.gelu(y)
    out = g * jax.nn.sigmoid(_dot(g.astype(BF16), wglu_ref[...]))
    y_ref[...] = jnp.swapaxes(out.reshape(tc, nb, d_ssm), 0, 1)


def _s5_call(u3, h0r, h0i, ar, ai, bbr, bbi, ccr, cci, d, wglu, *, tc):
    nb, l, d_ssm = u3.shape
    n_state = ar.shape[-1]
    lane_tiles = math.gcd(n_state // LANES // bbr.shape[0], max(1, 32 // nb))
    body = functools.partial(_s5_body, nb=nb, tc=tc, lane_tiles=lane_tiles, unroll=tc)
    return pl.pallas_call(
        body,
        grid=(l // tc,),
        in_specs=[pl.BlockSpec((nb, tc, d_ssm), lambda c: (0, c, 0)),
                  _const_spec(h0r.shape), _const_spec(h0i.shape),
                  _const_spec(ar.shape), _const_spec(ai.shape),
                  _const_spec(bbr.shape), _const_spec(bbi.shape),
                  _const_spec(ccr.shape), _const_spec(cci.shape),
                  _const_spec(d.shape), _const_spec(wglu.shape)],
        out_specs=[pl.BlockSpec((nb, tc, d_ssm), lambda c: (0, c, 0)),
                   pl.BlockSpec((nb, n_state), lambda c: (0, 0)),
                   pl.BlockSpec((nb, n_state), lambda c: (0, 0))],
        out_shape=[jax.ShapeDtypeStruct((nb, l, d_ssm), F32),
                   jax.ShapeDtypeStruct((nb, n_state), F32),
                   jax.ShapeDtypeStruct((nb, n_state), F32)],
        scratch_shapes=[pltpu.VMEM((n_state // LANES, nb * tc, LANES), F32),
                        pltpu.VMEM((n_state // LANES, nb * tc, LANES), F32),
                        pltpu.VMEM((nb, n_state), F32), pltpu.VMEM((nb, n_state), F32)],
        compiler_params=_params("arbitrary"),
        name="s5_scan_glu",
    )(u3, h0r, h0i, ar, ai, bbr, bbi, ccr, cci, d, wglu)


def _memkv_body(m_ref, g_ref, wk_ref, wv_ref, k_ref, v_ref):
    m = _rms(m_ref[...], g_ref[...]).astype(BF16)
    k_ref[...] = _dot(m, wk_ref[...]).reshape(k_ref.shape)
    v_ref[...] = _dot(m, wv_ref[...]).reshape(v_ref.shape)


def _memkv_call(mem2, g, wk, wv, *, tm, mem_heads):
    t, d = mem2.shape
    hd = wk.shape[1] // mem_heads
    out = pl.BlockSpec((tm * mem_heads, hd), lambda i: (i, 0))
    return pl.pallas_call(
        _memkv_body,
        grid=(t // tm,),
        in_specs=[pl.BlockSpec((tm, d), lambda i: (i, 0)),
                  _const_spec(g.shape), _const_spec(wk.shape), _const_spec(wv.shape)],
        out_specs=[out, out],
        out_shape=[jax.ShapeDtypeStruct((t * mem_heads, hd), F32)] * 2,
        compiler_params=_params("parallel"),
        name="mem_kv",
    )(mem2, g, wk, wv)


def _mix_mem_body(x_ref, a_ref, s_ref, mk_ref, mv_ref, gao_ref, gso_ref, woa_ref, wos_ref, gmp_ref,
                  gmem_ref, wq_ref, wo_ref, gmo_ref, o_ref, *, mem_heads, n_mem, seqs, sub, mem_scale):
    def head_rows(ref, si, hh):
        return ref[pl.ds((si * n_mem) * mem_heads + hh, n_mem, stride=mem_heads), :].astype(BF16)

    mem = [[(head_rows(mk_ref, si, hh), head_rows(mv_ref, si, hh)) for hh in range(mem_heads)]
           for si in range(seqs)]
    tile = x_ref.shape[0] // sub
    rows = tile // seqs
    hd = wq_ref.shape[1] // mem_heads
    tiles = [pl.ds(t * tile, tile) for t in range(sub)]
    a = [_rms(a_ref[r, :], gao_ref[...]).astype(BF16) for r in tiles]
    s = [_rms(s_ref[r, :], gso_ref[...]).astype(BF16) for r in tiles]
    mix = [_dot(a[t], woa_ref[...]) + _dot(s[t], wos_ref[...]) for t in range(sub)]
    x = [x_ref[tiles[t], :] + _rms(mix[t], gmp_ref[...]) for t in range(sub)]
    h = [_rms(x[t], gmem_ref[...]).astype(BF16) for t in range(sub)]
    q = [_dot(h[t], wq_ref[...]) for t in range(sub)]
    problems = [(t, si, hh) for t in range(sub) for si in range(seqs) for hh in range(mem_heads)]
    scores = {}
    for t, si, hh in problems:
        qh = q[t][si * rows:(si + 1) * rows, hh * hd:(hh + 1) * hd].astype(BF16)
        scores[t, si, hh] = _dot_nt(qh, mem[si][hh][0]) * mem_scale
    probs = {}
    for key in problems:
        e = jnp.exp(scores[key] - jnp.max(scores[key], axis=1, keepdims=True))
        probs[key] = (e / jnp.sum(e, axis=1, keepdims=True)).astype(BF16)
    outs = {key: _dot(probs[key], mem[key[1]][key[2]][1]) for key in problems}
    for t in range(sub):
        per_seq = [jnp.concatenate([outs[t, si, hh] for hh in range(mem_heads)], axis=1) for si in range(seqs)]
        o = (jnp.concatenate(per_seq, axis=0) if seqs > 1 else per_seq[0]).astype(BF16)
        o_ref[tiles[t], :] = x[t] + _rms(_dot(o, wo_ref[...]), gmo_ref[...])


def _mix_mem_call(x2, a2, s2, mk, mv, gao, gso, woa, wos, gmp, gmem, wq, wo, gmo, *, tm, rows_per_batch,
                  mem_heads, n_mem):
    t, d = x2.shape
    da, ds = a2.shape[1], s2.shape[1]
    hd = mk.shape[1]
    seqs = max(1, tm // rows_per_batch)
    tiles_per_batch = max(1, rows_per_batch // tm)
    sub = tm // 256 if (seqs == 1 and tm % 512 == 0) else 1
    body = functools.partial(_mix_mem_body, mem_heads=mem_heads, n_mem=n_mem, seqs=seqs, sub=sub,
                             mem_scale=hd ** -0.5)
    row = lambda w: pl.BlockSpec((tm, w), lambda i: (i, 0))
    mem = pl.BlockSpec((seqs * n_mem * mem_heads, hd), lambda i: (i // tiles_per_batch, 0))
    consts = [gao, gso, woa, wos, gmp, gmem, wq, wo, gmo]
    return pl.pallas_call(
        body,
        grid=(t // tm,),
        in_specs=[row(d), row(da), row(ds), mem, mem] + [_const_spec(c.shape) for c in consts],
        out_specs=row(d),
        out_shape=jax.ShapeDtypeStruct((t, d), F32),
        compiler_params=_params("parallel"),
        name="mix_out_mem_attn",
    )(x2, a2, s2, mk, mv, *consts)


def _ffn_body(x_ref, cprev_ref, gpre_ref, wg_ref, wu_ref, wd_ref, cw_ref, cb_ref, gpost_ref,
              o_ref, cnew_ref, halo, work, act, *, nseq, lc, fc, halo_rows):
    t = pl.program_id(1)
    d_ff = wg_ref.shape[1]
    tm = nseq * lc
    keep = cprev_ref.shape[1]
    lo = halo_rows - keep

    @pl.when(t == 0)
    def _load_state():
        halo[...] = cprev_ref[...]

    x = x_ref[...]
    h = _rms(x, gpre_ref[...]).astype(BF16)

    for c in range(d_ff // fc):
        cols = slice(c * fc, (c + 1) * fc)
        g = _dot(h, wg_ref[:, cols]).reshape(nseq, lc, fc)
        up = _dot(h, wu_ref[:, cols]).reshape(nseq, lc, fc)
        work[c, :, lo:halo_rows, :] = halo[:, :, cols]
        work[c, :, halo_rows:halo_rows + lc, :] = g
        w = cw_ref[:, cols]
        conv = w[0:1, :] * work[c, :, lo:lo + lc, :]
        for k in range(1, keep):
            conv = conv + w[k:k + 1, :] * work[c, :, lo + k:lo + k + lc, :]
        conv = conv + w[keep:keep + 1, :] * g
        gc = cb_ref[:, cols] + conv
        act[:, cols] = (jax.nn.silu(gc) * up).reshape(tm, fc).astype(BF16)
        tail = work[c, :, lc + lo:lc + halo_rows, :]
        halo[:, :, cols] = tail
        cnew_ref[:, :, cols] = tail

    o_ref[...] = x + _rms(_dot(act[...], wd_ref[...]), gpost_ref[...])


def _ffn_call(x2, cprev, gpre, wg, wu, wd, cw, cb, gpost, *, nseq, lc, fc, n_batch_blocks, tiles_per_batch):
    t, d = x2.shape
    d_ff = wg.shape[1]
    n_chunks = d_ff // fc
    keep = cprev.shape[1]
    halo_rows = 8
    tm = nseq * lc
    body = functools.partial(_ffn_body, nseq=nseq, lc=lc, fc=fc, halo_rows=halo_rows)
    state = pl.BlockSpec((nseq, keep, d_ff), lambda b, i: (b, 0, 0))
    return pl.pallas_call(
        body,
        grid=(n_batch_blocks, tiles_per_batch),
        in_specs=[pl.BlockSpec((tm, d), lambda b, i: (b * tiles_per_batch + i, 0)),
                  state,
                  _const_spec(gpre.shape), _const_spec(wg.shape), _const_spec(wu.shape),
                  _const_spec(wd.shape), _const_spec(cw.shape), _const_spec(cb.shape),
                  _const_spec(gpost.shape)],
        out_specs=[pl.BlockSpec((tm, d), lambda b, i: (b * tiles_per_batch + i, 0)), state],
        out_shape=[jax.ShapeDtypeStruct((t, d), F32),
                   jax.ShapeDtypeStruct(cprev.shape, F32)],
        scratch_shapes=[pltpu.VMEM((nseq, keep, d_ff), F32),
                        pltpu.VMEM((n_chunks, nseq, halo_rows + lc, fc), F32),
                        pltpu.VMEM((tm, d_ff), BF16)],
        compiler_params=_params("parallel", "arbitrary"),
        name="conv_ffn",
    )(x2, cprev, gpre, wg, wu, wd, cw, cb, gpost)


def _rope_tables(pos, rope_dim, nope_dim, q_scale):
    half = rope_dim // 2
    inv = ROPE_THETA ** (-jnp.arange(half, dtype=F32) * (2.0 / rope_dim))
    ang = pos.astype(F32)[:, None] * inv[None, :]
    cos, sin = jnp.cos(ang), jnp.sin(ang)
    n = pos.shape[0]
    pad = jnp.zeros((n, LANES - rope_dim), F32)
    ck = jnp.concatenate([cos, cos, pad], axis=1)
    sk = jnp.concatenate([sin, sin, pad], axis=1)
    ones = jnp.concatenate([jnp.ones((n, nope_dim), F32), jnp.zeros((n, LANES - rope_dim - nope_dim), F32)], axis=1)
    cq = jnp.concatenate([cos, cos, ones], axis=1) * q_scale
    return jnp.stack([ck, sk, cq, sk * q_scale])


def _rot_half_cols(w):
    half = w.shape[-1] // 2
    return jnp.concatenate([-w[..., half:], w[..., :half]], axis=-1)


def _pad_last(w, width):
    return jnp.pad(w, [(0, 0)] * (w.ndim - 1) + [(0, width - w.shape[-1])])


def _layer_weights(l, w_in, q_norm, kv_norm, w_uq, w_uk, w_uv, ssm_a_re, ssm_a_im, ssm_log_dt, ssm_b_re,
                   ssm_b_im, ssm_c_re, ssm_c_im, ssm_d, ssm_w_glu, w_out, w_gate, w_up, w_down, ffn_conv_w,
                   ffn_conv_b, fc):
    q_rank = q_norm.shape[-1]
    kv_rank = kv_norm.shape[-1]
    n_heads = w_uq.shape[2]
    nope = w_uk.shape[3]
    rope_dim = w_uq.shape[3] - nope
    dv = w_uv.shape[3]
    d_ssm = ssm_d.shape[-1]
    win = w_in[l]
    o1, o2, o3 = q_rank, q_rank + kv_rank, q_rank + kv_rank + rope_dim
    w_kr = win[:, o2:o3]
    p = {}
    p["win"] = jnp.concatenate([win[:, :o2], win[:, o3:], _pad_last(w_kr, LANES),
                                _pad_last(_rot_half_cols(w_kr), LANES)], axis=1).astype(BF16)
    uq = w_uq[l]
    q_nope, q_pe = uq[..., :nope], uq[..., nope:]
    wq1 = _pad_last(jnp.concatenate([q_pe, q_nope], axis=-1), LANES).reshape(q_rank, n_heads * LANES)
    wq2 = _pad_last(_rot_half_cols(q_pe), LANES).reshape(q_rank, n_heads * LANES)
    p["wq"] = jnp.concatenate([wq1, wq2], axis=1).astype(BF16)
    uk = w_uk[l]
    wuk_slots = jnp.pad(uk, ((0, 0), (0, 0), (rope_dim, LANES - rope_dim - nope)))
    p["wkv"] = jnp.concatenate([wuk_slots.reshape(kv_rank, n_heads * LANES),
                                _pad_last(w_uv[l], LANES).reshape(kv_rank, n_heads * LANES)], axis=1).astype(BF16)
    p["vone"] = jnp.tile((jnp.arange(LANES) == dv).astype(F32), n_heads).reshape(1, n_heads * LANES)
    p["wukt"] = jnp.transpose(wuk_slots, (1, 2, 0)).astype(BF16)
    uv = jnp.transpose(w_uv[l], (1, 0, 2))
    eye = jnp.eye(n_heads, dtype=F32)
    p["wuvp"] = (uv[:, :, None, :] * eye[:, None, :, None]).reshape(n_heads, kv_rank, n_heads * dv).astype(BF16)
    g, n = ssm_a_re.shape[1:]
    abr, abi, p["bbr"], p["bbi"], p["ccr"], p["cci"] = _s5_disc_call(
        ssm_a_re[l], ssm_a_im[l], ssm_log_dt[l], jnp.transpose(ssm_b_re[l], (0, 2, 1)),
        jnp.transpose(ssm_b_im[l], (0, 2, 1)), ssm_c_re[l], ssm_c_im[l], halves=2)
    p["abr"] = abr.reshape(1, g * n)
    p["abi"] = abi.reshape(1, g * n)
    p["ssm_d"] = ssm_d[l].reshape(1, d_ssm)
    p["wglu"] = ssm_w_glu[l].astype(BF16)
    d_attn = n_heads * dv
    p["woa"] = w_out[l][:d_attn].astype(BF16)
    p["wos"] = w_out[l][d_attn:].astype(BF16)
    d_model, d_ff = w_gate.shape[1:]
    nch = d_ff // fc
    p["wg"] = w_gate[l].astype(BF16)
    p["wu"] = w_up[l].astype(BF16)
    p["wd"] = w_down[l].astype(BF16)
    conv_w = ffn_conv_w.shape[1]
    p["cw"] = ffn_conv_w[l]
    p["cb"] = ffn_conv_b[l].reshape(1, d_ff)
    p["dims"] = dict(q_rank=q_rank, kv_rank=kv_rank, n_heads=n_heads, nope=nope, rope_dim=rope_dim, dv=dv,
                     d_ssm=d_ssm, g=g, n=n, d_ff=d_ff, fc=fc, nch=nch, conv_w=conv_w)
    return p


def _row(v):
    return v.reshape(1, -1)


def kernel(x_prompt, x_sample, mem_prompt, cache_kv_latent, cache_k_rope, page_table, state_ssm_re, state_ssm_im, state_ffn_conv, cache_mem_k, cache_mem_v, norm_mix_pre, w_in, q_norm, kv_norm, w_uq, w_uk, w_uv, ssm_a_re, ssm_a_im, ssm_log_dt, ssm_b_re, ssm_b_im, ssm_c_re, ssm_c_im, ssm_d, ssm_w_glu, norm_attn_out, norm_ssm_out, w_out, norm_mix_post, norm_mem_pre, mem_norm, w_q_mem, w_k_mem, w_v_mem, w_o_mem, norm_mem_post, norm_ffn_pre, w_gate, w_up, ffn_conv_w, ffn_conv_b, w_down, norm_ffn_post):
    depth = w_in.shape[0]
    b, l, d_model = x_prompt.shape
    db, ls, _ = x_sample.shape
    n_mem = mem_prompt.shape[1]
    mem_heads = cache_mem_k.shape[3]
    past_len = page_table.shape[1] * cache_kv_latent.shape[2]
    fc = 256
    tm = min(512, l)
    tm_wide = min(512, l)
    tq = min(512, l)
    tc = min(128, l)
    pages = min(32, page_table.shape[1] // 2)

    xp = x_prompt.reshape(b * l, d_model)
    xs = x_sample.reshape(db * ls, d_model)
    outs = {k: [] for k in ("p_kv", "p_kr", "p_sr", "p_si", "p_cv", "p_mk", "p_mv",
                            "s_kv", "s_kr", "s_sr", "s_si", "s_cv")}
    for li in range(depth):
        p = _layer_weights(li, w_in, q_norm, kv_norm, w_uq, w_uk, w_uv, ssm_a_re, ssm_a_im, ssm_log_dt,
                           ssm_b_re, ssm_b_im, ssm_c_re, ssm_c_im, ssm_d, ssm_w_glu, w_out, w_gate, w_up,
                           w_down, ffn_conv_w, ffn_conv_b, fc)
        dm = p["dims"]
        n_heads, dv, rope_dim, nope = dm["n_heads"], dm["dv"], dm["rope_dim"], dm["nope"]
        g, n, d_ssm = dm["g"], dm["n"], dm["d_ssm"]
        q_scale = (nope + rope_dim) ** -0.5 * LOG2E
        pre_kw = dict(n_heads=n_heads, q_rank=dm["q_rank"], kv_rank=dm["kv_rank"], d_ssm=d_ssm,
                      rope_dim=rope_dim)
        gpre, gq, gkv = _row(norm_mix_pre[li]), _row(q_norm[li]), _row(kv_norm[li])
        mix_consts = (_row(norm_attn_out[li]), _row(norm_ssm_out[li]), p["woa"], p["wos"],
                      _row(norm_mix_post[li]), _row(norm_mem_pre[li]), w_q_mem[li].astype(BF16),
                      w_o_mem[li].astype(BF16), _row(norm_mem_post[li]))
        ffn_consts = (_row(norm_ffn_pre[li]), p["wg"], p["wu"], p["wd"], p["cw"], p["cb"],
                      _row(norm_ffn_post[li]))

        mk, mv = _memkv_call(mem_prompt.reshape(b * n_mem, d_model), _row(mem_norm[li]),
                             w_k_mem[li].astype(BF16), w_v_mem[li].astype(BF16), tm=min(512, b * n_mem),
                             mem_heads=mem_heads)
        tab_p = _rope_tables(jnp.arange(l, dtype=jnp.int32), rope_dim, nope, q_scale)
        q, k, v, ckv, kr, u = _pre_call(xp, tab_p, gpre, gq, gkv, p["win"], p["wq"], p["wkv"], p["vone"],
                                        tm=tm_wide, q_dtype=BF16, **pre_kw)
        attn = _attn_call(q.reshape(b, l, -1), k.reshape(b, l, -1), v.reshape(b, l, -1),
                          n_heads=n_heads, dv=dv, tq=tq)
        zeros_state = jnp.zeros((b, g * n), F32)
        ssm, hr, hi = _s5_call(u.reshape(b, l, d_ssm), zeros_state, zeros_state, p["abr"], p["abi"],
                               p["bbr"], p["bbi"], p["ccr"], p["cci"], p["ssm_d"], p["wglu"], tc=tc)
        xp = _mix_mem_call(xp, attn.reshape(b * l, -1), ssm.reshape(b * l, -1),
                           mk, mv, *mix_consts, tm=min(1024, l), rows_per_batch=l, mem_heads=mem_heads,
                           n_mem=n_mem)
        conv0 = jnp.zeros((b, dm["conv_w"] - 1, dm["d_ff"]), F32)
        xp, cv = _ffn_call(xp, conv0, *ffn_consts, nseq=1, lc=tm, fc=fc, n_batch_blocks=b,
                           tiles_per_batch=l // tm)
        outs["p_kv"].append(ckv.reshape(b, l, -1))
        outs["p_kr"].append(kr.reshape(b, l, -1))
        outs["p_sr"].append(hr.reshape(b, g, n))
        outs["p_si"].append(hi.reshape(b, g, n))
        outs["p_cv"].append(cv)
        outs["p_mk"].append(mk.reshape(b, n_mem, mem_heads, -1))
        outs["p_mv"].append(mv.reshape(b, n_mem, mem_heads, -1))

        ts = db * ls
        pos_s = past_len + jnp.arange(ls, dtype=jnp.int32)
        tab_s = jnp.tile(_rope_tables(pos_s, rope_dim, nope, q_scale), (1, db, 1))
        q, _, _, ckv, kr, u = _pre_call(xs, tab_s, gpre, gq, gkv, p["win"], p["wq"], p["wkv"], p["vone"],
                                        tm=ts, q_dtype=F32, **pre_kw)
        attn = _dec_attn_call(page_table, q.reshape(db, ls, -1), ckv.reshape(db, ls, -1),
                              kr.reshape(db, ls, -1), p["wukt"], p["wuvp"], cache_kv_latent[li],
                              jnp.swapaxes(cache_k_rope[li], 1, 2), n_heads=n_heads, dv=dv, pages=pages)
        ssm, hr, hi = _s5_call(u.reshape(db, ls, d_ssm), state_ssm_re[li].reshape(db, g * n),
                               state_ssm_im[li].reshape(db, g * n), p["abr"], p["abi"], p["bbr"], p["bbi"],
                               p["ccr"], p["cci"], p["ssm_d"], p["wglu"], tc=ls)
        xs = _mix_mem_call(xs, attn.reshape(ts, -1), ssm.reshape(ts, -1),
                           cache_mem_k[li].reshape(db * n_mem * mem_heads, -1),
                           cache_mem_v[li].reshape(db * n_mem * mem_heads, -1),
                           *mix_consts, tm=min(4, db) * ls, rows_per_batch=ls, mem_heads=mem_heads, n_mem=n_mem)
        xs, cv = _ffn_call(xs, state_ffn_conv[li], *ffn_consts, nseq=db, lc=ls, fc=fc, n_batch_blocks=1,
                           tiles_per_batch=1)
        outs["s_kv"].append(ckv.reshape(db, ls, -1))
        outs["s_kr"].append(kr.reshape(db, ls, -1))
        outs["s_sr"].append(hr.reshape(db, g, n))
        outs["s_si"].append(hi.reshape(db, g, n))
        outs["s_cv"].append(cv)

    st = lambda key: jnp.stack(outs[key])
    return (xp.reshape(b, l, d_model), xs.reshape(db, ls, d_model),
            st("p_kv"), st("p_kr"), st("p_sr"), st("p_si"), st("p_cv"), st("p_mk"), st("p_mv"),
            st("s_kv"), st("s_kr"), st("s_sr"), st("s_si"), st("s_cv"))
```

```python
import functools
import math

import jax
import jax.numpy as jnp
from jax import lax
from jax.experimental import pallas as pl
from jax.experimental.pallas import tpu as pltpu

F32 = jnp.float32
BF16 = jnp.bfloat16

EPS = 1e-6
ROPE_THETA = 10000.0
LANES = 128
NEG_BIG = -1e30
LOG2E = 1.4426950408889634
VMEM_LIMIT = 56 * 1024 * 1024


def _rms(x, g):
    y = x * lax.rsqrt(jnp.mean(x * x, axis=-1, keepdims=True) + EPS)
    return y * g


def _dot(a, b):
    return jnp.dot(a, b, preferred_element_type=F32)


def _dot_nt(a, b):
    return lax.dot_general(a, b, (((1,), (1,)), ((), ())), preferred_element_type=F32)


def _rep_lanes(x, n):
    return jnp.concatenate([x] * n, axis=1) if n > 1 else x


def _const_spec(shape):
    nd = len(shape)
    return pl.BlockSpec(shape, lambda *_: (0,) * nd, pipeline_mode=pl.Buffered(1))


def _params(*sem):
    return pltpu.CompilerParams(dimension_semantics=sem, vmem_limit_bytes=VMEM_LIMIT)


def _pre_body(x_ref, tab_ref, gpre_ref, gq_ref, gkv_ref, win_ref, wq_ref, wkv_ref, vone_ref,
              q_ref, k_ref, v_ref, ckv_ref, kr_ref, u_ref, *, n_heads, q_rank, kv_rank, d_ssm, rope_dim):
    x = x_ref[...]
    h = _rms(x, gpre_ref[...]).astype(BF16)
    z = _dot(h, win_ref[...])
    o1 = q_rank
    o2 = o1 + kv_rank
    o3 = o2 + d_ssm
    o4 = o3 + LANES
    cq, ckv, u = z[:, :o1], z[:, o1:o2], z[:, o2:o3]
    k1, k2 = z[:, o3:o4], z[:, o4:o4 + LANES]
    krs = k1 * tab_ref[0] + k2 * tab_ref[1]
    kr_ref[...] = krs[:, :rope_dim]
    ckv_n = _rms(ckv, gkv_ref[...])
    ckv_ref[...] = ckv_n
    kv2 = _dot(ckv_n.astype(BF16), wkv_ref[...])
    nk = n_heads * LANES
    k_ref[...] = (kv2[:, :nk] + _rep_lanes(krs, n_heads)).astype(k_ref.dtype)
    v_ref[...] = (kv2[:, nk:] + vone_ref[...]).astype(v_ref.dtype)
    qn = _rms(cq, gq_ref[...]).astype(BF16)
    qq = _dot(qn, wq_ref[...])
    q = qq[:, :nk] * _rep_lanes(tab_ref[2], n_heads) + qq[:, nk:] * _rep_lanes(tab_ref[3], n_heads)
    q_ref[...] = q.astype(q_ref.dtype)
    u_ref[...] = u


def _pre_call(x2, tab, gpre, gq, gkv, win, wq, wkv, vone, *, tm, n_heads, q_rank, kv_rank, d_ssm, rope_dim,
              q_dtype):
    t, d = x2.shape
    ntab = tab.shape[1] // tm
    nk = n_heads * LANES
    row = lambda w: pl.BlockSpec((tm, w), lambda i: (i, 0))
    body = functools.partial(_pre_body, n_heads=n_heads, q_rank=q_rank, kv_rank=kv_rank, d_ssm=d_ssm,
                             rope_dim=rope_dim)
    return pl.pallas_call(
        body,
        grid=(t // tm,),
        in_specs=[row(d),
                  pl.BlockSpec((4, tm, LANES), lambda i: (0, i % ntab, 0)),
                  _const_spec(gpre.shape), _const_spec(gq.shape), _const_spec(gkv.shape),
                  _const_spec(win.shape), _const_spec(wq.shape), _const_spec(wkv.shape),
                  _const_spec(vone.shape)],
        out_specs=[row(nk), row(nk), row(nk), row(kv_rank), row(rope_dim), row(d_ssm)],
        out_shape=[jax.ShapeDtypeStruct((t, nk), q_dtype),
                   jax.ShapeDtypeStruct((t, nk), BF16),
                   jax.ShapeDtypeStruct((t, nk), BF16),
                   jax.ShapeDtypeStruct((t, kv_rank), F32),
                   jax.ShapeDtypeStruct((t, rope_dim), F32),
                   jax.ShapeDtypeStruct((t, d_ssm), F32)],
        compiler_params=_params("parallel"),
        name="pre_proj",
    )(x2, tab, gpre, gq, gkv, win, wq, wkv, vone)


def _attn_body(qi_ref, kj_ref, q_ref, k_ref, v_ref, o_ref, m_ref, acc_ref, *, n_heads, tq, tk, dv):
    pair = pl.program_id(1)
    i = qi_ref[pair]
    j = kj_ref[pair]

    @pl.when(j == 0)
    def _init():
        m_ref[...] = jnp.full(m_ref.shape, NEG_BIG, F32)
        acc_ref[...] = jnp.zeros(acc_ref.shape, F32)

    def scores(h):
        return _dot_nt(q_ref[:, h * LANES:(h + 1) * LANES], k_ref[:, h * LANES:(h + 1) * LANES])

    def step(masked):
        ahead = not masked
        s_next = scores(0) if ahead else None
        for h in range(n_heads):
            s = s_next if ahead else scores(h)
            if ahead and h + 1 < n_heads:
                s_next = scores(h + 1)
            if masked:
                row = lax.broadcasted_iota(jnp.int32, (tq, tk), 0)
                col = lax.broadcasted_iota(jnp.int32, (tq, tk), 1)
                s = jnp.where(col <= row, s, NEG_BIG)
            m_prev = m_ref[h]
            m_next = jnp.maximum(m_prev, jnp.max(s, axis=1, keepdims=True))
            alpha = jnp.exp2(m_prev - m_next)
            p = jnp.exp2(s - _rep_lanes(m_next, tk // LANES))
            m_ref[h] = m_next
            acc_ref[h] = acc_ref[h] * alpha + _dot(p.astype(BF16), v_ref[:, h * LANES:(h + 1) * LANES])

    @pl.when(j < i)
    def _off_diag():
        step(False)

    @pl.when(j == i)
    def _diag():
        step(True)
        for h in range(n_heads):
            acc = acc_ref[h]
            o_ref[:, h * dv:(h + 1) * dv] = acc[:, :dv] / acc[:, dv:dv + 1]


def _attn_call(q, k, v, *, n_heads, dv, tq):
    b, l, nk = q.shape
    tk = tq
    nq = l // tq
    qi = jnp.asarray([i for i in range(nq) for _ in range(i + 1)], jnp.int32)
    kj = jnp.asarray([j for i in range(nq) for j in range(i + 1)], jnp.int32)
    body = functools.partial(_attn_body, n_heads=n_heads, tq=tq, tk=tk, dv=dv)
    grid_spec = pltpu.PrefetchScalarGridSpec(
        num_scalar_prefetch=2,
        grid=(b, qi.shape[0]),
        in_specs=[pl.BlockSpec((None, tq, nk), lambda bb, p, qi, kj: (bb, qi[p], 0)),
                  pl.BlockSpec((None, tk, nk), lambda bb, p, qi, kj: (bb, kj[p], 0)),
                  pl.BlockSpec((None, tk, nk), lambda bb, p, qi, kj: (bb, kj[p], 0))],
        out_specs=pl.BlockSpec((None, tq, n_heads * dv), lambda bb, p, qi, kj: (bb, qi[p], 0)),
        scratch_shapes=[pltpu.VMEM((n_heads, tq, LANES), F32),
                        pltpu.VMEM((n_heads, tq, LANES), F32)])
    return pl.pallas_call(
        body,
        grid_spec=grid_spec,
        out_shape=jax.ShapeDtypeStruct((b, l, n_heads * dv), F32),
        compiler_params=_params("parallel", "arbitrary"),
        name="mla_prompt_attn",
    )(qi, kj, q, k, v)


def _dec_attn_body(pt_ref, q_ref, ckv_ref, kr_ref, wukt_ref, wuvp_ref, lat_hbm, pe_hbm, o_ref,
                   lat_buf, pe_buf, sems, ql_s, qp_s, m_s, l_s, acc_s,
                   *, n_heads, s_len, rope_dim, pages, page, groups, key_block):
    b = pl.program_id(0)
    nb = pl.num_programs(0)
    rows = n_heads * s_len
    mine = (b % 2) * groups
    other = groups - mine

    def page_copies(group, slot, real):
        out = []
        for r in range(pages):
            pg = pt_ref[group * pages + r] if real else 0
            out.append(pltpu.make_async_copy(lat_hbm.at[pg], lat_buf.at[slot, pl.ds(r * page, page), :],
                                             sems.at[0, slot]))
            out.append(pltpu.make_async_copy(pe_hbm.at[pg], pe_buf.at[slot, :, pl.ds(r * page, page)],
                                             sems.at[1, slot]))
        return out

    def issue(group, slot):
        for c in page_copies(group, slot, True):
            c.start()

    def wait(slot):
        for c in page_copies(0, slot, False):
            c.wait()

    @pl.when(b == 0)
    def _prime():
        for j in range(groups):
            issue(j, j)

    @pl.when(b + 1 < nb)
    def _prefetch_next():
        for j in range(groups):
            issue((b + 1) * groups + j, other + j)

    for h in range(n_heads):
        qs = q_ref[:, h * LANES:(h + 1) * LANES]
        ql_s[h * s_len:(h + 1) * s_len, :] = _dot(qs.astype(BF16), wukt_ref[h])
        qp_s[h * s_len:(h + 1) * s_len, :] = qs[:, :rope_dim]
    m_s[...] = jnp.full(m_s.shape, NEG_BIG, F32)
    l_s[...] = jnp.zeros(l_s.shape, F32)
    acc_s[...] = jnp.zeros(acc_s.shape, F32)
    ql = ql_s[...].astype(BF16)
    qp = qp_s[...].astype(BF16)

    def probs(s):
        m_b = jnp.max(s, axis=1, keepdims=True)
        p = jnp.exp2(s - m_b)
        return m_b, jnp.sum(p, axis=1, keepdims=True), p.astype(BF16)

    def merge(parts):
        m_prev = m_s[...]
        m_next = m_prev
        for m_b, _, _ in parts:
            m_next = jnp.maximum(m_next, m_b)
        w = jnp.exp2(m_prev - m_next)
        l = l_s[...] * w
        acc = acc_s[...] * w
        for m_b, l_b, o_b in parts:
            w = jnp.exp2(m_b - m_next)
            l = l + l_b * w
            acc = acc + o_b * w
        m_s[...] = m_next
        l_s[...] = l
        acc_s[...] = acc

    for j in range(groups):
        wait(mine + j)
    for j in range(groups):
        lats, scores = [], []
        for kb in range(pages * page // key_block):
            keys = pl.ds(kb * key_block, key_block)
            lat = lat_buf[mine + j, keys, :].astype(BF16)
            pe_t = pe_buf[mine + j, :, keys].astype(BF16)
            lats.append(lat)
            scores.append(_dot_nt(ql, lat) + _dot(qp, pe_t))
        stats = [probs(s) for s in scores]
        merge([(m_b, l_b, _dot(p, lat)) for (m_b, l_b, p), lat in zip(stats, lats)])

    pad = LANES - s_len
    new_lat = jnp.concatenate([ckv_ref[...], jnp.zeros((pad, ckv_ref.shape[1]), F32)], axis=0).astype(BF16)
    new_pe = jnp.concatenate([kr_ref[...], jnp.zeros((pad, rope_dim), F32)], axis=0).astype(BF16)
    s = _dot_nt(ql, new_lat) + _dot_nt(qp, new_pe)
    row = lax.broadcasted_iota(jnp.int32, (rows, LANES), 0)
    col = lax.broadcasted_iota(jnp.int32, (rows, LANES), 1)
    m_b, l_b, p = probs(jnp.where(col <= row % s_len, s, NEG_BIG))
    merge([(m_b, l_b, _dot(p, new_lat))])

    o_lat = (acc_s[...] / l_s[...]).astype(BF16)
    out = _dot(o_lat[0:s_len], wuvp_ref[0])
    for h in range(1, n_heads):
        out = out + _dot(o_lat[h * s_len:(h + 1) * s_len], wuvp_ref[h])
    o_ref[...] = out


def _dec_attn_call(page_table, q, ckv, kr, wukt, wuvp, cache_lat, cache_pe_t, *, n_heads, dv, pages):
    db, s_len, nk = q.shape
    kv_rank = ckv.shape[-1]
    rope_dim = kr.shape[-1]
    n_pages = page_table.shape[1]
    page = cache_lat.shape[1]
    groups = n_pages // pages
    key_block = min(4, pages) * page
    slots = 2 * groups
    assert n_pages % pages == 0 and (pages * page) % key_block == 0
    rows = n_heads * s_len
    body = functools.partial(_dec_attn_body, n_heads=n_heads, s_len=s_len, rope_dim=rope_dim, pages=pages,
                             page=page, groups=groups, key_block=key_block)
    grid_spec = pltpu.PrefetchScalarGridSpec(
        num_scalar_prefetch=1,
        grid=(db,),
        in_specs=[pl.BlockSpec((None, s_len, nk), lambda b, pt: (b, 0, 0)),
                  pl.BlockSpec((None, s_len, kv_rank), lambda b, pt: (b, 0, 0)),
                  pl.BlockSpec((None, s_len, rope_dim), lambda b, pt: (b, 0, 0)),
                  _const_spec(wukt.shape), _const_spec(wuvp.shape),
                  pl.BlockSpec(memory_space=pl.ANY), pl.BlockSpec(memory_space=pl.ANY)],
        out_specs=pl.BlockSpec((None, s_len, n_heads * dv), lambda b, pt: (b, 0, 0)),
        scratch_shapes=[pltpu.VMEM((slots, pages * page, kv_rank), F32),
                        pltpu.VMEM((slots, rope_dim, pages * page), F32),
                        pltpu.SemaphoreType.DMA((2, slots)),
                        pltpu.VMEM((rows, kv_rank), F32),
                        pltpu.VMEM((rows, rope_dim), F32),
                        pltpu.VMEM((rows, 1), F32),
                        pltpu.VMEM((rows, 1), F32),
                        pltpu.VMEM((rows, kv_rank), F32)])
    return pl.pallas_call(
        body,
        grid_spec=grid_spec,
        out_shape=jax.ShapeDtypeStruct((db, s_len, n_heads * dv), F32),
        compiler_params=_params("arbitrary"),
        name="mla_sample_attn",
    )(page_table.reshape(-1), q, ckv, kr, wukt, wuvp, cache_lat, cache_pe_t)


def _s5_disc_body(are_ref, aim_ref, ldt_ref, bre_ref, bim_ref, cre_ref, cim_ref,
                  abr_ref, abi_ref, bbr_ref, bbi_ref, ccr_ref, cci_ref):
    a_re = are_ref[...]
    a_im = aim_ref[...]
    dt = jnp.exp(ldt_ref[...])
    mag = jnp.exp(dt * a_re)
    abr = mag * jnp.cos(dt * a_im)
    abi = mag * jnp.sin(dt * a_im)
    den = a_re * a_re + a_im * a_im
    nr, ni = abr - 1.0, abi
    fr = (nr * a_re + ni * a_im) / den
    fi = (ni * a_re - nr * a_im) / den
    abr_ref[...] = abr
    abi_ref[...] = abi
    b_re = bre_ref[...]
    b_im = bim_ref[...]
    bbr = fr[:, None, :] * b_re - fi[:, None, :] * b_im
    bbi = fr[:, None, :] * b_im + fi[:, None, :] * b_re
    g, c, n = b_re.shape
    gh = g // bbr_ref.shape[0]
    for ref, blocks in ((bbr_ref, bbr), (bbi_ref, bbi), (ccr_ref, cre_ref[...]), (cci_ref, -cim_ref[...])):
        ref[...] = jnp.zeros(ref.shape, ref.dtype)
        for gi in range(g):
            hf, k = divmod(gi, gh)
            ref[hf, k * c:(k + 1) * c, k * n:(k + 1) * n] = blocks[gi].astype(ref.dtype)


def _s5_disc_call(a_re, a_im, log_dt, b_re_t, b_im_t, c_re, c_im, *, halves):
    g, n = a_re.shape
    c = b_re_t.shape[1]
    gh = g // halves
    op = jax.ShapeDtypeStruct((halves, gh * c, gh * n), BF16)
    return pl.pallas_call(
        _s5_disc_body,
        out_shape=[jax.ShapeDtypeStruct((g, n), F32), jax.ShapeDtypeStruct((g, n), F32), op, op, op, op],
        name="s5_discretise",
    )(a_re, a_im, log_dt.reshape(g, 1), b_re_t, b_im_t, c_re, c_im)


def _s5_body(u_ref, h0r_ref, h0i_ref, ar_ref, ai_ref, bbr_ref, bbi_ref, ccr_ref, cci_ref, d_ref, wglu_ref,
             y_ref, hr_out, hi_out, bur, bui, hr_s, hi_s, *, nb, tc, lane_tiles, unroll):
    c = pl.program_id(0)
    d_ssm = u_ref.shape[-1]
    n_state = hr_s.shape[-1]
    halves = bbr_ref.shape[0]
    ch_half = d_ssm // halves
    st_half = n_state // halves

    @pl.when(c == 0)
    def _init():
        hr_s[...] = h0r_ref[...]
        hi_s[...] = h0i_ref[...]

    u = jnp.swapaxes(u_ref[...], 0, 1).reshape(tc * nb, d_ssm)
    ub = u.astype(BF16)
    tiles_half = st_half // LANES
    for hf in range(halves):
        ublk = ub[:, hf * ch_half:(hf + 1) * ch_half]
        br = _dot(ublk, bbr_ref[hf])
        bi = _dot(ublk, bbi_ref[hf])
        for k in range(tiles_half):
            bur[hf * tiles_half + k] = br[:, k * LANES:(k + 1) * LANES]
            bui[hf * tiles_half + k] = bi[:, k * LANES:(k + 1) * LANES]

    ys = []
    for k0 in range(0, n_state // LANES, lane_tiles):
        tiles = range(k0, k0 + lane_tiles)
        ar = [jnp.broadcast_to(ar_ref[:, k * LANES:(k + 1) * LANES], (nb, LANES)) for k in tiles]
        ai = [jnp.broadcast_to(ai_ref[:, k * LANES:(k + 1) * LANES], (nb, LANES)) for k in tiles]

        def body(t, carry, tiles=tiles, ar=ar, ai=ai):
            rows = pl.ds(pl.multiple_of(t * nb, nb), nb)
            out = []
            for n, k in enumerate(tiles):
                hr, hi = carry[2 * n], carry[2 * n + 1]
                nr = ar[n] * hr - ai[n] * hi + bur[k, rows, :]
                ni = ar[n] * hi + ai[n] * hr + bui[k, rows, :]
                bur[k, rows, :] = nr
                bui[k, rows, :] = ni
                out += [nr, ni]
            return tuple(out)

        init = []
        for k in tiles:
            init += [hr_s[:, k * LANES:(k + 1) * LANES], hi_s[:, k * LANES:(k + 1) * LANES]]
        fin = lax.fori_loop(0, tc, body, tuple(init), unroll=unroll)
        for n, k in enumerate(tiles):
            hr_s[:, k * LANES:(k + 1) * LANES] = fin[2 * n]
            hi_s[:, k * LANES:(k + 1) * LANES] = fin[2 * n + 1]
        if (k0 + lane_tiles) % tiles_half == 0:
            hf = k0 // tiles_half
            hr_hist = jnp.concatenate([bur[hf * tiles_half + k] for k in range(tiles_half)], axis=1).astype(BF16)
            hi_hist = jnp.concatenate([bui[hf * tiles_half + k] for k in range(tiles_half)], axis=1).astype(BF16)
            ys.append(_dot_nt(hr_hist, ccr_ref[hf]) + _dot_nt(hi_hist, cci_ref[hf]))

    hr_out[...] = hr_s[...]
    hi_out[...] = hi_s[...]
    y = jnp.concatenate(ys, axis=1) + d_ref[...] * u
    g = jax.nn.gelu(y)
    out = g * jax.nn.sigmoid(_dot(g.astype(BF16), wglu_ref[...]))
    y_ref[...] = jnp.swapaxes(out.reshape(tc, nb, d_ssm), 0, 1)


def _s5_call(u3, h0r, h0i, ar, ai, bbr, bbi, ccr, cci, d, wglu, *, tc):
    nb, l, d_ssm = u3.shape
    n_state = ar.shape[-1]
    lane_tiles = math.gcd(n_state // LANES // bbr.shape[0], max(1, 32 // nb))
    body = functools.partial(_s5_body, nb=nb, tc=tc, lane_tiles=lane_tiles, unroll=tc)
    return pl.pallas_call(
        body,
        grid=(l // tc,),
        in_specs=[pl.BlockSpec((nb, tc, d_ssm), lambda c: (0, c, 0)),
                  _const_spec(h0r.shape), _const_spec(h0i.shape),
                  _const_spec(ar.shape), _const_spec(ai.shape),
                  _const_spec(bbr.shape), _const_spec(bbi.shape),
                  _const_spec(ccr.shape), _const_spec(cci.shape),
                  _const_spec(d.shape), _const_spec(wglu.shape)],
        out_specs=[pl.BlockSpec((nb, tc, d_ssm), lambda c: (0, c, 0)),
                   pl.BlockSpec((nb, n_state), lambda c: (0, 0)),
                   pl.BlockSpec((nb, n_state), lambda c: (0, 0))],
        out_shape=[jax.ShapeDtypeStruct((nb, l, d_ssm), F32),
                   jax.ShapeDtypeStruct((nb, n_state), F32),
                   jax.ShapeDtypeStruct((nb, n_state), F32)],
        scratch_shapes=[pltpu.VMEM((n_state // LANES, nb * tc, LANES), F32),
                        pltpu.VMEM((n_state // LANES, nb * tc, LANES), F32),
                        pltpu.VMEM((nb, n_state), F32), pltpu.VMEM((nb, n_state), F32)],
        compiler_params=_params("arbitrary"),
        name="s5_scan_glu",
    )(u3, h0r, h0i, ar, ai, bbr, bbi, ccr, cci, d, wglu)


def _memkv_body(m_ref, g_ref, wk_ref, wv_ref, k_ref, v_ref):
    m = _rms(m_ref[...], g_ref[...]).astype(BF16)
    k_ref[...] = _dot(m, wk_ref[...]).reshape(k_ref.shape)
    v_ref[...] = _dot(m, wv_ref[...]).reshape(v_ref.shape)


def _memkv_call(mem2, g, wk, wv, *, tm, mem_heads):
    t, d = mem2.shape
    hd = wk.shape[1] // mem_heads
    out = pl.BlockSpec((tm * mem_heads, hd), lambda i: (i, 0))
    return pl.pallas_call(
        _memkv_body,
        grid=(t // tm,),
        in_specs=[pl.BlockSpec((tm, d), lambda i: (i, 0)),
                  _const_spec(g.shape), _const_spec(wk.shape), _const_spec(wv.shape)],
        out_specs=[out, out],
        out_shape=[jax.ShapeDtypeStruct((t * mem_heads, hd), F32)] * 2,
        compiler_params=_params("parallel"),
        name="mem_kv",
    )(mem2, g, wk, wv)


def _mix_mem_body(x_ref, a_ref, s_ref, mk_ref, mv_ref, gao_ref, gso_ref, woa_ref, wos_ref, gmp_ref,
                  gmem_ref, wq_ref, wo_ref, gmo_ref, o_ref, *, mem_heads, n_mem, seqs, sub, mem_scale):
    def head_rows(ref, si, hh):
        return ref[pl.ds((si * n_mem) * mem_heads + hh, n_mem, stride=mem_heads), :].astype(BF16)

    mem = [[(head_rows(mk_ref, si, hh), head_rows(mv_ref, si, hh)) for hh in range(mem_heads)]
           for si in range(seqs)]
    tile = x_ref.shape[0] // sub
    rows = tile // seqs
    hd = wq_ref.shape[1] // mem_heads
    tiles = [pl.ds(t * tile, tile) for t in range(sub)]
    a = [_rms(a_ref[r, :], gao_ref[...]).astype(BF16) for r in tiles]
    s = [_rms(s_ref[r, :], gso_ref[...]).astype(BF16) for r in tiles]
    mix = [_dot(a[t], woa_ref[...]) + _dot(s[t], wos_ref[...]) for t in range(sub)]
    x = [x_ref[tiles[t], :] + _rms(mix[t], gmp_ref[...]) for t in range(sub)]
    h = [_rms(x[t], gmem_ref[...]).astype(BF16) for t in range(sub)]
    q = [_dot(h[t], wq_ref[...]) for t in range(sub)]
    problems = [(t, si, hh) for t in range(sub) for si in range(seqs) for hh in range(mem_heads)]
    scores = {}
    for t, si, hh in problems:
        qh = q[t][si * rows:(si + 1) * rows, hh * hd:(hh + 1) * hd].astype(BF16)
        scores[t, si, hh] = _dot_nt(qh, mem[si][hh][0]) * mem_scale
    probs = {}
    for key in problems:
        e = jnp.exp(scores[key] - jnp.max(scores[key], axis=1, keepdims=True))
        probs[key] = (e / jnp.sum(e, axis=1, keepdims=True)).astype(BF16)
    outs = {key: _dot(probs[key], mem[key[1]][key[2]][1]) for key in problems}
    for t in range(sub):
        per_seq = [jnp.concatenate([outs[t, si, hh] for hh in range(mem_heads)], axis=1) for si in range(seqs)]
        o = (jnp.concatenate(per_seq, axis=0) if seqs > 1 else per_seq[0]).astype(BF16)
        o_ref[tiles[t], :] = x[t] + _rms(_dot(o, wo_ref[...]), gmo_ref[...])


def _mix_mem_call(x2, a2, s2, mk, mv, gao, gso, woa, wos, gmp, gmem, wq, wo, gmo, *, tm, rows_per_batch,
                  mem_heads, n_mem):
    t, d = x2.shape
    da, ds = a2.shape[1], s2.shape[1]
    hd = mk.shape[1]
    seqs = max(1, tm // rows_per_batch)
    tiles_per_batch = max(1, rows_per_batch // tm)
    sub = tm // 256 if (seqs == 1 and tm % 512 == 0) else 1
    body = functools.partial(_mix_mem_body, mem_heads=mem_heads, n_mem=n_mem, seqs=seqs, sub=sub,
                             mem_scale=hd ** -0.5)
    row = lambda w: pl.BlockSpec((tm, w), lambda i: (i, 0))
    mem = pl.BlockSpec((seqs * n_mem * mem_heads, hd), lambda i: (i // tiles_per_batch, 0))
    consts = [gao, gso, woa, wos, gmp, gmem, wq, wo, gmo]
    return pl.pallas_call(
        body,
        grid=(t // tm,),
        in_specs=[row(d), row(da), row(ds), mem, mem] + [_const_spec(c.shape) for c in consts],
        out_specs=row(d),
        out_shape=jax.ShapeDtypeStruct((t, d), F32),
        compiler_params=_params("parallel"),
        name="mix_out_mem_attn",
    )(x2, a2, s2, mk, mv, *consts)


def _ffn_body(x_ref, cprev_ref, gpre_ref, wg_ref, wu_ref, wd_ref, cw_ref, cb_ref, gpost_ref,
              o_ref, cnew_ref, halo, work, act, *, nseq, lc, fc, halo_rows):
    t = pl.program_id(1)
    d_ff = wg_ref.shape[1]
    tm = nseq * lc
    keep = cprev_ref.shape[1]
    lo = halo_rows - keep

    @pl.when(t == 0)
    def _load_state():
        halo[...] = cprev_ref[...]

    x = x_ref[...]
    h = _rms(x, gpre_ref[...]).astype(BF16)

    for c in range(d_ff // fc):
        cols = slice(c * fc, (c + 1) * fc)
        g = _dot(h, wg_ref[:, cols]).reshape(nseq, lc, fc)
        up = _dot(h, wu_ref[:, cols]).reshape(nseq, lc, fc)
        work[c, :, lo:halo_rows, :] = halo[:, :, cols]
        work[c, :, halo_rows:halo_rows + lc, :] = g
        w = cw_ref[:, cols]
        conv = w[0:1, :] * work[c, :, lo:lo + lc, :]
        for k in range(1, keep):
            conv = conv + w[k:k + 1, :] * work[c, :, lo + k:lo + k + lc, :]
        conv = conv + w[keep:keep + 1, :] * g
        gc = cb_ref[:, cols] + conv
        act[:, cols] = (jax.nn.silu(gc) * up).reshape(tm, fc).astype(BF16)
        tail = work[c, :, lc + lo:lc + halo_rows, :]
        halo[:, :, cols] = tail
        cnew_ref[:, :, cols] = tail

    o_ref[...] = x + _rms(_dot(act[...], wd_ref[...]), gpost_ref[...])


def _ffn_call(x2, cprev, gpre, wg, wu, wd, cw, cb, gpost, *, nseq, lc, fc, n_batch_blocks, tiles_per_batch):
    t, d = x2.shape
    d_ff = wg.shape[1]
    n_chunks = d_ff // fc
    keep = cprev.shape[1]
    halo_rows = 8
    tm = nseq * lc
    body = functools.partial(_ffn_body, nseq=nseq, lc=lc, fc=fc, halo_rows=halo_rows)
    state = pl.BlockSpec((nseq, keep, d_ff), lambda b, i: (b, 0, 0))
    return pl.pallas_call(
        body,
        grid=(n_batch_blocks, tiles_per_batch),
        in_specs=[pl.BlockSpec((tm, d), lambda b, i: (b * tiles_per_batch + i, 0)),
                  state,
                  _const_spec(gpre.shape), _const_spec(wg.shape), _const_spec(wu.shape),
                  _const_spec(wd.shape), _const_spec(cw.shape), _const_spec(cb.shape),
                  _const_spec(gpost.shape)],
        out_specs=[pl.BlockSpec((tm, d), lambda b, i: (b * tiles_per_batch + i, 0)), state],
        out_shape=[jax.ShapeDtypeStruct((t, d), F32),
                   jax.ShapeDtypeStruct(cprev.shape, F32)],
        scratch_shapes=[pltpu.VMEM((nseq, keep, d_ff), F32),
                        pltpu.VMEM((n_chunks, nseq, halo_rows + lc, fc), F32),
                        pltpu.VMEM((tm, d_ff), BF16)],
        compiler_params=_params("parallel", "arbitrary"),
        name="conv_ffn",
    )(x2, cprev, gpre, wg, wu, wd, cw, cb, gpost)


def _rope_tables(pos, rope_dim, nope_dim, q_scale):
    half = rope_dim // 2
    inv = ROPE_THETA ** (-jnp.arange(half, dtype=F32) * (2.0 / rope_dim))
    ang = pos.astype(F32)[:, None] * inv[None, :]
    cos, sin = jnp.cos(ang), jnp.sin(ang)
    n = pos.shape[0]
    pad = jnp.zeros((n, LANES - rope_dim), F32)
    ck = jnp.concatenate([cos, cos, pad], axis=1)
    sk = jnp.concatenate([sin, sin, pad], axis=1)
    ones = jnp.concatenate([jnp.ones((n, nope_dim), F32), jnp.zeros((n, LANES - rope_dim - nope_dim), F32)], axis=1)
    cq = jnp.concatenate([cos, cos, ones], axis=1) * q_scale
    return jnp.stack([ck, sk, cq, sk * q_scale])


def _rot_half_cols(w):
    half = w.shape[-1] // 2
    return jnp.concatenate([-w[..., half:], w[..., :half]], axis=-1)


def _pad_last(w, width):
    return jnp.pad(w, [(0, 0)] * (w.ndim - 1) + [(0, width - w.shape[-1])])


def _layer_weights(l, w_in, q_norm, kv_norm, w_uq, w_uk, w_uv, ssm_a_re, ssm_a_im, ssm_log_dt, ssm_b_re,
                   ssm_b_im, ssm_c_re, ssm_c_im, ssm_d, ssm_w_glu, w_out, w_gate, w_up, w_down, ffn_conv_w,
                   ffn_conv_b, fc):
    q_rank = q_norm.shape[-1]
    kv_rank = kv_norm.shape[-1]
    n_heads = w_uq.shape[2]
    nope = w_uk.shape[3]
    rope_dim = w_uq.shape[3] - nope
    dv = w_uv.shape[3]
    d_ssm = ssm_d.shape[-1]
    win = w_in[l]
    o1, o2, o3 = q_rank, q_rank + kv_rank, q_rank + kv_rank + rope_dim
    w_kr = win[:, o2:o3]
    p = {}
    p["win"] = jnp.concatenate([win[:, :o2], win[:, o3:], _pad_last(w_kr, LANES),
                                _pad_last(_rot_half_cols(w_kr), LANES)], axis=1).astype(BF16)
    uq = w_uq[l]
    q_nope, q_pe = uq[..., :nope], uq[..., nope:]
    wq1 = _pad_last(jnp.concatenate([q_pe, q_nope], axis=-1), LANES).reshape(q_rank, n_heads * LANES)
    wq2 = _pad_last(_rot_half_cols(q_pe), LANES).reshape(q_rank, n_heads * LANES)
    p["wq"] = jnp.concatenate([wq1, wq2], axis=1).astype(BF16)
    uk = w_uk[l]
    wuk_slots = jnp.pad(uk, ((0, 0), (0, 0), (rope_dim, LANES - rope_dim - nope)))
    p["wkv"] = jnp.concatenate([wuk_slots.reshape(kv_rank, n_heads * LANES),
                                _pad_last(w_uv[l], LANES).reshape(kv_rank, n_heads * LANES)], axis=1).astype(BF16)
    p["vone"] = jnp.tile((jnp.arange(LANES) == dv).astype(F32), n_heads).reshape(1, n_heads * LANES)
    p["wukt"] = jnp.transpose(wuk_slots, (1, 2, 0)).astype(BF16)
    uv = jnp.transpose(w_uv[l], (1, 0, 2))
    eye = jnp.eye(n_heads, dtype=F32)
    p["wuvp"] = (uv[:, :, None, :] * eye[:, None, :, None]).reshape(n_heads, kv_rank, n_heads * dv).astype(BF16)
    g, n = ssm_a_re.shape[1:]
    abr, abi, p["bbr"], p["bbi"], p["ccr"], p["cci"] = _s5_disc_call(
        ssm_a_re[l], ssm_a_im[l], ssm_log_dt[l], jnp.transpose(ssm_b_re[l], (0, 2, 1)),
        jnp.transpose(ssm_b_im[l], (0, 2, 1)), ssm_c_re[l], ssm_c_im[l], halves=2)
    p["abr"] = abr.reshape(1, g * n)
    p["abi"] = abi.reshape(1, g * n)
    p["ssm_d"] = ssm_d[l].reshape(1, d_ssm)
    p["wglu"] = ssm_w_glu[l].astype(BF16)
    d_attn = n_heads * dv
    p["woa"] = w_out[l][:d_attn].astype(BF16)
    p["wos"] = w_out[l][d_attn:].astype(BF16)
    d_model, d_ff = w_gate.shape[1:]
    nch = d_ff // fc
    p["wg"] = w_gate[l].astype(BF16)
    p["wu"] = w_up[l].astype(BF16)
    p["wd"] = w_down[l].astype(BF16)
    conv_w = ffn_conv_w.shape[1]
    p["cw"] = ffn_conv_w[l]
    p["cb"] = ffn_conv_b[l].reshape(1, d_ff)
    p["dims"] = dict(q_rank=q_rank, kv_rank=kv_rank, n_heads=n_heads, nope=nope, rope_dim=rope_dim, dv=dv,
                     d_ssm=d_ssm, g=g, n=n, d_ff=d_ff, fc=fc, nch=nch, conv_w=conv_w)
    return p


def _row(v):
    return v.reshape(1, -1)


def kernel(x_prompt, x_sample, mem_prompt, cache_kv_latent, cache_k_rope, page_table, state_ssm_re, state_ssm_im, state_ffn_conv, cache_mem_k, cache_mem_v, norm_mix_pre, w_in, q_norm, kv_norm, w_uq, w_uk, w_uv, ssm_a_re, ssm_a_im, ssm_log_dt, ssm_b_re, ssm_b_im, ssm_c_re, ssm_c_im, ssm_d, ssm_w_glu, norm_attn_out, norm_ssm_out, w_out, norm_mix_post, norm_mem_pre, mem_norm, w_q_mem, w_k_mem, w_v_mem, w_o_mem, norm_mem_post, norm_ffn_pre, w_gate, w_up, ffn_conv_w, ffn_conv_b, w_down, norm_ffn_post):
    depth = w_in.shape[0]
    b, l, d_model = x_prompt.shape
    db, ls, _ = x_sample.shape
    n_mem = mem_prompt.shape[1]
    mem_heads = cache_mem_k.shape[3]
    past_len = page_table.shape[1] * cache_kv_latent.shape[2]
    fc = 256
    tm = min(512, l)
    tm_wide = min(512, l)
    tq = min(512, l)
    tc = min(128, l)
    pages = min(32, page_table.shape[1] // 2)

    xp = x_prompt.reshape(b * l, d_model)
    xs = x_sample.reshape(db * ls, d_model)
    outs = {k: [] for k in ("p_kv", "p_kr", "p_sr", "p_si", "p_cv", "p_mk", "p_mv",
                            "s_kv", "s_kr", "s_sr", "s_si", "s_cv")}
    for li in range(depth):
        p = _layer_weights(li, w_in, q_norm, kv_norm, w_uq, w_uk, w_uv, ssm_a_re, ssm_a_im, ssm_log_dt,
                           ssm_b_re, ssm_b_im, ssm_c_re, ssm_c_im, ssm_d, ssm_w_glu, w_out, w_gate, w_up,
                           w_down, ffn_conv_w, ffn_conv_b, fc)
        dm = p["dims"]
        n_heads, dv, rope_dim, nope = dm["n_heads"], dm["dv"], dm["rope_dim"], dm["nope"]
        g, n, d_ssm = dm["g"], dm["n"], dm["d_ssm"]
        q_scale = (nope + rope_dim) ** -0.5 * LOG2E
        pre_kw = dict(n_heads=n_heads, q_rank=dm["q_rank"], kv_rank=dm["kv_rank"], d_ssm=d_ssm,
                      rope_dim=rope_dim)
        gpre, gq, gkv = _row(norm_mix_pre[li]), _row(q_norm[li]), _row(kv_norm[li])
        mix_consts = (_row(norm_attn_out[li]), _row(norm_ssm_out[li]), p["woa"], p["wos"],
                      _row(norm_mix_post[li]), _row(norm_mem_pre[li]), w_q_mem[li].astype(BF16),
                      w_o_mem[li].astype(BF16), _row(norm_mem_post[li]))
        ffn_consts = (_row(norm_ffn_pre[li]), p["wg"], p["wu"], p["wd"], p["cw"], p["cb"],
                      _row(norm_ffn_post[li]))

        mk, mv = _memkv_call(mem_prompt.reshape(b * n_mem, d_model), _row(mem_norm[li]),
                             w_k_mem[li].astype(BF16), w_v_mem[li].astype(BF16), tm=min(512, b * n_mem),
                             mem_heads=mem_heads)
        tab_p = _rope_tables(jnp.arange(l, dtype=jnp.int32), rope_dim, nope, q_scale)
        q, k, v, ckv, kr, u = _pre_call(xp, tab_p, gpre, gq, gkv, p["win"], p["wq"], p["wkv"], p["vone"],
                                        tm=tm_wide, q_dtype=BF16, **pre_kw)
        attn = _attn_call(q.reshape(b, l, -1), k.reshape(b, l, -1), v.reshape(b, l, -1),
                          n_heads=n_heads, dv=dv, tq=tq)
        zeros_state = jnp.zeros((b, g * n), F32)
        ssm, hr, hi = _s5_call(u.reshape(b, l, d_ssm), zeros_state, zeros_state, p["abr"], p["abi"],
                               p["bbr"], p["bbi"], p["ccr"], p["cci"], p["ssm_d"], p["wglu"], tc=tc)
        xp = _mix_mem_call(xp, attn.reshape(b * l, -1), ssm.reshape(b * l, -1),
                           mk, mv, *mix_consts, tm=min(1024, l), rows_per_batch=l, mem_heads=mem_heads,
                           n_mem=n_mem)
        conv0 = jnp.zeros((b, dm["conv_w"] - 1, dm["d_ff"]), F32)
        xp, cv = _ffn_call(xp, conv0, *ffn_consts, nseq=1, lc=tm, fc=fc, n_batch_blocks=b,
                           tiles_per_batch=l // tm)
        outs["p_kv"].append(ckv.reshape(b, l, -1))
        outs["p_kr"].append(kr.reshape(b, l, -1))
        outs["p_sr"].append(hr.reshape(b, g, n))
        outs["p_si"].append(hi.reshape(b, g, n))
        outs["p_cv"].append(cv)
        outs["p_mk"].append(mk.reshape(b, n_mem, mem_heads, -1))
        outs["p_mv"].append(mv.reshape(b, n_mem, mem_heads, -1))

        ts = db * ls
        pos_s = past_len + jnp.arange(ls, dtype=jnp.int32)
        tab_s = jnp.tile(_rope_tables(pos_s, rope_dim, nope, q_scale), (1, db, 1))
        q, _, _, ckv, kr, u = _pre_call(xs, tab_s, gpre, gq, gkv, p["win"], p["wq"], p["wkv"], p["vone"],
                                        tm=ts, q_dtype=F32, **pre_kw)
        attn = _dec_attn_call(page_table, q.reshape(db, ls, -1), ckv.reshape(db, ls, -1),
                              kr.reshape(db, ls, -1), p["wukt"], p["wuvp"], cache_kv_latent[li],
                              jnp.swapaxes(cache_k_rope[li], 1, 2), n_heads=n_heads, dv=dv, pages=pages)
        ssm, hr, hi = _s5_call(u.reshape(db, ls, d_ssm), state_ssm_re[li].reshape(db, g * n),
                               state_ssm_im[li].reshape(db, g * n), p["abr"], p["abi"], p["bbr"], p["bbi"],
                               p["ccr"], p["cci"], p["ssm_d"], p["wglu"], tc=ls)
        xs = _mix_mem_call(xs, attn.reshape(ts, -1), ssm.reshape(ts, -1),
                           cache_mem_k[li].reshape(db * n_mem * mem_heads, -1),
                           cache_mem_v[li].reshape(db * n_mem * mem_heads, -1),
                           *mix_consts, tm=min(4, db) * ls, rows_per_batch=ls, mem_heads=mem_heads, n_mem=n_mem)
        xs, cv = _ffn_call(xs, state_ffn_conv[li], *ffn_consts, nseq=db, lc=ls, fc=fc, n_batch_blocks=1,
                           tiles_per_batch=1)
        outs["s_kv"].append(ckv.reshape(db, ls, -1))
        outs["s_kr"].append(kr.reshape(db, ls, -1))
        outs["s_sr"].append(hr.reshape(db, g, n))
        outs["s_si"].append(hi.reshape(db, g, n))
        outs["s_cv"].append(cv)

    st = lambda key: jnp.stack(outs[key])
    return (xp.reshape(b, l, d_model), xs.reshape(db, ls, d_model),
            st("p_kv"), st("p_kr"), st("p_sr"), st("p_si"), st("p_cv"), st("p_mk"), st("p_mv"),
            st("s_kv"), st("s_kr"), st("s_sr"), st("s_si"), st("s_cv"))
```

```python
import functools
import math

import jax
import jax.numpy as jnp
import numpy as np
from jax import lax
from jax.experimental import pallas as pl
from jax.experimental.pallas import tpu as pltpu

F32 = jnp.float32
BF16 = jnp.bfloat16

EPS = 1e-6
ROPE_THETA = 10000.0
LANES = 128
SUBLANES = 8
MXU_DIM = 256
VREGS = 64
ROW_TILE = 2 * MXU_DIM
NEG_BIG = -1e30
LOG2E = 1.4426950408889634
VMEM_LIMIT = 56 * 1024 * 1024


def _rms(x, g):
    y = x * lax.rsqrt(jnp.mean(x * x, axis=-1, keepdims=True) + EPS)
    return y * g


def _dot(a, b):
    return jnp.dot(a, b, preferred_element_type=F32)


def _dot_nt(a, b):
    return lax.dot_general(a, b, (((1,), (1,)), ((), ())), preferred_element_type=F32)


def _rep_lanes(x, n):
    return jnp.concatenate([x] * n, axis=1) if n > 1 else x


def _const_spec(shape):
    nd = len(shape)
    return pl.BlockSpec(shape, lambda *_: (0,) * nd, pipeline_mode=pl.Buffered(1))


def _params(*sem):
    return pltpu.CompilerParams(dimension_semantics=sem, vmem_limit_bytes=VMEM_LIMIT)


def _pre_body(x_ref, tab_ref, gpre_ref, gq_ref, gkv_ref, win_ref, wq_ref, wkv_ref, vone_ref,
              q_ref, k_ref, v_ref, ckv_ref, kr_ref, u_ref, *, n_heads, q_rank, kv_rank, d_ssm, rope_dim):
    x = x_ref[...]
    h = _rms(x, gpre_ref[...]).astype(BF16)
    z = _dot(h, win_ref[...])
    o1 = q_rank
    o2 = o1 + kv_rank
    o3 = o2 + d_ssm
    o4 = o3 + LANES
    cq, ckv, u = z[:, :o1], z[:, o1:o2], z[:, o2:o3]
    k1, k2 = z[:, o3:o4], z[:, o4:o4 + LANES]
    krs = k1 * tab_ref[0] + k2 * tab_ref[1]
    kr_ref[...] = krs[:, :rope_dim]
    ckv_n = _rms(ckv, gkv_ref[...])
    ckv_ref[...] = ckv_n
    kv2 = _dot(ckv_n.astype(BF16), wkv_ref[...])
    nk = n_heads * LANES
    k_ref[...] = (kv2[:, :nk] + _rep_lanes(krs, n_heads)).astype(k_ref.dtype)
    v_ref[...] = (kv2[:, nk:] + vone_ref[...]).astype(v_ref.dtype)
    qn = _rms(cq, gq_ref[...]).astype(BF16)
    qq = _dot(qn, wq_ref[...])
    q = qq[:, :nk] * _rep_lanes(tab_ref[2], n_heads) + qq[:, nk:] * _rep_lanes(tab_ref[3], n_heads)
    q_ref[...] = q.astype(q_ref.dtype)
    u_ref[...] = u


def _pre_call(x2, tab, gpre, gq, gkv, win, wq, wkv, vone, *, tm, n_heads, q_rank, kv_rank, d_ssm, rope_dim,
              q_dtype):
    t, d = x2.shape
    ntab = tab.shape[1] // tm
    nk = n_heads * LANES
    row = lambda w: pl.BlockSpec((tm, w), lambda i: (i, 0))
    body = functools.partial(_pre_body, n_heads=n_heads, q_rank=q_rank, kv_rank=kv_rank, d_ssm=d_ssm,
                             rope_dim=rope_dim)
    return pl.pallas_call(
        body,
        grid=(t // tm,),
        in_specs=[row(d),
                  pl.BlockSpec((4, tm, LANES), lambda i: (0, i % ntab, 0)),
                  _const_spec(gpre.shape), _const_spec(gq.shape), _const_spec(gkv.shape),
                  _const_spec(win.shape), _const_spec(wq.shape), _const_spec(wkv.shape),
                  _const_spec(vone.shape)],
        out_specs=[row(nk), row(nk), row(nk), row(kv_rank), row(rope_dim), row(d_ssm)],
        out_shape=[jax.ShapeDtypeStruct((t, nk), q_dtype),
                   jax.ShapeDtypeStruct((t, nk), BF16),
                   jax.ShapeDtypeStruct((t, nk), BF16),
                   jax.ShapeDtypeStruct((t, kv_rank), F32),
                   jax.ShapeDtypeStruct((t, rope_dim), F32),
                   jax.ShapeDtypeStruct((t, d_ssm), F32)],
        compiler_params=_params("parallel"),
        name="pre_proj",
    )(x2, tab, gpre, gq, gkv, win, wq, wkv, vone)


def _attn_body(qi_ref, kj_ref, q_ref, k_ref, v_ref, o_ref, m_ref, acc_ref, *, n_heads, tq, tk, dv):
    pair = pl.program_id(1)
    i = qi_ref[pair]
    j = kj_ref[pair]

    @pl.when(j == 0)
    def _init():
        m_ref[...] = jnp.full(m_ref.shape, NEG_BIG, F32)
        acc_ref[...] = jnp.zeros(acc_ref.shape, F32)

    items = [(e, h) for e in range(q_ref.shape[0]) for h in range(n_heads)]

    def scores(n):
        e, h = items[n]
        return _dot_nt(q_ref[e, :, h * LANES:(h + 1) * LANES], k_ref[e, :, h * LANES:(h + 1) * LANES])

    def step(masked):
        ahead = not masked
        s_next = scores(0) if ahead else None
        for n, (e, h) in enumerate(items):
            s = s_next if ahead else scores(n)
            if ahead and n + 1 < len(items):
                s_next = scores(n + 1)
            if masked:
                row = lax.broadcasted_iota(jnp.int32, (tq, tk), 0)
                col = lax.broadcasted_iota(jnp.int32, (tq, tk), 1)
                s = jnp.where(col <= row, s, NEG_BIG)
            m_prev = m_ref[n]
            m_next = jnp.maximum(m_prev, jnp.max(s, axis=1, keepdims=True))
            alpha = jnp.exp2(m_prev - m_next)
            p = jnp.exp2(s - _rep_lanes(m_next, tk // LANES))
            m_ref[n] = m_next
            acc_ref[n] = acc_ref[n] * alpha + _dot(p.astype(BF16), v_ref[e, :, h * LANES:(h + 1) * LANES])

    @pl.when(j < i)
    def _off_diag():
        step(False)

    @pl.when(j == i)
    def _diag():
        step(True)
        for n, (e, h) in enumerate(items):
            acc = acc_ref[n]
            o_ref[e, :, h * dv:(h + 1) * dv] = acc[:, :dv] / acc[:, dv:dv + 1]


def _attn_call(q, k, v, *, n_heads, dv, tq):
    b, l, nk = q.shape
    tk = tq
    nq = l // tq
    qi = jnp.asarray([i for i in range(nq) for _ in range(i + 1)], jnp.int32)
    kj = jnp.asarray([j for i in range(nq) for j in range(i + 1)], jnp.int32)
    body = functools.partial(_attn_body, n_heads=n_heads, tq=tq, tk=tk, dv=dv)
    eb = 2 if b % 2 == 0 else 1
    grid_spec = pltpu.PrefetchScalarGridSpec(
        num_scalar_prefetch=2,
        grid=(b // eb, qi.shape[0]),
        in_specs=[pl.BlockSpec((eb, tq, nk), lambda bb, p, qi, kj: (bb, qi[p], 0)),
                  pl.BlockSpec((eb, tk, nk), lambda bb, p, qi, kj: (bb, kj[p], 0)),
                  pl.BlockSpec((eb, tk, nk), lambda bb, p, qi, kj: (bb, kj[p], 0))],
        out_specs=pl.BlockSpec((eb, tq, n_heads * dv), lambda bb, p, qi, kj: (bb, qi[p], 0)),
        scratch_shapes=[pltpu.VMEM((eb * n_heads, tq, LANES), F32),
                        pltpu.VMEM((eb * n_heads, tq, LANES), F32)])
    return pl.pallas_call(
        body,
        grid_spec=grid_spec,
        out_shape=jax.ShapeDtypeStruct((b, l, n_heads * dv), F32),
        compiler_params=_params("parallel", "arbitrary"),
        name="mla_prompt_attn",
    )(qi, kj, q, k, v)


def _dec_attn_body(pt_ref, q_ref, ckv_ref, kr_ref, wukt_ref, wuvp_ref, lat_hbm, pe_hbm, o_ref,
                   lat_buf, pe_buf, sems, ql_s, qp_s, m_s, l_s, acc_s,
                   *, n_heads, s_len, rope_dim, pages, page, groups, key_block):
    slots = lat_buf.shape[0]
    b = pl.program_id(0)
    nb = pl.num_programs(0)
    rows = n_heads * s_len

    def page_copies(group, slot, real):
        out = []
        for r in range(pages):
            pg = pt_ref[group * pages + r] if real else 0
            out.append(pltpu.make_async_copy(lat_hbm.at[pg], lat_buf.at[slot, pl.ds(r * page, page), :],
                                             sems.at[0, slot]))
            out.append(pltpu.make_async_copy(pe_hbm.at[pg], pe_buf.at[slot, :, pl.ds(r * page, page)],
                                             sems.at[1, slot]))
        return out

    def issue(group, slot):
        for c in page_copies(group, slot, True):
            c.start()

    def wait(slot):
        for c in page_copies(0, slot, False):
            c.wait()

    @pl.when(b == 0)
    def _prime():
        for g in range(slots):
            issue(g, g)

    for h in range(n_heads):
        qs = q_ref[:, h * LANES:(h + 1) * LANES]
        ql_s[h * s_len:(h + 1) * s_len, :] = _dot(qs.astype(BF16), wukt_ref[h])
        qp_s[h * s_len:(h + 1) * s_len, :] = qs[:, :rope_dim]
    m_s[...] = jnp.full(m_s.shape, NEG_BIG, F32)
    l_s[...] = jnp.zeros(l_s.shape, F32)
    acc_s[...] = jnp.zeros(acc_s.shape, F32)
    ql = ql_s[...].astype(BF16)
    qp = qp_s[...].astype(BF16)

    def probs(s):
        m_b = jnp.max(s, axis=1, keepdims=True)
        p = jnp.exp2(s - m_b)
        return m_b, jnp.sum(p, axis=1, keepdims=True), p.astype(BF16)

    def merge(parts):
        m_prev = m_s[...]
        m_next = m_prev
        for m_b, _, _ in parts:
            m_next = jnp.maximum(m_next, m_b)
        w = jnp.exp2(m_prev - m_next)
        l = l_s[...] * w
        acc = acc_s[...] * w
        for m_b, l_b, o_b in parts:
            w = jnp.exp2(m_b - m_next)
            l = l + l_b * w
            acc = acc + o_b * w
        m_s[...] = m_next
        l_s[...] = l
        acc_s[...] = acc

    for j in range(groups):
        slot = j % slots
        wait(slot)
        lats, scores = [], []
        for kb in range(pages * page // key_block):
            keys = pl.ds(kb * key_block, key_block)
            lat = lat_buf[slot, keys, :].astype(BF16)
            pe_t = pe_buf[slot, :, keys].astype(BF16)
            lats.append(lat)
            scores.append(_dot_nt(ql, lat) + _dot(qp, pe_t))
        stats = [probs(s) for s in scores]
        merge([(m_b, l_b, _dot(p, lat)) for (m_b, l_b, p), lat in zip(stats, lats)])
        nxt = b * groups + j + slots
        if j + slots < groups:
            issue(nxt, slot)
        else:
            @pl.when(b + 1 < nb)
            def _next_batch(nxt=nxt, slot=slot):
                issue(nxt, slot)

    pad = LANES - s_len
    new_lat = jnp.concatenate([ckv_ref[...], jnp.zeros((pad, ckv_ref.shape[1]), F32)], axis=0).astype(BF16)
    new_pe = jnp.concatenate([kr_ref[...], jnp.zeros((pad, rope_dim), F32)], axis=0).astype(BF16)
    s = _dot_nt(ql, new_lat) + _dot_nt(qp, new_pe)
    row = lax.broadcasted_iota(jnp.int32, (rows, LANES), 0)
    col = lax.broadcasted_iota(jnp.int32, (rows, LANES), 1)
    m_b, l_b, p = probs(jnp.where(col <= row % s_len, s, NEG_BIG))
    merge([(m_b, l_b, _dot(p, new_lat))])

    o_lat = (acc_s[...] / l_s[...]).astype(BF16)
    out = _dot(o_lat[0:s_len], wuvp_ref[0])
    for h in range(1, n_heads):
        out = out + _dot(o_lat[h * s_len:(h + 1) * s_len], wuvp_ref[h])
    o_ref[...] = out


def _dec_attn_call(page_table, q, ckv, kr, wukt, wuvp, cache_lat, cache_pe_t, *, n_heads, dv, pages):
    db, s_len, nk = q.shape
    kv_rank = ckv.shape[-1]
    rope_dim = kr.shape[-1]
    n_pages = page_table.shape[1]
    page = cache_lat.shape[1]
    groups = n_pages // pages
    key_block = min(ROW_TILE, pages * page)
    slots = math.gcd(groups, 4)
    assert n_pages % pages == 0 and slots >= 2 and (pages * page) % key_block == 0
    rows = n_heads * s_len
    body = functools.partial(_dec_attn_body, n_heads=n_heads, s_len=s_len, rope_dim=rope_dim, pages=pages,
                             page=page, groups=groups, key_block=key_block)
    grid_spec = pltpu.PrefetchScalarGridSpec(
        num_scalar_prefetch=1,
        grid=(db,),
        in_specs=[pl.BlockSpec((None, s_len, nk), lambda b, pt: (b, 0, 0)),
                  pl.BlockSpec((None, s_len, kv_rank), lambda b, pt: (b, 0, 0)),
                  pl.BlockSpec((None, s_len, rope_dim), lambda b, pt: (b, 0, 0)),
                  _const_spec(wukt.shape), _const_spec(wuvp.shape),
                  pl.BlockSpec(memory_space=pl.ANY), pl.BlockSpec(memory_space=pl.ANY)],
        out_specs=pl.BlockSpec((None, s_len, n_heads * dv), lambda b, pt: (b, 0, 0)),
        scratch_shapes=[pltpu.VMEM((slots, pages * page, kv_rank), F32),
                        pltpu.VMEM((slots, rope_dim, pages * page), F32),
                        pltpu.SemaphoreType.DMA((2, slots)),
                        pltpu.VMEM((rows, kv_rank), F32),
                        pltpu.VMEM((rows, rope_dim), F32),
                        pltpu.VMEM((rows, 1), F32),
                        pltpu.VMEM((rows, 1), F32),
                        pltpu.VMEM((rows, kv_rank), F32)])
    return pl.pallas_call(
        body,
        grid_spec=grid_spec,
        out_shape=jax.ShapeDtypeStruct((db, s_len, n_heads * dv), F32),
        compiler_params=_params("arbitrary"),
        name="mla_sample_attn",
    )(page_table.reshape(-1), q, ckv, kr, wukt, wuvp, cache_lat, cache_pe_t)


def _s5_disc_body(are_ref, aim_ref, ldt_ref, bre_ref, bim_ref, cre_ref, cim_ref,
                  abr_ref, abi_ref, bbr_ref, bbi_ref, ccr_ref, cci_ref):
    a_re = are_ref[...]
    a_im = aim_ref[...]
    dt = jnp.exp(ldt_ref[...])
    mag = jnp.exp(dt * a_re)
    abr = mag * jnp.cos(dt * a_im)
    abi = mag * jnp.sin(dt * a_im)
    den = a_re * a_re + a_im * a_im
    nr, ni = abr - 1.0, abi
    fr = (nr * a_re + ni * a_im) / den
    fi = (ni * a_re - nr * a_im) / den
    abr_ref[...] = abr
    abi_ref[...] = abi
    b_re = bre_ref[...]
    b_im = bim_ref[...]
    bbr = fr[:, None, :] * b_re - fi[:, None, :] * b_im
    bbi = fr[:, None, :] * b_im + fi[:, None, :] * b_re
    g, c, n = b_re.shape
    gh = g // bbr_ref.shape[0]
    for ref, blocks in ((bbr_ref, bbr), (bbi_ref, bbi), (ccr_ref, cre_ref[...]), (cci_ref, -cim_ref[...])):
        ref[...] = jnp.zeros(ref.shape, ref.dtype)
        for gi in range(g):
            hf, k = divmod(gi, gh)
            ref[hf, k * c:(k + 1) * c, k * n:(k + 1) * n] = blocks[gi].astype(ref.dtype)


def _s5_disc_call(a_re, a_im, log_dt, b_re_t, b_im_t, c_re, c_im, *, halves):
    g, n = a_re.shape
    c = b_re_t.shape[1]
    gh = g // halves
    op = jax.ShapeDtypeStruct((halves, gh * c, gh * n), BF16)
    return pl.pallas_call(
        _s5_disc_body,
        out_shape=[jax.ShapeDtypeStruct((g, n), F32), jax.ShapeDtypeStruct((g, n), F32), op, op, op, op],
        name="s5_discretise",
    )(a_re, a_im, log_dt.reshape(g, 1), b_re_t, b_im_t, c_re, c_im)


def _s5_body(u_ref, h0r_ref, h0i_ref, ar_ref, ai_ref, bbr_ref, bbi_ref, ccr_ref, cci_ref, d_ref, wglu_ref,
             y_ref, hr_out, hi_out, bur, bui, hr_s, hi_s, *, nb, tc, lane_tiles, unroll):
    c = pl.program_id(0)
    d_ssm = u_ref.shape[-1]
    n_state = hr_s.shape[-1]
    halves = bbr_ref.shape[0]
    ch_half = d_ssm // halves
    st_half = n_state // halves

    @pl.when(c == 0)
    def _init():
        hr_s[...] = h0r_ref[...]
        hi_s[...] = h0i_ref[...]

    u = jnp.swapaxes(u_ref[...], 0, 1).reshape(tc * nb, d_ssm)
    ub = u.astype(BF16)
    tiles_half = st_half // LANES
    for hf in range(halves):
        ublk = ub[:, hf * ch_half:(hf + 1) * ch_half]
        br = _dot(ublk, bbr_ref[hf])
        bi = _dot(ublk, bbi_ref[hf])
        for k in range(tiles_half):
            bur[hf * tiles_half + k] = br[:, k * LANES:(k + 1) * LANES]
            bui[hf * tiles_half + k] = bi[:, k * LANES:(k + 1) * LANES]

    for k0 in range(0, n_state // LANES, lane_tiles):
        tiles = range(k0, k0 + lane_tiles)
        ar = [jnp.broadcast_to(ar_ref[:, k * LANES:(k + 1) * LANES], (nb, LANES)) for k in tiles]
        ai = [jnp.broadcast_to(ai_ref[:, k * LANES:(k + 1) * LANES], (nb, LANES)) for k in tiles]

        def body(t, carry, tiles=tiles, ar=ar, ai=ai):
            rows = pl.ds(pl.multiple_of(t * nb, nb), nb)
            out = []
            for n, k in enumerate(tiles):
                hr, hi = carry[2 * n], carry[2 * n + 1]
                nr = ar[n] * hr - ai[n] * hi + bur[k, rows, :]
                ni = ar[n] * hi + ai[n] * hr + bui[k, rows, :]
                bur[k, rows, :] = nr
                bui[k, rows, :] = ni
                out += [nr, ni]
            return tuple(out)

        init = []
        for k in tiles:
            init += [hr_s[:, k * LANES:(k + 1) * LANES], hi_s[:, k * LANES:(k + 1) * LANES]]
        fin = lax.fori_loop(0, tc, body, tuple(init), unroll=unroll)
        for n, k in enumerate(tiles):
            hr_s[:, k * LANES:(k + 1) * LANES] = fin[2 * n]
            hi_s[:, k * LANES:(k + 1) * LANES] = fin[2 * n + 1]

    hr_out[...] = hr_s[...]
    hi_out[...] = hi_s[...]

    ys = []
    for hf in range(halves):
        hr_hist = jnp.concatenate([bur[hf * tiles_half + k] for k in range(tiles_half)], axis=1).astype(BF16)
        hi_hist = jnp.concatenate([bui[hf * tiles_half + k] for k in range(tiles_half)], axis=1).astype(BF16)
        ys.append(_dot_nt(hr_hist, ccr_ref[hf]) + _dot_nt(hi_hist, cci_ref[hf]))
    y = jnp.concatenate(ys, axis=1) + d_ref[...] * u
    g = jax.nn.gelu(y)
    out = g * jax.nn.sigmoid(_dot(g.astype(BF16), wglu_ref[...]))
    y_ref[...] = jnp.swapaxes(out.reshape(tc, nb, d_ssm), 0, 1)


def _s5_call(u3, h0r, h0i, ar, ai, bbr, bbi, ccr, cci, d, wglu, *, tc):
    nb, l, d_ssm = u3.shape
    n_state = ar.shape[-1]
    lane_tiles = math.gcd(n_state // LANES, max(1, VREGS * SUBLANES // (16 * nb)))
    body = functools.partial(_s5_body, nb=nb, tc=tc, lane_tiles=lane_tiles, unroll=tc)
    return pl.pallas_call(
        body,
        grid=(l // tc,),
        in_specs=[pl.BlockSpec((nb, tc, d_ssm), lambda c: (0, c, 0)),
                  _const_spec(h0r.shape), _const_spec(h0i.shape),
                  _const_spec(ar.shape), _const_spec(ai.shape),
                  _const_spec(bbr.shape), _const_spec(bbi.shape),
                  _const_spec(ccr.shape), _const_spec(cci.shape),
                  _const_spec(d.shape), _const_spec(wglu.shape)],
        out_specs=[pl.BlockSpec((nb, tc, d_ssm), lambda c: (0, c, 0)),
                   pl.BlockSpec((nb, n_state), lambda c: (0, 0)),
                   pl.BlockSpec((nb, n_state), lambda c: (0, 0))],
        out_shape=[jax.ShapeDtypeStruct((nb, l, d_ssm), F32),
                   jax.ShapeDtypeStruct((nb, n_state), F32),
                   jax.ShapeDtypeStruct((nb, n_state), F32)],
        scratch_shapes=[pltpu.VMEM((n_state // LANES, nb * tc, LANES), F32),
                        pltpu.VMEM((n_state // LANES, nb * tc, LANES), F32),
                        pltpu.VMEM((nb, n_state), F32), pltpu.VMEM((nb, n_state), F32)],
        compiler_params=_params("arbitrary"),
        name="s5_scan_glu",
    )(u3, h0r, h0i, ar, ai, bbr, bbi, ccr, cci, d, wglu)


def _memkv_body(m_ref, g_ref, wk_ref, wv_ref, k_ref, v_ref):
    m = _rms(m_ref[...], g_ref[...]).astype(BF16)
    k_ref[...] = _dot(m, wk_ref[...]).reshape(k_ref.shape)
    v_ref[...] = _dot(m, wv_ref[...]).reshape(v_ref.shape)


def _memkv_call(mem2, g, wk, wv, *, tm, mem_heads):
    t, d = mem2.shape
    hd = wk.shape[1] // mem_heads
    out = pl.BlockSpec((tm * mem_heads, hd), lambda i: (i, 0))
    return pl.pallas_call(
        _memkv_body,
        grid=(t // tm,),
        in_specs=[pl.BlockSpec((tm, d), lambda i: (i, 0)),
                  _const_spec(g.shape), _const_spec(wk.shape), _const_spec(wv.shape)],
        out_specs=[out, out],
        out_shape=[jax.ShapeDtypeStruct((t * mem_heads, hd), F32)] * 2,
        compiler_params=_params("parallel"),
        name="mem_kv",
    )(mem2, g, wk, wv)


def _mix_mem_body(x_ref, a_ref, s_ref, mk_ref, mv_ref, gao_ref, gso_ref, woa_ref, wos_ref, gmp_ref,
                  gmem_ref, wq_ref, wo_ref, gmo_ref, o_ref, *, mem_heads, n_mem, seqs, sub, mem_scale):
    def head_rows(ref, si, hh):
        return ref[pl.ds((si * n_mem) * mem_heads + hh, n_mem, stride=mem_heads), :].astype(BF16)

    mem = [[(head_rows(mk_ref, si, hh), head_rows(mv_ref, si, hh)) for hh in range(mem_heads)]
           for si in range(seqs)]
    tile = x_ref.shape[0] // sub
    rows = tile // seqs
    hd = wq_ref.shape[1] // mem_heads
    tiles = [pl.ds(t * tile, tile) for t in range(sub)]
    a = [_rms(a_ref[r, :], gao_ref[...]).astype(BF16) for r in tiles]
    s = [_rms(s_ref[r, :], gso_ref[...]).astype(BF16) for r in tiles]
    mix = [_dot(a[t], woa_ref[...]) + _dot(s[t], wos_ref[...]) for t in range(sub)]
    x = [x_ref[tiles[t], :] + _rms(mix[t], gmp_ref[...]) for t in range(sub)]
    h = [_rms(x[t], gmem_ref[...]).astype(BF16) for t in range(sub)]
    q = [_dot(h[t], wq_ref[...]) for t in range(sub)]
    problems = [(t, si, hh) for t in range(sub) for si in range(seqs) for hh in range(mem_heads)]
    scores = {}
    for t, si, hh in problems:
        qh = q[t][si * rows:(si + 1) * rows, hh * hd:(hh + 1) * hd].astype(BF16)
        scores[t, si, hh] = _dot_nt(qh, mem[si][hh][0]) * mem_scale
    probs = {}
    for key in problems:
        e = jnp.exp(scores[key] - jnp.max(scores[key], axis=1, keepdims=True))
        probs[key] = (e / jnp.sum(e, axis=1, keepdims=True)).astype(BF16)
    outs = {key: _dot(probs[key], mem[key[1]][key[2]][1]) for key in problems}
    for t in range(sub):
        per_seq = [jnp.concatenate([outs[t, si, hh] for hh in range(mem_heads)], axis=1) for si in range(seqs)]
        o = (jnp.concatenate(per_seq, axis=0) if seqs > 1 else per_seq[0]).astype(BF16)
        o_ref[tiles[t], :] = x[t] + _rms(_dot(o, wo_ref[...]), gmo_ref[...])


def _mix_mem_call(x2, a2, s2, mk, mv, gao, gso, woa, wos, gmp, gmem, wq, wo, gmo, *, tm, rows_per_batch,
                  mem_heads, n_mem):
    t, d = x2.shape
    da, ds = a2.shape[1], s2.shape[1]
    hd = mk.shape[1]
    seqs = max(1, tm // rows_per_batch)
    tiles_per_batch = max(1, rows_per_batch // tm)
    sub = tm // MXU_DIM if (seqs == 1 and tm % ROW_TILE == 0) else 1
    body = functools.partial(_mix_mem_body, mem_heads=mem_heads, n_mem=n_mem, seqs=seqs, sub=sub,
                             mem_scale=hd ** -0.5)
    row = lambda w: pl.BlockSpec((tm, w), lambda i: (i, 0))
    mem = pl.BlockSpec((seqs * n_mem * mem_heads, hd), lambda i: (i // tiles_per_batch, 0))
    consts = [gao, gso, woa, wos, gmp, gmem, wq, wo, gmo]
    return pl.pallas_call(
        body,
        grid=(t // tm,),
        in_specs=[row(d), row(da), row(ds), mem, mem] + [_const_spec(c.shape) for c in consts],
        out_specs=row(d),
        out_shape=jax.ShapeDtypeStruct((t, d), F32),
        compiler_params=_params("parallel"),
        name="mix_out_mem_attn",
    )(x2, a2, s2, mk, mv, *consts)


def _ffn_body(x_ref, cprev_ref, gpre_ref, wg_ref, wu_ref, wd_ref, cw_ref, cb_ref, gpost_ref,
              o_ref, cnew_ref, halo, work, act, *, nseq, lc, fc, halo_rows):
    t = pl.program_id(1)
    d_ff = wg_ref.shape[1]
    tm = nseq * lc
    keep = cprev_ref.shape[1]
    lo = halo_rows - keep

    @pl.when(t == 0)
    def _load_state():
        halo[...] = cprev_ref[...]

    x = x_ref[...]
    h = _rms(x, gpre_ref[...]).astype(BF16)

    for c in range(d_ff // fc):
        cols = slice(c * fc, (c + 1) * fc)
        g = _dot(h, wg_ref[:, cols]).reshape(nseq, lc, fc)
        up = _dot(h, wu_ref[:, cols]).reshape(nseq, lc, fc)
        work[c, :, lo:halo_rows, :] = halo[:, :, cols]
        work[c, :, halo_rows:halo_rows + lc, :] = g
        w = cw_ref[:, cols]
        conv = w[0:1, :] * work[c, :, lo:lo + lc, :]
        for k in range(1, keep):
            conv = conv + w[k:k + 1, :] * work[c, :, lo + k:lo + k + lc, :]
        conv = conv + w[keep:keep + 1, :] * g
        gc = cb_ref[:, cols] + conv
        act[:, cols] = (jax.nn.silu(gc) * up).reshape(tm, fc).astype(BF16)
        tail = work[c, :, lc + lo:lc + halo_rows, :]
        halo[:, :, cols] = tail
        cnew_ref[:, :, cols] = tail

    o_ref[...] = x + _rms(_dot(act[...], wd_ref[...]), gpost_ref[...])


def _ffn_call(x2, cprev, gpre, wg, wu, wd, cw, cb, gpost, *, nseq, lc, fc, n_batch_blocks, tiles_per_batch):
    t, d = x2.shape
    d_ff = wg.shape[1]
    n_chunks = d_ff // fc
    keep = cprev.shape[1]
    halo_rows = SUBLANES
    tm = nseq * lc
    body = functools.partial(_ffn_body, nseq=nseq, lc=lc, fc=fc, halo_rows=halo_rows)
    state = pl.BlockSpec((nseq, keep, d_ff), lambda b, i: (b, 0, 0))
    return pl.pallas_call(
        body,
        grid=(n_batch_blocks, tiles_per_batch),
        in_specs=[pl.BlockSpec((tm, d), lambda b, i: (b * tiles_per_batch + i, 0)),
                  state,
                  _const_spec(gpre.shape), _const_spec(wg.shape), _const_spec(wu.shape),
                  _const_spec(wd.shape), _const_spec(cw.shape), _const_spec(cb.shape),
                  _const_spec(gpost.shape)],
        out_specs=[pl.BlockSpec((tm, d), lambda b, i: (b * tiles_per_batch + i, 0)), state],
        out_shape=[jax.ShapeDtypeStruct((t, d), F32),
                   jax.ShapeDtypeStruct(cprev.shape, F32)],
        scratch_shapes=[pltpu.VMEM((nseq, keep, d_ff), F32),
                        pltpu.VMEM((n_chunks, nseq, halo_rows + lc, fc), F32),
                        pltpu.VMEM((tm, d_ff), BF16)],
        compiler_params=_params("parallel", "arbitrary"),
        name="conv_ffn",
    )(x2, cprev, gpre, wg, wu, wd, cw, cb, gpost)


def _rope_tables(pos, rope_dim, nope_dim, q_scale):
    half = rope_dim // 2
    inv = ROPE_THETA ** (-np.arange(half, dtype=np.float64) * (2.0 / rope_dim))
    ang = np.asarray(pos, np.float64)[:, None] * inv[None, :]
    cos, sin = np.cos(ang), np.sin(ang)
    n = ang.shape[0]
    pad = np.zeros((n, LANES - rope_dim))
    ck = np.concatenate([cos, cos, pad], axis=1)
    sk = np.concatenate([sin, sin, pad], axis=1)
    ones = np.concatenate([np.ones((n, nope_dim)), np.zeros((n, LANES - rope_dim - nope_dim))], axis=1)
    cq = np.concatenate([cos, cos, ones], axis=1) * q_scale
    return np.stack([ck, sk, cq, sk * q_scale]).astype(np.float32)


def _rot_half_cols(w):
    half = w.shape[-1] // 2
    return jnp.concatenate([-w[..., half:], w[..., :half]], axis=-1)


def _pad_last(w, width):
    return jnp.pad(w, [(0, 0)] * (w.ndim - 1) + [(0, width - w.shape[-1])])


def _layer_weights(l, w_in, q_norm, kv_norm, w_uq, w_uk, w_uv, ssm_a_re, ssm_a_im, ssm_log_dt, ssm_b_re,
                   ssm_b_im, ssm_c_re, ssm_c_im, ssm_d, ssm_w_glu, w_out, w_gate, w_up, w_down, ffn_conv_w,
                   ffn_conv_b):
    q_rank = q_norm.shape[-1]
    kv_rank = kv_norm.shape[-1]
    n_heads = w_uq.shape[2]
    nope = w_uk.shape[3]
    rope_dim = w_uq.shape[3] - nope
    dv = w_uv.shape[3]
    d_ssm = ssm_d.shape[-1]
    win = w_in[l]
    o2, o3 = q_rank + kv_rank, q_rank + kv_rank + rope_dim
    w_kr = win[:, o2:o3]
    p = {}
    p["win"] = jnp.concatenate([win[:, :o2], win[:, o3:], _pad_last(w_kr, LANES),
                                _pad_last(_rot_half_cols(w_kr), LANES)], axis=1).astype(BF16)
    uq = w_uq[l]
    q_nope, q_pe = uq[..., :nope], uq[..., nope:]
    wq1 = _pad_last(jnp.concatenate([q_pe, q_nope], axis=-1), LANES).reshape(q_rank, n_heads * LANES)
    wq2 = _pad_last(_rot_half_cols(q_pe), LANES).reshape(q_rank, n_heads * LANES)
    p["wq"] = jnp.concatenate([wq1, wq2], axis=1).astype(BF16)
    uk = w_uk[l]
    wuk_slots = jnp.pad(uk, ((0, 0), (0, 0), (rope_dim, LANES - rope_dim - nope)))
    p["wkv"] = jnp.concatenate([wuk_slots.reshape(kv_rank, n_heads * LANES),
                                _pad_last(w_uv[l], LANES).reshape(kv_rank, n_heads * LANES)], axis=1).astype(BF16)
    p["vone"] = jnp.tile((jnp.arange(LANES) == dv).astype(F32), n_heads).reshape(1, n_heads * LANES)
    p["wukt"] = jnp.transpose(wuk_slots, (1, 2, 0)).astype(BF16)
    uv = jnp.transpose(w_uv[l], (1, 0, 2))
    eye = jnp.eye(n_heads, dtype=F32)
    p["wuvp"] = (uv[:, :, None, :] * eye[:, None, :, None]).reshape(n_heads, kv_rank, n_heads * dv).astype(BF16)
    g, n = ssm_a_re.shape[1:]
    abr, abi, p["bbr"], p["bbi"], p["ccr"], p["cci"] = _s5_disc_call(
        ssm_a_re[l], ssm_a_im[l], ssm_log_dt[l], jnp.transpose(ssm_b_re[l], (0, 2, 1)),
        jnp.transpose(ssm_b_im[l], (0, 2, 1)), ssm_c_re[l], ssm_c_im[l], halves=2)
    p["abr"] = abr.reshape(1, g * n)
    p["abi"] = abi.reshape(1, g * n)
    p["ssm_d"] = ssm_d[l].reshape(1, d_ssm)
    p["wglu"] = ssm_w_glu[l].astype(BF16)
    d_attn = n_heads * dv
    p["woa"] = w_out[l][:d_attn].astype(BF16)
    p["wos"] = w_out[l][d_attn:].astype(BF16)
    d_ff = w_gate.shape[2]
    p["wg"] = w_gate[l].astype(BF16)
    p["wu"] = w_up[l].astype(BF16)
    p["wd"] = w_down[l].astype(BF16)
    conv_w = ffn_conv_w.shape[1]
    p["cw"] = jnp.pad(ffn_conv_w[l], ((0, -conv_w % SUBLANES), (0, 0)))
    p["cb"] = ffn_conv_b[l].reshape(1, d_ff)
    p["dims"] = dict(q_rank=q_rank, kv_rank=kv_rank, n_heads=n_heads, nope=nope, rope_dim=rope_dim, dv=dv,
                     d_ssm=d_ssm, g=g, n=n, d_ff=d_ff, conv_w=conv_w)
    return p


def _row(v):
    return v.reshape(1, -1)


def kernel(x_prompt, x_sample, mem_prompt, cache_kv_latent, cache_k_rope, page_table, state_ssm_re, state_ssm_im, state_ffn_conv, cache_mem_k, cache_mem_v, norm_mix_pre, w_in, q_norm, kv_norm, w_uq, w_uk, w_uv, ssm_a_re, ssm_a_im, ssm_log_dt, ssm_b_re, ssm_b_im, ssm_c_re, ssm_c_im, ssm_d, ssm_w_glu, norm_attn_out, norm_ssm_out, w_out, norm_mix_post, norm_mem_pre, mem_norm, w_q_mem, w_k_mem, w_v_mem, w_o_mem, norm_mem_post, norm_ffn_pre, w_gate, w_up, ffn_conv_w, ffn_conv_b, w_down, norm_ffn_post):
    depth = w_in.shape[0]
    b, l, d_model = x_prompt.shape
    db, ls, _ = x_sample.shape
    n_mem = mem_prompt.shape[1]
    mem_heads = cache_mem_k.shape[3]
    past_len = page_table.shape[1] * cache_kv_latent.shape[2]
    fc = MXU_DIM
    tm = min(ROW_TILE, l)
    tm_mix = min(2 * ROW_TILE, l)
    tq = min(ROW_TILE, l)
    tc = min(LANES, l)
    pages = min(ROW_TILE * SUBLANES // cache_kv_latent.shape[2], page_table.shape[1] // 2)

    xp = x_prompt.reshape(b * l, d_model)
    xs = x_sample.reshape(db * ls, d_model)
    outs = {k: [] for k in ("p_kv", "p_kr", "p_sr", "p_si", "p_cv", "p_mk", "p_mv",
                            "s_kv", "s_kr", "s_sr", "s_si", "s_cv")}
    for li in range(depth):
        p = _layer_weights(li, w_in, q_norm, kv_norm, w_uq, w_uk, w_uv, ssm_a_re, ssm_a_im, ssm_log_dt,
                           ssm_b_re, ssm_b_im, ssm_c_re, ssm_c_im, ssm_d, ssm_w_glu, w_out, w_gate, w_up,
                           w_down, ffn_conv_w, ffn_conv_b)
        dm = p["dims"]
        n_heads, dv, rope_dim, nope = dm["n_heads"], dm["dv"], dm["rope_dim"], dm["nope"]
        g, n, d_ssm = dm["g"], dm["n"], dm["d_ssm"]
        q_scale = (nope + rope_dim) ** -0.5 * LOG2E
        pre_kw = dict(n_heads=n_heads, q_rank=dm["q_rank"], kv_rank=dm["kv_rank"], d_ssm=d_ssm,
                      rope_dim=rope_dim)
        gpre, gq, gkv = _row(norm_mix_pre[li]), _row(q_norm[li]), _row(kv_norm[li])
        mix_consts = (_row(norm_attn_out[li]), _row(norm_ssm_out[li]), p["woa"], p["wos"],
                      _row(norm_mix_post[li]), _row(norm_mem_pre[li]), w_q_mem[li].astype(BF16),
                      w_o_mem[li].astype(BF16), _row(norm_mem_post[li]))
        ffn_consts = (_row(norm_ffn_pre[li]), p["wg"], p["wu"], p["wd"], p["cw"], p["cb"],
                      _row(norm_ffn_post[li]))

        mk, mv = _memkv_call(mem_prompt.reshape(b * n_mem, d_model), _row(mem_norm[li]),
                             w_k_mem[li].astype(BF16), w_v_mem[li].astype(BF16), tm=min(ROW_TILE, b * n_mem),
                             mem_heads=mem_heads)
        tab_p = _rope_tables(np.arange(l), rope_dim, nope, q_scale)
        q, k, v, ckv, kr, u = _pre_call(xp, tab_p, gpre, gq, gkv, p["win"], p["wq"], p["wkv"], p["vone"],
                                        tm=tm, q_dtype=BF16, **pre_kw)
        attn = _attn_call(q.reshape(b, l, -1), k.reshape(b, l, -1), v.reshape(b, l, -1),
                          n_heads=n_heads, dv=dv, tq=tq)
        zeros_state = np.zeros((b, g * n), np.float32)
        ssm, hr, hi = _s5_call(u.reshape(b, l, d_ssm), zeros_state, zeros_state, p["abr"], p["abi"],
                               p["bbr"], p["bbi"], p["ccr"], p["cci"], p["ssm_d"], p["wglu"], tc=tc)
        xp = _mix_mem_call(xp, attn.reshape(b * l, -1), ssm.reshape(b * l, -1),
                           mk, mv, *mix_consts, tm=tm_mix, rows_per_batch=l, mem_heads=mem_heads,
                           n_mem=n_mem)
        conv0 = np.zeros((b, dm["conv_w"] - 1, dm["d_ff"]), np.float32)
        xp, cv = _ffn_call(xp, conv0, *ffn_consts, nseq=1, lc=tm, fc=fc, n_batch_blocks=b,
                           tiles_per_batch=l // tm)
        outs["p_kv"].append(ckv.reshape(b, l, -1))
        outs["p_kr"].append(kr.reshape(b, l, -1))
        outs["p_sr"].append(hr.reshape(b, g, n))
        outs["p_si"].append(hi.reshape(b, g, n))
        outs["p_cv"].append(cv)
        outs["p_mk"].append(mk.reshape(b, n_mem, mem_heads, -1))
        outs["p_mv"].append(mv.reshape(b, n_mem, mem_heads, -1))

        ts = db * ls
        tab_s = np.tile(_rope_tables(past_len + np.arange(ls), rope_dim, nope, q_scale), (1, db, 1))
        q, _, _, ckv, kr, u = _pre_call(xs, tab_s, gpre, gq, gkv, p["win"], p["wq"], p["wkv"], p["vone"],
                                        tm=ts, q_dtype=F32, **pre_kw)
        attn = _dec_attn_call(page_table, q.reshape(db, ls, -1), ckv.reshape(db, ls, -1),
                              kr.reshape(db, ls, -1), p["wukt"], p["wuvp"], cache_kv_latent[li],
                              jnp.swapaxes(cache_k_rope[li], 1, 2), n_heads=n_heads, dv=dv, pages=pages)
        ssm, hr, hi = _s5_call(u.reshape(db, ls, d_ssm), state_ssm_re[li].reshape(db, g * n),
                               state_ssm_im[li].reshape(db, g * n), p["abr"], p["abi"], p["bbr"], p["bbi"],
                               p["ccr"], p["cci"], p["ssm_d"], p["wglu"], tc=ls)
        xs = _mix_mem_call(xs, attn.reshape(ts, -1), ssm.reshape(ts, -1),
                           cache_mem_k[li].reshape(db * n_mem * mem_heads, -1),
                           cache_mem_v[li].reshape(db * n_mem * mem_heads, -1),
                           *mix_consts, tm=min(SUBLANES, db) * ls, rows_per_batch=ls, mem_heads=mem_heads,
                           n_mem=n_mem)
        xs, cv = _ffn_call(xs, state_ffn_conv[li], *ffn_consts, nseq=db, lc=ls, fc=fc, n_batch_blocks=1,
                           tiles_per_batch=1)
        outs["s_kv"].append(ckv.reshape(db, ls, -1))
        outs["s_kr"].append(kr.reshape(db, ls, -1))
        outs["s_sr"].append(hr.reshape(db, g, n))
        outs["s_si"].append(hi.reshape(db, g, n))
        outs["s_cv"].append(cv)

    st = lambda key: jnp.stack(outs[key])
    return (xp.reshape(b, l, d_model), xs.reshape(db, ls, d_model),
            st("p_kv"), st("p_kr"), st("p_sr"), st("p_si"), st("p_cv"), st("p_mk"), st("p_mv"),
            st("s_kv"), st("s_kr"), st("s_sr"), st("s_si"), st("s_cv"))
```

```python
import functools
import math

import jax
import jax.numpy as jnp
import numpy as np
from jax import lax
from jax.experimental import pallas as pl
from jax.experimental.pallas import tpu as pltpu

F32 = jnp.float32
BF16 = jnp.bfloat16

EPS = 1e-6
ROPE_THETA = 10000.0
LANES = 128
SUBLANES = 8
MXU_DIM = 256
VREGS = 64
ROW_TILE = 2 * MXU_DIM
NEG_BIG = -1e30
LOG2E = 1.4426950408889634
VMEM_LIMIT = 56 * 1024 * 1024


def _rms(x, g):
    y = x * lax.rsqrt(jnp.mean(x * x, axis=-1, keepdims=True) + EPS)
    return y * g


def _dot(a, b):
    return jnp.dot(a, b, preferred_element_type=F32)


def _dot_nt(a, b):
    return lax.dot_general(a, b, (((1,), (1,)), ((), ())), preferred_element_type=F32)


def _rep_lanes(x, n):
    return jnp.concatenate([x] * n, axis=1) if n > 1 else x


def _const_spec(shape):
    nd = len(shape)
    return pl.BlockSpec(shape, lambda *_: (0,) * nd, pipeline_mode=pl.Buffered(1))


def _params(*sem):
    return pltpu.CompilerParams(dimension_semantics=sem, vmem_limit_bytes=VMEM_LIMIT)


def _pre_body(x_ref, tab_ref, gpre_ref, gq_ref, gkv_ref, win_ref, wq_ref, wkv_ref, vone_ref,
              q_ref, k_ref, v_ref, ckv_ref, kr_ref, u_ref, *, n_heads, q_rank, kv_rank, d_ssm, rope_dim):
    x = x_ref[...]
    h = _rms(x, gpre_ref[...]).astype(BF16)
    z = _dot(h, win_ref[...])
    o1 = q_rank
    o2 = o1 + kv_rank
    o3 = o2 + d_ssm
    o4 = o3 + LANES
    cq, ckv, u = z[:, :o1], z[:, o1:o2], z[:, o2:o3]
    k1, k2 = z[:, o3:o4], z[:, o4:o4 + LANES]
    krs = k1 * tab_ref[0] + k2 * tab_ref[1]
    kr_ref[...] = krs[:, :rope_dim]
    ckv_n = _rms(ckv, gkv_ref[...])
    ckv_ref[...] = ckv_n
    kv2 = _dot(ckv_n.astype(BF16), wkv_ref[...])
    nk = n_heads * LANES
    k_ref[...] = (kv2[:, :nk] + _rep_lanes(krs, n_heads)).astype(k_ref.dtype)
    v_ref[...] = (kv2[:, nk:] + vone_ref[...]).astype(v_ref.dtype)
    qn = _rms(cq, gq_ref[...]).astype(BF16)
    qq = _dot(qn, wq_ref[...])
    q = qq[:, :nk] * _rep_lanes(tab_ref[2], n_heads) + qq[:, nk:] * _rep_lanes(tab_ref[3], n_heads)
    q_ref[...] = q.astype(q_ref.dtype)
    u_ref[...] = u


def _pre_call(x2, tab, gpre, gq, gkv, win, wq, wkv, vone, *, tm, n_heads, q_rank, kv_rank, d_ssm, rope_dim,
              q_dtype):
    t, d = x2.shape
    ntab = tab.shape[1] // tm
    nk = n_heads * LANES
    row = lambda w: pl.BlockSpec((tm, w), lambda i: (i, 0))
    body = functools.partial(_pre_body, n_heads=n_heads, q_rank=q_rank, kv_rank=kv_rank, d_ssm=d_ssm,
                             rope_dim=rope_dim)
    return pl.pallas_call(
        body,
        grid=(t // tm,),
        in_specs=[row(d),
                  pl.BlockSpec((4, tm, LANES), lambda i: (0, i % ntab, 0)),
                  _const_spec(gpre.shape), _const_spec(gq.shape), _const_spec(gkv.shape),
                  _const_spec(win.shape), _const_spec(wq.shape), _const_spec(wkv.shape),
                  _const_spec(vone.shape)],
        out_specs=[row(nk), row(nk), row(nk), row(kv_rank), row(rope_dim), row(d_ssm)],
        out_shape=[jax.ShapeDtypeStruct((t, nk), q_dtype),
                   jax.ShapeDtypeStruct((t, nk), BF16),
                   jax.ShapeDtypeStruct((t, nk), BF16),
                   jax.ShapeDtypeStruct((t, kv_rank), F32),
                   jax.ShapeDtypeStruct((t, rope_dim), F32),
                   jax.ShapeDtypeStruct((t, d_ssm), F32)],
        compiler_params=_params("parallel"),
        name="pre_proj",
    )(x2, tab, gpre, gq, gkv, win, wq, wkv, vone)


def _attn_body(qi_ref, kj_ref, q_ref, k_ref, v_ref, o_ref, m_ref, acc_ref, *, n_heads, tq, tk, dv):
    pair = pl.program_id(1)
    i = qi_ref[pair]
    j = kj_ref[pair]

    @pl.when(j == 0)
    def _init():
        m_ref[...] = jnp.full(m_ref.shape, NEG_BIG, F32)
        acc_ref[...] = jnp.zeros(acc_ref.shape, F32)

    items = [(e, h) for e in range(q_ref.shape[0]) for h in range(n_heads)]

    def scores(n):
        e, h = items[n]
        return _dot_nt(q_ref[e, :, h * LANES:(h + 1) * LANES], k_ref[e, :, h * LANES:(h + 1) * LANES])

    def step(masked):
        ahead = not masked
        s_next = scores(0) if ahead else None
        for n, (e, h) in enumerate(items):
            s = s_next if ahead else scores(n)
            if ahead and n + 1 < len(items):
                s_next = scores(n + 1)
            if masked:
                row = lax.broadcasted_iota(jnp.int32, (tq, tk), 0)
                col = lax.broadcasted_iota(jnp.int32, (tq, tk), 1)
                s = jnp.where(col <= row, s, NEG_BIG)
            m_prev = m_ref[n]
            m_next = jnp.maximum(m_prev, jnp.max(s, axis=1, keepdims=True))
            alpha = jnp.exp2(m_prev - m_next)
            p = jnp.exp2(s - _rep_lanes(m_next, tk // LANES))
            m_ref[n] = m_next
            acc_ref[n] = acc_ref[n] * alpha + _dot(p.astype(BF16), v_ref[e, :, h * LANES:(h + 1) * LANES])

    @pl.when(j < i)
    def _off_diag():
        step(False)

    @pl.when(j == i)
    def _diag():
        step(True)
        for n, (e, h) in enumerate(items):
            acc = acc_ref[n]
            o_ref[e, :, h * dv:(h + 1) * dv] = acc[:, :dv] / acc[:, dv:dv + 1]


def _attn_call(q, k, v, *, n_heads, dv, tq):
    b, l, nk = q.shape
    tk = tq
    nq = l // tq
    qi = jnp.asarray([i for i in range(nq) for _ in range(i + 1)], jnp.int32)
    kj = jnp.asarray([j for i in range(nq) for j in range(i + 1)], jnp.int32)
    body = functools.partial(_attn_body, n_heads=n_heads, tq=tq, tk=tk, dv=dv)
    eb = 2 if b % 2 == 0 else 1
    grid_spec = pltpu.PrefetchScalarGridSpec(
        num_scalar_prefetch=2,
        grid=(b // eb, qi.shape[0]),
        in_specs=[pl.BlockSpec((eb, tq, nk), lambda bb, p, qi, kj: (bb, qi[p], 0)),
                  pl.BlockSpec((eb, tk, nk), lambda bb, p, qi, kj: (bb, kj[p], 0)),
                  pl.BlockSpec((eb, tk, nk), lambda bb, p, qi, kj: (bb, kj[p], 0))],
        out_specs=pl.BlockSpec((eb, tq, n_heads * dv), lambda bb, p, qi, kj: (bb, qi[p], 0)),
        scratch_shapes=[pltpu.VMEM((eb * n_heads, tq, LANES), F32),
                        pltpu.VMEM((eb * n_heads, tq, LANES), F32)])
    return pl.pallas_call(
        body,
        grid_spec=grid_spec,
        out_shape=jax.ShapeDtypeStruct((b, l, n_heads * dv), F32),
        compiler_params=_params("parallel", "arbitrary"),
        name="mla_prompt_attn",
    )(qi, kj, q, k, v)


def _dec_attn_body(pt_ref, q_ref, ckv_ref, kr_ref, wukt_ref, wuvp_ref, lat_hbm, pe_hbm, o_ref,
                   lat_buf, pe_buf, sems, ql_s, qp_s, m_s, l_s, acc_s,
                   *, n_heads, s_len, rope_dim, pages, page, groups, key_block):
    slots = lat_buf.shape[0]
    b = pl.program_id(0)
    nb = pl.num_programs(0)
    rows = n_heads * s_len

    def page_copies(group, slot, real):
        out = []
        for r in range(pages):
            pg = pt_ref[group * pages + r] if real else 0
            out.append(pltpu.make_async_copy(lat_hbm.at[pg], lat_buf.at[slot, pl.ds(r * page, page), :],
                                             sems.at[0, slot]))
            out.append(pltpu.make_async_copy(pe_hbm.at[pg], pe_buf.at[slot, :, pl.ds(r * page, page)],
                                             sems.at[1, slot]))
        return out

    def issue(group, slot):
        for c in page_copies(group, slot, True):
            c.start()

    def wait(slot):
        for c in page_copies(0, slot, False):
            c.wait()

    @pl.when(b == 0)
    def _prime():
        for g in range(slots):
            issue(g, g)

    for h in range(n_heads):
        qs = q_ref[:, h * LANES:(h + 1) * LANES]
        ql_s[h * s_len:(h + 1) * s_len, :] = _dot(qs.astype(BF16), wukt_ref[h])
        qp_s[h * s_len:(h + 1) * s_len, :] = qs[:, :rope_dim]
    m_s[...] = jnp.full(m_s.shape, NEG_BIG, F32)
    l_s[...] = jnp.zeros(l_s.shape, F32)
    acc_s[...] = jnp.zeros(acc_s.shape, F32)
    ql = ql_s[...].astype(BF16)
    qp = qp_s[...].astype(BF16)

    def probs(s):
        m_b = jnp.max(s, axis=1, keepdims=True)
        p = jnp.exp2(s - m_b)
        return m_b, jnp.sum(p, axis=1, keepdims=True), p.astype(BF16)

    def merge(parts):
        m_prev = m_s[...]
        m_next = m_prev
        for m_b, _, _ in parts:
            m_next = jnp.maximum(m_next, m_b)
        w = jnp.exp2(m_prev - m_next)
        l = l_s[...] * w
        acc = acc_s[...] * w
        for m_b, l_b, o_b in parts:
            w = jnp.exp2(m_b - m_next)
            l = l + l_b * w
            acc = acc + o_b * w
        m_s[...] = m_next
        l_s[...] = l
        acc_s[...] = acc

    for j in range(groups):
        slot = j % slots
        wait(slot)
        lats, scores = [], []
        for kb in range(pages * page // key_block):
            keys = pl.ds(kb * key_block, key_block)
            lat = lat_buf[slot, keys, :].astype(BF16)
            pe_t = pe_buf[slot, :, keys].astype(BF16)
            lats.append(lat)
            scores.append(_dot_nt(ql, lat) + _dot(qp, pe_t))
        stats = [probs(s) for s in scores]
        merge([(m_b, l_b, _dot(p, lat)) for (m_b, l_b, p), lat in zip(stats, lats)])
        nxt = b * groups + j + slots
        if j + slots < groups:
            issue(nxt, slot)
        else:
            @pl.when(b + 1 < nb)
            def _next_batch(nxt=nxt, slot=slot):
                issue(nxt, slot)

    pad = LANES - s_len
    new_lat = jnp.concatenate([ckv_ref[...], jnp.zeros((pad, ckv_ref.shape[1]), F32)], axis=0).astype(BF16)
    new_pe = jnp.concatenate([kr_ref[...], jnp.zeros((pad, rope_dim), F32)], axis=0).astype(BF16)
    s = _dot_nt(ql, new_lat) + _dot_nt(qp, new_pe)
    row = lax.broadcasted_iota(jnp.int32, (rows, LANES), 0)
    col = lax.broadcasted_iota(jnp.int32, (rows, LANES), 1)
    m_b, l_b, p = probs(jnp.where(col <= row % s_len, s, NEG_BIG))
    merge([(m_b, l_b, _dot(p, new_lat))])

    o_lat = (acc_s[...] / l_s[...]).astype(BF16)
    out = _dot(o_lat[0:s_len], wuvp_ref[0])
    for h in range(1, n_heads):
        out = out + _dot(o_lat[h * s_len:(h + 1) * s_len], wuvp_ref[h])
    o_ref[...] = out


def _dec_attn_call(page_table, q, ckv, kr, wukt, wuvp, cache_lat, cache_pe_t, *, n_heads, dv, pages):
    db, s_len, nk = q.shape
    kv_rank = ckv.shape[-1]
    rope_dim = kr.shape[-1]
    n_pages = page_table.shape[1]
    page = cache_lat.shape[1]
    groups = n_pages // pages
    key_block = min(ROW_TILE, pages * page)
    slots = math.gcd(groups, 4)
    assert n_pages % pages == 0 and slots >= 2 and (pages * page) % key_block == 0
    rows = n_heads * s_len
    body = functools.partial(_dec_attn_body, n_heads=n_heads, s_len=s_len, rope_dim=rope_dim, pages=pages,
                             page=page, groups=groups, key_block=key_block)
    grid_spec = pltpu.PrefetchScalarGridSpec(
        num_scalar_prefetch=1,
        grid=(db,),
        in_specs=[pl.BlockSpec((None, s_len, nk), lambda b, pt: (b, 0, 0)),
                  pl.BlockSpec((None, s_len, kv_rank), lambda b, pt: (b, 0, 0)),
                  pl.BlockSpec((None, s_len, rope_dim), lambda b, pt: (b, 0, 0)),
                  _const_spec(wukt.shape), _const_spec(wuvp.shape),
                  pl.BlockSpec(memory_space=pl.ANY), pl.BlockSpec(memory_space=pl.ANY)],
        out_specs=pl.BlockSpec((None, s_len, n_heads * dv), lambda b, pt: (b, 0, 0)),
        scratch_shapes=[pltpu.VMEM((slots, pages * page, kv_rank), F32),
                        pltpu.VMEM((slots, rope_dim, pages * page), F32),
                        pltpu.SemaphoreType.DMA((2, slots)),
                        pltpu.VMEM((rows, kv_rank), F32),
                        pltpu.VMEM((rows, rope_dim), F32),
                        pltpu.VMEM((rows, 1), F32),
                        pltpu.VMEM((rows, 1), F32),
                        pltpu.VMEM((rows, kv_rank), F32)])
    return pl.pallas_call(
        body,
        grid_spec=grid_spec,
        out_shape=jax.ShapeDtypeStruct((db, s_len, n_heads * dv), F32),
        compiler_params=_params("arbitrary"),
        name="mla_sample_attn",
    )(page_table.reshape(-1), q, ckv, kr, wukt, wuvp, cache_lat, cache_pe_t)


def _s5_disc_body(are_ref, aim_ref, ldt_ref, bre_ref, bim_ref, cre_ref, cim_ref,
                  abr_ref, abi_ref, bbr_ref, bbi_ref, ccr_ref, cci_ref):
    a_re = are_ref[...]
    a_im = aim_ref[...]
    dt = jnp.exp(ldt_ref[...])
    mag = jnp.exp(dt * a_re)
    abr = mag * jnp.cos(dt * a_im)
    abi = mag * jnp.sin(dt * a_im)
    den = a_re * a_re + a_im * a_im
    nr, ni = abr - 1.0, abi
    fr = (nr * a_re + ni * a_im) / den
    fi = (ni * a_re - nr * a_im) / den
    abr_ref[...] = abr
    abi_ref[...] = abi
    b_re = bre_ref[...]
    b_im = bim_ref[...]
    bbr = fr[:, None, :] * b_re - fi[:, None, :] * b_im
    bbi = fr[:, None, :] * b_im + fi[:, None, :] * b_re
    g, c, n = b_re.shape
    gh = g // bbr_ref.shape[0]
    for ref, blocks in ((bbr_ref, bbr), (bbi_ref, bbi), (ccr_ref, cre_ref[...]), (cci_ref, -cim_ref[...])):
        ref[...] = jnp.zeros(ref.shape, ref.dtype)
        for gi in range(g):
            hf, k = divmod(gi, gh)
            ref[hf, k * c:(k + 1) * c, k * n:(k + 1) * n] = blocks[gi].astype(ref.dtype)


def _s5_disc_call(a_re, a_im, log_dt, b_re_t, b_im_t, c_re, c_im, *, halves):
    g, n = a_re.shape
    c = b_re_t.shape[1]
    gh = g // halves
    op = jax.ShapeDtypeStruct((halves, gh * c, gh * n), BF16)
    return pl.pallas_call(
        _s5_disc_body,
        out_shape=[jax.ShapeDtypeStruct((g, n), F32), jax.ShapeDtypeStruct((g, n), F32), op, op, op, op],
        name="s5_discretise",
    )(a_re, a_im, log_dt.reshape(g, 1), b_re_t, b_im_t, c_re, c_im)


def _s5_body(u_ref, h0r_ref, h0i_ref, ar_ref, ai_ref, bbr_ref, bbi_ref, ccr_ref, cci_ref, d_ref, wglu_ref,
             y_ref, hr_out, hi_out, bur, bui, hr_s, hi_s, *, nb, tc, lane_tiles, unroll):
    c = pl.program_id(0)
    d_ssm = u_ref.shape[-1]
    n_state = hr_s.shape[-1]
    halves = bbr_ref.shape[0]
    ch_half = d_ssm // halves
    st_half = n_state // halves

    @pl.when(c == 0)
    def _init():
        hr_s[...] = h0r_ref[...]
        hi_s[...] = h0i_ref[...]

    u = jnp.swapaxes(u_ref[...], 0, 1).reshape(tc * nb, d_ssm)
    ub = u.astype(BF16)
    tiles_half = st_half // LANES
    for hf in range(halves):
        ublk = ub[:, hf * ch_half:(hf + 1) * ch_half]
        br = _dot(ublk, bbr_ref[hf])
        bi = _dot(ublk, bbi_ref[hf])
        for k in range(tiles_half):
            bur[hf * tiles_half + k] = br[:, k * LANES:(k + 1) * LANES]
            bui[hf * tiles_half + k] = bi[:, k * LANES:(k + 1) * LANES]

    for k0 in range(0, n_state // LANES, lane_tiles):
        tiles = range(k0, k0 + lane_tiles)
        ar = [jnp.broadcast_to(ar_ref[:, k * LANES:(k + 1) * LANES], (nb, LANES)) for k in tiles]
        ai = [jnp.broadcast_to(ai_ref[:, k * LANES:(k + 1) * LANES], (nb, LANES)) for k in tiles]

        def body(t, carry, tiles=tiles, ar=ar, ai=ai):
            rows = pl.ds(pl.multiple_of(t * nb, nb), nb)
            out = []
            for n, k in enumerate(tiles):
                hr, hi = carry[2 * n], carry[2 * n + 1]
                nr = ar[n] * hr - ai[n] * hi + bur[k, rows, :]
                ni = ar[n] * hi + ai[n] * hr + bui[k, rows, :]
                bur[k, rows, :] = nr
                bui[k, rows, :] = ni
                out += [nr, ni]
            return tuple(out)

        init = []
        for k in tiles:
            init += [hr_s[:, k * LANES:(k + 1) * LANES], hi_s[:, k * LANES:(k + 1) * LANES]]
        fin = lax.fori_loop(0, tc, body, tuple(init), unroll=unroll)
        for n, k in enumerate(tiles):
            hr_s[:, k * LANES:(k + 1) * LANES] = fin[2 * n]
            hi_s[:, k * LANES:(k + 1) * LANES] = fin[2 * n + 1]

    hr_out[...] = hr_s[...]
    hi_out[...] = hi_s[...]

    ys = []
    for hf in range(halves):
        hr_hist = jnp.concatenate([bur[hf * tiles_half + k] for k in range(tiles_half)], axis=1).astype(BF16)
        hi_hist = jnp.concatenate([bui[hf * tiles_half + k] for k in range(tiles_half)], axis=1).astype(BF16)
        ys.append(_dot_nt(hr_hist, ccr_ref[hf]) + _dot_nt(hi_hist, cci_ref[hf]))
    y = jnp.concatenate(ys, axis=1) + d_ref[...] * u
    g = jax.nn.gelu(y)
    out = g * jax.nn.sigmoid(_dot(g.astype(BF16), wglu_ref[...]))
    y_ref[...] = jnp.swapaxes(out.reshape(tc, nb, d_ssm), 0, 1)


def _s5_call(u3, h0r, h0i, ar, ai, bbr, bbi, ccr, cci, d, wglu, *, tc):
    nb, l, d_ssm = u3.shape
    n_state = ar.shape[-1]
    lane_tiles = math.gcd(n_state // LANES, max(1, VREGS * SUBLANES // (16 * nb)))
    body = functools.partial(_s5_body, nb=nb, tc=tc, lane_tiles=lane_tiles, unroll=tc)
    return pl.pallas_call(
        body,
        grid=(l // tc,),
        in_specs=[pl.BlockSpec((nb, tc, d_ssm), lambda c: (0, c, 0)),
                  _const_spec(h0r.shape), _const_spec(h0i.shape),
                  _const_spec(ar.shape), _const_spec(ai.shape),
                  _const_spec(bbr.shape), _const_spec(bbi.shape),
                  _const_spec(ccr.shape), _const_spec(cci.shape),
                  _const_spec(d.shape), _const_spec(wglu.shape)],
        out_specs=[pl.BlockSpec((nb, tc, d_ssm), lambda c: (0, c, 0)),
                   pl.BlockSpec((nb, n_state), lambda c: (0, 0)),
                   pl.BlockSpec((nb, n_state), lambda c: (0, 0))],
        out_shape=[jax.ShapeDtypeStruct((nb, l, d_ssm), F32),
                   jax.ShapeDtypeStruct((nb, n_state), F32),
                   jax.ShapeDtypeStruct((nb, n_state), F32)],
        scratch_shapes=[pltpu.VMEM((n_state // LANES, nb * tc, LANES), F32),
                        pltpu.VMEM((n_state // LANES, nb * tc, LANES), F32),
                        pltpu.VMEM((nb, n_state), F32), pltpu.VMEM((nb, n_state), F32)],
        compiler_params=_params("arbitrary"),
        name="s5_scan_glu",
    )(u3, h0r, h0i, ar, ai, bbr, bbi, ccr, cci, d, wglu)


def _memkv_body(m_ref, g_ref, wk_ref, wv_ref, k_ref, v_ref):
    m = _rms(m_ref[...], g_ref[...]).astype(BF16)
    k_ref[...] = _dot(m, wk_ref[...]).reshape(k_ref.shape)
    v_ref[...] = _dot(m, wv_ref[...]).reshape(v_ref.shape)


def _memkv_call(mem2, g, wk, wv, *, tm, mem_heads):
    t, d = mem2.shape
    hd = wk.shape[1] // mem_heads
    out = pl.BlockSpec((tm * mem_heads, hd), lambda i: (i, 0))
    return pl.pallas_call(
        _memkv_body,
        grid=(t // tm,),
        in_specs=[pl.BlockSpec((tm, d), lambda i: (i, 0)),
                  _const_spec(g.shape), _const_spec(wk.shape), _const_spec(wv.shape)],
        out_specs=[out, out],
        out_shape=[jax.ShapeDtypeStruct((t * mem_heads, hd), F32)] * 2,
        compiler_params=_params("parallel"),
        name="mem_kv",
    )(mem2, g, wk, wv)


def _mix_mem_body(x_ref, a_ref, s_ref, mk_ref, mv_ref, gao_ref, gso_ref, woa_ref, wos_ref, gmp_ref,
                  gmem_ref, wq_ref, wo_ref, gmo_ref, o_ref, *, mem_heads, n_mem, seqs, sub, mem_scale):
    def head_rows(ref, si, hh):
        return ref[pl.ds((si * n_mem) * mem_heads + hh, n_mem, stride=mem_heads), :].astype(BF16)

    mem = [[(head_rows(mk_ref, si, hh), head_rows(mv_ref, si, hh)) for hh in range(mem_heads)]
           for si in range(seqs)]
    tile = x_ref.shape[0] // sub
    rows = tile // seqs
    hd = wq_ref.shape[1] // mem_heads
    tiles = [pl.ds(t * tile, tile) for t in range(sub)]
    a = [_rms(a_ref[r, :], gao_ref[...]).astype(BF16) for r in tiles]
    s = [_rms(s_ref[r, :], gso_ref[...]).astype(BF16) for r in tiles]
    mix = [_dot(a[t], woa_ref[...]) + _dot(s[t], wos_ref[...]) for t in range(sub)]
    x = [x_ref[tiles[t], :] + _rms(mix[t], gmp_ref[...]) for t in range(sub)]
    h = [_rms(x[t], gmem_ref[...]).astype(BF16) for t in range(sub)]
    q = [_dot(h[t], wq_ref[...]) for t in range(sub)]
    problems = [(t, si, hh) for t in range(sub) for si in range(seqs) for hh in range(mem_heads)]
    scores = {}
    for t, si, hh in problems:
        qh = q[t][si * rows:(si + 1) * rows, hh * hd:(hh + 1) * hd].astype(BF16)
        scores[t, si, hh] = _dot_nt(qh, mem[si][hh][0]) * mem_scale
    probs = {}
    for key in problems:
        e = jnp.exp(scores[key] - jnp.max(scores[key], axis=1, keepdims=True))
        probs[key] = (e / jnp.sum(e, axis=1, keepdims=True)).astype(BF16)
    outs = {key: _dot(probs[key], mem[key[1]][key[2]][1]) for key in problems}
    for t in range(sub):
        per_seq = [jnp.concatenate([outs[t, si, hh] for hh in range(mem_heads)], axis=1) for si in range(seqs)]
        o = (jnp.concatenate(per_seq, axis=0) if seqs > 1 else per_seq[0]).astype(BF16)
        o_ref[tiles[t], :] = x[t] + _rms(_dot(o, wo_ref[...]), gmo_ref[...])


def _mix_mem_call(x2, a2, s2, mk, mv, gao, gso, woa, wos, gmp, gmem, wq, wo, gmo, *, tm, rows_per_batch,
                  mem_heads, n_mem):
    t, d = x2.shape
    da, ds = a2.shape[1], s2.shape[1]
    hd = mk.shape[1]
    seqs = max(1, tm // rows_per_batch)
    tiles_per_batch = max(1, rows_per_batch // tm)
    sub = tm // MXU_DIM if (seqs == 1 and tm % ROW_TILE == 0) else 1
    body = functools.partial(_mix_mem_body, mem_heads=mem_heads, n_mem=n_mem, seqs=seqs, sub=sub,
                             mem_scale=hd ** -0.5)
    row = lambda w: pl.BlockSpec((tm, w), lambda i: (i, 0))
    mem = pl.BlockSpec((seqs * n_mem * mem_heads, hd), lambda i: (i // tiles_per_batch, 0))
    consts = [gao, gso, woa, wos, gmp, gmem, wq, wo, gmo]
    return pl.pallas_call(
        body,
        grid=(t // tm,),
        in_specs=[row(d), row(da), row(ds), mem, mem] + [_const_spec(c.shape) for c in consts],
        out_specs=row(d),
        out_shape=jax.ShapeDtypeStruct((t, d), F32),
        compiler_params=_params("parallel"),
        name="mix_out_mem_attn",
    )(x2, a2, s2, mk, mv, *consts)


def _ffn_body(x_ref, cprev_ref, gpre_ref, wg_ref, wu_ref, wd_ref, cw_ref, cb_ref, gpost_ref,
              o_ref, cnew_ref, halo, work, act, *, nseq, lc, fc, halo_rows):
    t = pl.program_id(1)
    d_ff = wg_ref.shape[1]
    tm = nseq * lc
    keep = cprev_ref.shape[1]
    lo = halo_rows - keep

    @pl.when(t == 0)
    def _load_state():
        halo[...] = cprev_ref[...]

    x = x_ref[...]
    h = _rms(x, gpre_ref[...]).astype(BF16)

    for c in range(d_ff // fc):
        cols = slice(c * fc, (c + 1) * fc)
        g = _dot(h, wg_ref[:, cols]).reshape(nseq, lc, fc)
        up = _dot(h, wu_ref[:, cols]).reshape(nseq, lc, fc)
        work[c, :, lo:halo_rows, :] = halo[:, :, cols]
        work[c, :, halo_rows:halo_rows + lc, :] = g
        w = cw_ref[:, cols]
        conv = w[0:1, :] * work[c, :, lo:lo + lc, :]
        for k in range(1, keep):
            conv = conv + w[k:k + 1, :] * work[c, :, lo + k:lo + k + lc, :]
        conv = conv + w[keep:keep + 1, :] * g
        gc = cb_ref[:, cols] + conv
        act[:, cols] = (jax.nn.silu(gc) * up).reshape(tm, fc).astype(BF16)
        tail = work[c, :, lc + lo:lc + halo_rows, :]
        halo[:, :, cols] = tail
        cnew_ref[:, :, cols] = tail

    o_ref[...] = x + _rms(_dot(act[...], wd_ref[...]), gpost_ref[...])


def _ffn_call(x2, cprev, gpre, wg, wu, wd, cw, cb, gpost, *, nseq, lc, fc, n_batch_blocks, tiles_per_batch):
    t, d = x2.shape
    d_ff = wg.shape[1]
    n_chunks = d_ff // fc
    keep = cprev.shape[1]
    halo_rows = SUBLANES
    tm = nseq * lc
    body = functools.partial(_ffn_body, nseq=nseq, lc=lc, fc=fc, halo_rows=halo_rows)
    state = pl.BlockSpec((nseq, keep, d_ff), lambda b, i: (b, 0, 0))
    return pl.pallas_call(
        body,
        grid=(n_batch_blocks, tiles_per_batch),
        in_specs=[pl.BlockSpec((tm, d), lambda b, i: (b * tiles_per_batch + i, 0)),
                  state,
                  _const_spec(gpre.shape), _const_spec(wg.shape), _const_spec(wu.shape),
                  _const_spec(wd.shape), _const_spec(cw.shape), _const_spec(cb.shape),
                  _const_spec(gpost.shape)],
        out_specs=[pl.BlockSpec((tm, d), lambda b, i: (b * tiles_per_batch + i, 0)), state],
        out_shape=[jax.ShapeDtypeStruct((t, d), F32),
                   jax.ShapeDtypeStruct(cprev.shape, F32)],
        scratch_shapes=[pltpu.VMEM((nseq, keep, d_ff), F32),
                        pltpu.VMEM((n_chunks, nseq, halo_rows + lc, fc), F32),
                        pltpu.VMEM((tm, d_ff), BF16)],
        compiler_params=_params("parallel", "arbitrary"),
        name="conv_ffn",
    )(x2, cprev, gpre, wg, wu, wd, cw, cb, gpost)


def _rope_tables(pos, rope_dim, nope_dim, q_scale):
    half = rope_dim // 2
    inv = ROPE_THETA ** (-np.arange(half, dtype=np.float64) * (2.0 / rope_dim))
    ang = np.asarray(pos, np.float64)[:, None] * inv[None, :]
    cos, sin = np.cos(ang), np.sin(ang)
    n = ang.shape[0]
    pad = np.zeros((n, LANES - rope_dim))
    ck = np.concatenate([cos, cos, pad], axis=1)
    sk = np.concatenate([sin, sin, pad], axis=1)
    ones = np.concatenate([np.ones((n, nope_dim)), np.zeros((n, LANES - rope_dim - nope_dim))], axis=1)
    cq = np.concatenate([cos, cos, ones], axis=1) * q_scale
    return np.stack([ck, sk, cq, sk * q_scale]).astype(np.float32)


def _rot_half_cols(w):
    half = w.shape[-1] // 2
    return jnp.concatenate([-w[..., half:], w[..., :half]], axis=-1)


def _pad_last(w, width):
    return jnp.pad(w, [(0, 0)] * (w.ndim - 1) + [(0, width - w.shape[-1])])


def _layer_weights(l, w_in, q_norm, kv_norm, w_uq, w_uk, w_uv, ssm_a_re, ssm_a_im, ssm_log_dt, ssm_b_re,
                   ssm_b_im, ssm_c_re, ssm_c_im, ssm_d, ssm_w_glu, w_out, w_gate, w_up, w_down, ffn_conv_w,
                   ffn_conv_b):
    q_rank = q_norm.shape[-1]
    kv_rank = kv_norm.shape[-1]
    n_heads = w_uq.shape[2]
    nope = w_uk.shape[3]
    rope_dim = w_uq.shape[3] - nope
    dv = w_uv.shape[3]
    d_ssm = ssm_d.shape[-1]
    win = w_in[l]
    o2, o3 = q_rank + kv_rank, q_rank + kv_rank + rope_dim
    w_kr = win[:, o2:o3]
    p = {}
    p["win"] = jnp.concatenate([win[:, :o2], win[:, o3:], _pad_last(w_kr, LANES),
                                _pad_last(_rot_half_cols(w_kr), LANES)], axis=1).astype(BF16)
    uq = w_uq[l]
    q_nope, q_pe = uq[..., :nope], uq[..., nope:]
    wq1 = _pad_last(jnp.concatenate([q_pe, q_nope], axis=-1), LANES).reshape(q_rank, n_heads * LANES)
    wq2 = _pad_last(_rot_half_cols(q_pe), LANES).reshape(q_rank, n_heads * LANES)
    p["wq"] = jnp.concatenate([wq1, wq2], axis=1).astype(BF16)
    uk = w_uk[l]
    wuk_slots = jnp.pad(uk, ((0, 0), (0, 0), (rope_dim, LANES - rope_dim - nope)))
    p["wkv"] = jnp.concatenate([wuk_slots.reshape(kv_rank, n_heads * LANES),
                                _pad_last(w_uv[l], LANES).reshape(kv_rank, n_heads * LANES)], axis=1).astype(BF16)
    p["vone"] = jnp.tile((jnp.arange(LANES) == dv).astype(F32), n_heads).reshape(1, n_heads * LANES)
    p["wukt"] = jnp.transpose(wuk_slots, (1, 2, 0)).astype(BF16)
    uv = jnp.transpose(w_uv[l], (1, 0, 2))
    eye = jnp.eye(n_heads, dtype=F32)
    p["wuvp"] = (uv[:, :, None, :] * eye[:, None, :, None]).reshape(n_heads, kv_rank, n_heads * dv).astype(BF16)
    g, n = ssm_a_re.shape[1:]
    abr, abi, p["bbr"], p["bbi"], p["ccr"], p["cci"] = _s5_disc_call(
        ssm_a_re[l], ssm_a_im[l], ssm_log_dt[l], jnp.transpose(ssm_b_re[l], (0, 2, 1)),
        jnp.transpose(ssm_b_im[l], (0, 2, 1)), ssm_c_re[l], ssm_c_im[l], halves=2)
    p["abr"] = abr.reshape(1, g * n)
    p["abi"] = abi.reshape(1, g * n)
    p["ssm_d"] = ssm_d[l].reshape(1, d_ssm)
    p["wglu"] = ssm_w_glu[l].astype(BF16)
    d_attn = n_heads * dv
    p["woa"] = w_out[l][:d_attn].astype(BF16)
    p["wos"] = w_out[l][d_attn:].astype(BF16)
    d_ff = w_gate.shape[2]
    p["wg"] = w_gate[l].astype(BF16)
    p["wu"] = w_up[l].astype(BF16)
    p["wd"] = w_down[l].astype(BF16)
    conv_w = ffn_conv_w.shape[1]
    p["cw"] = jnp.pad(ffn_conv_w[l], ((0, -conv_w % SUBLANES), (0, 0)))
    p["cb"] = ffn_conv_b[l].reshape(1, d_ff)
    p["dims"] = dict(q_rank=q_rank, kv_rank=kv_rank, n_heads=n_heads, nope=nope, rope_dim=rope_dim, dv=dv,
                     d_ssm=d_ssm, g=g, n=n, d_ff=d_ff, conv_w=conv_w)
    return p


def _row(v):
    return v.reshape(1, -1)


def kernel(x_prompt, x_sample, mem_prompt, cache_kv_latent, cache_k_rope, page_table, state_ssm_re, state_ssm_im, state_ffn_conv, cache_mem_k, cache_mem_v, norm_mix_pre, w_in, q_norm, kv_norm, w_uq, w_uk, w_uv, ssm_a_re, ssm_a_im, ssm_log_dt, ssm_b_re, ssm_b_im, ssm_c_re, ssm_c_im, ssm_d, ssm_w_glu, norm_attn_out, norm_ssm_out, w_out, norm_mix_post, norm_mem_pre, mem_norm, w_q_mem, w_k_mem, w_v_mem, w_o_mem, norm_mem_post, norm_ffn_pre, w_gate, w_up, ffn_conv_w, ffn_conv_b, w_down, norm_ffn_post):
    depth = w_in.shape[0]
    b, l, d_model = x_prompt.shape
    db, ls, _ = x_sample.shape
    n_mem = mem_prompt.shape[1]
    mem_heads = cache_mem_k.shape[3]
    past_len = page_table.shape[1] * cache_kv_latent.shape[2]
    fc = MXU_DIM
    tm = min(ROW_TILE, l)
    tm_mix = min(2 * ROW_TILE, l)
    tq = min(ROW_TILE, l)
    tc = min(LANES, l)
    pages = min(2 * ROW_TILE * SUBLANES // cache_kv_latent.shape[2], page_table.shape[1] // 2)

    xp = x_prompt.reshape(b * l, d_model)
    xs = x_sample.reshape(db * ls, d_model)
    outs = {k: [] for k in ("p_kv", "p_kr", "p_sr", "p_si", "p_cv", "p_mk", "p_mv",
                            "s_kv", "s_kr", "s_sr", "s_si", "s_cv")}
    for li in range(depth):
        p = _layer_weights(li, w_in, q_norm, kv_norm, w_uq, w_uk, w_uv, ssm_a_re, ssm_a_im, ssm_log_dt,
                           ssm_b_re, ssm_b_im, ssm_c_re, ssm_c_im, ssm_d, ssm_w_glu, w_out, w_gate, w_up,
                           w_down, ffn_conv_w, ffn_conv_b)
        dm = p["dims"]
        n_heads, dv, rope_dim, nope = dm["n_heads"], dm["dv"], dm["rope_dim"], dm["nope"]
        g, n, d_ssm = dm["g"], dm["n"], dm["d_ssm"]
        q_scale = (nope + rope_dim) ** -0.5 * LOG2E
        pre_kw = dict(n_heads=n_heads, q_rank=dm["q_rank"], kv_rank=dm["kv_rank"], d_ssm=d_ssm,
                      rope_dim=rope_dim)
        gpre, gq, gkv = _row(norm_mix_pre[li]), _row(q_norm[li]), _row(kv_norm[li])
        mix_consts = (_row(norm_attn_out[li]), _row(norm_ssm_out[li]), p["woa"], p["wos"],
                      _row(norm_mix_post[li]), _row(norm_mem_pre[li]), w_q_mem[li].astype(BF16),
                      w_o_mem[li].astype(BF16), _row(norm_mem_post[li]))
        ffn_consts = (_row(norm_ffn_pre[li]), p["wg"], p["wu"], p["wd"], p["cw"], p["cb"],
                      _row(norm_ffn_post[li]))

        mk, mv = _memkv_call(mem_prompt.reshape(b * n_mem, d_model), _row(mem_norm[li]),
                             w_k_mem[li].astype(BF16), w_v_mem[li].astype(BF16), tm=min(ROW_TILE, b * n_mem),
                             mem_heads=mem_heads)
        tab_p = _rope_tables(np.arange(l), rope_dim, nope, q_scale)
        q, k, v, ckv, kr, u = _pre_call(xp, tab_p, gpre, gq, gkv, p["win"], p["wq"], p["wkv"], p["vone"],
                                        tm=tm, q_dtype=BF16, **pre_kw)
        attn = _attn_call(q.reshape(b, l, -1), k.reshape(b, l, -1), v.reshape(b, l, -1),
                          n_heads=n_heads, dv=dv, tq=tq)
        zeros_state = np.zeros((b, g * n), np.float32)
        ssm, hr, hi = _s5_call(u.reshape(b, l, d_ssm), zeros_state, zeros_state, p["abr"], p["abi"],
                               p["bbr"], p["bbi"], p["ccr"], p["cci"], p["ssm_d"], p["wglu"], tc=tc)
        xp = _mix_mem_call(xp, attn.reshape(b * l, -1), ssm.reshape(b * l, -1),
                           mk, mv, *mix_consts, tm=tm_mix, rows_per_batch=l, mem_heads=mem_heads,
                           n_mem=n_mem)
        conv0 = np.zeros((b, dm["conv_w"] - 1, dm["d_ff"]), np.float32)
        xp, cv = _ffn_call(xp, conv0, *ffn_consts, nseq=1, lc=tm, fc=fc, n_batch_blocks=b,
                           tiles_per_batch=l // tm)
        outs["p_kv"].append(ckv.reshape(b, l, -1))
        outs["p_kr"].append(kr.reshape(b, l, -1))
        outs["p_sr"].append(hr.reshape(b, g, n))
        outs["p_si"].append(hi.reshape(b, g, n))
        outs["p_cv"].append(cv)
        outs["p_mk"].append(mk.reshape(b, n_mem, mem_heads, -1))
        outs["p_mv"].append(mv.reshape(b, n_mem, mem_heads, -1))

        ts = db * ls
        tab_s = np.tile(_rope_tables(past_len + np.arange(ls), rope_dim, nope, q_scale), (1, db, 1))
        q, _, _, ckv, kr, u = _pre_call(xs, tab_s, gpre, gq, gkv, p["win"], p["wq"], p["wkv"], p["vone"],
                                        tm=ts, q_dtype=F32, **pre_kw)
        attn = _dec_attn_call(page_table, q.reshape(db, ls, -1), ckv.reshape(db, ls, -1),
                              kr.reshape(db, ls, -1), p["wukt"], p["wuvp"], cache_kv_latent[li],
                              jnp.swapaxes(cache_k_rope[li], 1, 2), n_heads=n_heads, dv=dv, pages=pages)
        ssm, hr, hi = _s5_call(u.reshape(db, ls, d_ssm), state_ssm_re[li].reshape(db, g * n),
                               state_ssm_im[li].reshape(db, g * n), p["abr"], p["abi"], p["bbr"], p["bbi"],
                               p["ccr"], p["cci"], p["ssm_d"], p["wglu"], tc=ls)
        xs = _mix_mem_call(xs, attn.reshape(ts, -1), ssm.reshape(ts, -1),
                           cache_mem_k[li].reshape(db * n_mem * mem_heads, -1),
                           cache_mem_v[li].reshape(db * n_mem * mem_heads, -1),
                           *mix_consts, tm=min(SUBLANES, db) * ls, rows_per_batch=ls, mem_heads=mem_heads,
                           n_mem=n_mem)
        xs, cv = _ffn_call(xs, state_ffn_conv[li], *ffn_consts, nseq=db, lc=ls, fc=fc, n_batch_blocks=1,
                           tiles_per_batch=1)
        outs["s_kv"].append(ckv.reshape(db, ls, -1))
        outs["s_kr"].append(kr.reshape(db, ls, -1))
        outs["s_sr"].append(hr.reshape(db, g, n))
        outs["s_si"].append(hi.reshape(db, g, n))
        outs["s_cv"].append(cv)

    st = lambda key: jnp.stack(outs[key])
    return (xp.reshape(b, l, d_model), xs.reshape(db, ls, d_model),
            st("p_kv"), st("p_kr"), st("p_sr"), st("p_si"), st("p_cv"), st("p_mk"), st("p_mv"),
            st("s_kv"), st("s_kr"), st("s_sr"), st("s_si"), st("s_cv"))
```

```python
import functools
import math

import jax
import jax.numpy as jnp
import numpy as np
from jax import lax
from jax.experimental import pallas as pl
from jax.experimental.pallas import tpu as pltpu

F32 = jnp.float32
BF16 = jnp.bfloat16

EPS = 1e-6
ROPE_THETA = 10000.0
LANES = 128
SUBLANES = 8
MXU_DIM = 256
VREGS = 64
ROW_TILE = 2 * MXU_DIM
NEG_BIG = -1e30
LOG2E = 1.4426950408889634
VMEM_LIMIT = 56 * 1024 * 1024


def _rms(x, g):
    y = x * lax.rsqrt(jnp.mean(x * x, axis=-1, keepdims=True) + EPS)
    return y * g


def _dot(a, b):
    return jnp.dot(a, b, preferred_element_type=F32)


def _dot_nt(a, b):
    return lax.dot_general(a, b, (((1,), (1,)), ((), ())), preferred_element_type=F32)


def _rep_lanes(x, n):
    return jnp.concatenate([x] * n, axis=1) if n > 1 else x


def _const_spec(shape):
    nd = len(shape)
    return pl.BlockSpec(shape, lambda *_: (0,) * nd, pipeline_mode=pl.Buffered(1))


def _params(*sem):
    return pltpu.CompilerParams(dimension_semantics=sem, vmem_limit_bytes=VMEM_LIMIT)


def _pre_body(x_ref, tab_ref, gpre_ref, gq_ref, gkv_ref, win_ref, wq_ref, wkv_ref, vone_ref,
              q_ref, k_ref, v_ref, ckv_ref, kr_ref, u_ref, *, n_heads, q_rank, kv_rank, d_ssm, rope_dim):
    x = x_ref[...]
    h = _rms(x, gpre_ref[...]).astype(BF16)
    z = _dot(h, win_ref[...])
    o1 = q_rank
    o2 = o1 + kv_rank
    o3 = o2 + d_ssm
    o4 = o3 + LANES
    cq, ckv, u = z[:, :o1], z[:, o1:o2], z[:, o2:o3]
    k1, k2 = z[:, o3:o4], z[:, o4:o4 + LANES]
    krs = k1 * tab_ref[0] + k2 * tab_ref[1]
    kr_ref[...] = krs[:, :rope_dim]
    ckv_n = _rms(ckv, gkv_ref[...])
    ckv_ref[...] = ckv_n
    kv2 = _dot(ckv_n.astype(BF16), wkv_ref[...])
    nk = n_heads * LANES
    k_ref[...] = (kv2[:, :nk] + _rep_lanes(krs, n_heads)).astype(k_ref.dtype)
    v_ref[...] = (kv2[:, nk:] + vone_ref[...]).astype(v_ref.dtype)
    qn = _rms(cq, gq_ref[...]).astype(BF16)
    qq = _dot(qn, wq_ref[...])
    q = qq[:, :nk] * _rep_lanes(tab_ref[2], n_heads) + qq[:, nk:] * _rep_lanes(tab_ref[3], n_heads)
    q_ref[...] = q.astype(q_ref.dtype)
    u_ref[...] = u


def _pre_call(x2, tab, gpre, gq, gkv, win, wq, wkv, vone, *, tm, n_heads, q_rank, kv_rank, d_ssm, rope_dim,
              q_dtype):
    t, d = x2.shape
    ntab = tab.shape[1] // tm
    nk = n_heads * LANES
    row = lambda w: pl.BlockSpec((tm, w), lambda i: (i, 0))
    body = functools.partial(_pre_body, n_heads=n_heads, q_rank=q_rank, kv_rank=kv_rank, d_ssm=d_ssm,
                             rope_dim=rope_dim)
    return pl.pallas_call(
        body,
        grid=(t // tm,),
        in_specs=[row(d),
                  pl.BlockSpec((4, tm, LANES), lambda i: (0, i % ntab, 0)),
                  _const_spec(gpre.shape), _const_spec(gq.shape), _const_spec(gkv.shape),
                  _const_spec(win.shape), _const_spec(wq.shape), _const_spec(wkv.shape),
                  _const_spec(vone.shape)],
        out_specs=[row(nk), row(nk), row(nk), row(kv_rank), row(rope_dim), row(d_ssm)],
        out_shape=[jax.ShapeDtypeStruct((t, nk), q_dtype),
                   jax.ShapeDtypeStruct((t, nk), BF16),
                   jax.ShapeDtypeStruct((t, nk), BF16),
                   jax.ShapeDtypeStruct((t, kv_rank), F32),
                   jax.ShapeDtypeStruct((t, rope_dim), F32),
                   jax.ShapeDtypeStruct((t, d_ssm), F32)],
        compiler_params=_params("parallel"),
        name="pre_proj",
    )(x2, tab, gpre, gq, gkv, win, wq, wkv, vone)


def _attn_body(qi_ref, kj_ref, q_ref, k_ref, v_ref, o_ref, m_ref, acc_ref, *, n_heads, tq, tk, dv):
    pair = pl.program_id(1)
    i = qi_ref[pair]
    j = kj_ref[pair]

    @pl.when(j == 0)
    def _init():
        m_ref[...] = jnp.full(m_ref.shape, NEG_BIG, F32)
        acc_ref[...] = jnp.zeros(acc_ref.shape, F32)

    items = [(e, h) for e in range(q_ref.shape[0]) for h in range(n_heads)]

    def scores(n):
        e, h = items[n]
        return _dot_nt(q_ref[e, :, h * LANES:(h + 1) * LANES], k_ref[e, :, h * LANES:(h + 1) * LANES])

    def step(masked):
        ahead = not masked
        s_next = scores(0) if ahead else None
        for n, (e, h) in enumerate(items):
            s = s_next if ahead else scores(n)
            if ahead and n + 1 < len(items):
                s_next = scores(n + 1)
            if masked:
                row = lax.broadcasted_iota(jnp.int32, (tq, tk), 0)
                col = lax.broadcasted_iota(jnp.int32, (tq, tk), 1)
                s = jnp.where(col <= row, s, NEG_BIG)
            m_prev = m_ref[n]
            m_next = jnp.maximum(m_prev, jnp.max(s, axis=1, keepdims=True))
            alpha = jnp.exp2(m_prev - m_next)
            p = jnp.exp2(s - _rep_lanes(m_next, tk // LANES))
            m_ref[n] = m_next
            acc_ref[n] = acc_ref[n] * alpha + _dot(p.astype(BF16), v_ref[e, :, h * LANES:(h + 1) * LANES])

    @pl.when(j < i)
    def _off_diag():
        step(False)

    @pl.when(j == i)
    def _diag():
        step(True)
        for n, (e, h) in enumerate(items):
            acc = acc_ref[n]
            o_ref[e, :, h * dv:(h + 1) * dv] = acc[:, :dv] / acc[:, dv:dv + 1]


def _attn_call(q, k, v, *, n_heads, dv, tq):
    b, l, nk = q.shape
    tk = tq
    nq = l // tq
    qi = jnp.asarray([i for i in range(nq) for _ in range(i + 1)], jnp.int32)
    kj = jnp.asarray([j for i in range(nq) for j in range(i + 1)], jnp.int32)
    body = functools.partial(_attn_body, n_heads=n_heads, tq=tq, tk=tk, dv=dv)
    eb = 2 if b % 2 == 0 else 1
    grid_spec = pltpu.PrefetchScalarGridSpec(
        num_scalar_prefetch=2,
        grid=(b // eb, qi.shape[0]),
        in_specs=[pl.BlockSpec((eb, tq, nk), lambda bb, p, qi, kj: (bb, qi[p], 0)),
                  pl.BlockSpec((eb, tk, nk), lambda bb, p, qi, kj: (bb, kj[p], 0)),
                  pl.BlockSpec((eb, tk, nk), lambda bb, p, qi, kj: (bb, kj[p], 0))],
        out_specs=pl.BlockSpec((eb, tq, n_heads * dv), lambda bb, p, qi, kj: (bb, qi[p], 0)),
        scratch_shapes=[pltpu.VMEM((eb * n_heads, tq, LANES), F32),
                        pltpu.VMEM((eb * n_heads, tq, LANES), F32)])
    return pl.pallas_call(
        body,
        grid_spec=grid_spec,
        out_shape=jax.ShapeDtypeStruct((b, l, n_heads * dv), F32),
        compiler_params=_params("parallel", "arbitrary"),
        name="mla_prompt_attn",
    )(qi, kj, q, k, v)


def _dec_attn_body(pt_ref, q_ref, ckv_ref, kr_ref, wukt_ref, wuvp_ref, lat_hbm, pe_hbm, o_ref,
                   lat_buf, pe_buf, sems, ql_s, qp_s, m_s, l_s, acc_s,
                   *, n_heads, s_len, rope_dim, pages, page, groups, key_block):
    slots = lat_buf.shape[0]
    eb = q_ref.shape[0]
    step = pl.program_id(0)
    steps = pl.num_programs(0)
    rows = n_heads * s_len

    def page_copies(group, slot, real):
        out = []
        for r in range(pages):
            pg = pt_ref[group * pages + r] if real else 0
            out.append(pltpu.make_async_copy(lat_hbm.at[pg], lat_buf.at[slot, pl.ds(r * page, page), :],
                                             sems.at[0, slot]))
            out.append(pltpu.make_async_copy(pe_hbm.at[pg], pe_buf.at[slot, :, pl.ds(r * page, page)],
                                             sems.at[1, slot]))
        return out

    def issue(group, slot):
        for c in page_copies(group, slot, True):
            c.start()

    def wait(slot):
        for c in page_copies(0, slot, False):
            c.wait()

    @pl.when(step == 0)
    def _prime():
        for g in range(slots):
            issue(g, g)

    def probs(s):
        m_b = jnp.max(s, axis=1, keepdims=True)
        p = jnp.exp2(s - m_b)
        return m_b, jnp.sum(p, axis=1, keepdims=True), p.astype(BF16)

    def merge(parts):
        m_prev = m_s[...]
        m_next = m_prev
        for m_b, _, _ in parts:
            m_next = jnp.maximum(m_next, m_b)
        w = jnp.exp2(m_prev - m_next)
        l = l_s[...] * w
        acc = acc_s[...] * w
        for m_b, l_b, o_b in parts:
            w = jnp.exp2(m_b - m_next)
            l = l + l_b * w
            acc = acc + o_b * w
        m_s[...] = m_next
        l_s[...] = l
        acc_s[...] = acc

    def element(k):
        for h in range(n_heads):
            qs = q_ref[k, :, h * LANES:(h + 1) * LANES]
            ql_s[h * s_len:(h + 1) * s_len, :] = _dot(qs.astype(BF16), wukt_ref[h])
            qp_s[h * s_len:(h + 1) * s_len, :] = qs[:, :rope_dim]
        m_s[...] = jnp.full(m_s.shape, NEG_BIG, F32)
        l_s[...] = jnp.zeros(l_s.shape, F32)
        acc_s[...] = jnp.zeros(acc_s.shape, F32)
        ql = ql_s[...].astype(BF16)
        qp = qp_s[...].astype(BF16)

        for j in range(groups):
            local = k * groups + j
            slot = local % slots
            wait(slot)
            lats, scores = [], []
            for kb in range(pages * page // key_block):
                keys = pl.ds(kb * key_block, key_block)
                lat = lat_buf[slot, keys, :].astype(BF16)
                pe_t = pe_buf[slot, :, keys].astype(BF16)
                lats.append(lat)
                scores.append(_dot_nt(ql, lat) + _dot(qp, pe_t))
            stats = [probs(s) for s in scores]
            merge([(m_b, l_b, _dot(p, lat)) for (m_b, l_b, p), lat in zip(stats, lats)])
            nxt = step * (eb * groups) + local + slots
            if local + slots < eb * groups:
                issue(nxt, slot)
            else:
                @pl.when(step + 1 < steps)
                def _next_step(nxt=nxt, slot=slot):
                    issue(nxt, slot)

        pad = LANES - s_len
        new_lat = jnp.concatenate([ckv_ref[k], jnp.zeros((pad, ckv_ref.shape[2]), F32)], axis=0).astype(BF16)
        new_pe = jnp.concatenate([kr_ref[k], jnp.zeros((pad, rope_dim), F32)], axis=0).astype(BF16)
        s = _dot_nt(ql, new_lat) + _dot_nt(qp, new_pe)
        row = lax.broadcasted_iota(jnp.int32, (rows, LANES), 0)
        col = lax.broadcasted_iota(jnp.int32, (rows, LANES), 1)
        m_b, l_b, p = probs(jnp.where(col <= row % s_len, s, NEG_BIG))
        merge([(m_b, l_b, _dot(p, new_lat))])

        o_lat = (acc_s[...] / l_s[...]).astype(BF16)
        out = _dot(o_lat[0:s_len], wuvp_ref[0])
        for h in range(1, n_heads):
            out = out + _dot(o_lat[h * s_len:(h + 1) * s_len], wuvp_ref[h])
        o_ref[k] = out

    for k in range(eb):
        element(k)


def _dec_attn_call(page_table, q, ckv, kr, wukt, wuvp, cache_lat, cache_pe_t, *, n_heads, dv, pages):
    db, s_len, nk = q.shape
    kv_rank = ckv.shape[-1]
    rope_dim = kr.shape[-1]
    n_pages = page_table.shape[1]
    page = cache_lat.shape[1]
    groups = n_pages // pages
    key_block = min(ROW_TILE, pages * page)
    slots = 2
    eb = 2 if db % 2 == 0 else 1
    assert n_pages % pages == 0 and (eb * groups) % slots == 0 and (pages * page) % key_block == 0
    rows = n_heads * s_len
    body = functools.partial(_dec_attn_body, n_heads=n_heads, s_len=s_len, rope_dim=rope_dim, pages=pages,
                             page=page, groups=groups, key_block=key_block)
    grid_spec = pltpu.PrefetchScalarGridSpec(
        num_scalar_prefetch=1,
        grid=(db // eb,),
        in_specs=[pl.BlockSpec((eb, s_len, nk), lambda b, pt: (b, 0, 0)),
                  pl.BlockSpec((eb, s_len, kv_rank), lambda b, pt: (b, 0, 0)),
                  pl.BlockSpec((eb, s_len, rope_dim), lambda b, pt: (b, 0, 0)),
                  _const_spec(wukt.shape), _const_spec(wuvp.shape),
                  pl.BlockSpec(memory_space=pl.ANY), pl.BlockSpec(memory_space=pl.ANY)],
        out_specs=pl.BlockSpec((eb, s_len, n_heads * dv), lambda b, pt: (b, 0, 0)),
        scratch_shapes=[pltpu.VMEM((slots, pages * page, kv_rank), F32),
                        pltpu.VMEM((slots, rope_dim, pages * page), F32),
                        pltpu.SemaphoreType.DMA((2, slots)),
                        pltpu.VMEM((rows, kv_rank), F32),
                        pltpu.VMEM((rows, rope_dim), F32),
                        pltpu.VMEM((rows, 1), F32),
                        pltpu.VMEM((rows, 1), F32),
                        pltpu.VMEM((rows, kv_rank), F32)])
    return pl.pallas_call(
        body,
        grid_spec=grid_spec,
        out_shape=jax.ShapeDtypeStruct((db, s_len, n_heads * dv), F32),
        compiler_params=_params("arbitrary"),
        name="mla_sample_attn",
    )(page_table.reshape(-1), q, ckv, kr, wukt, wuvp, cache_lat, cache_pe_t)


def _s5_disc_body(are_ref, aim_ref, ldt_ref, bre_ref, bim_ref, cre_ref, cim_ref,
                  abr_ref, abi_ref, bbr_ref, bbi_ref, ccr_ref, cci_ref):
    a_re = are_ref[...]
    a_im = aim_ref[...]
    dt = jnp.exp(ldt_ref[...])
    mag = jnp.exp(dt * a_re)
    abr = mag * jnp.cos(dt * a_im)
    abi = mag * jnp.sin(dt * a_im)
    den = a_re * a_re + a_im * a_im
    nr, ni = abr - 1.0, abi
    fr = (nr * a_re + ni * a_im) / den
    fi = (ni * a_re - nr * a_im) / den
    abr_ref[...] = abr
    abi_ref[...] = abi
    b_re = bre_ref[...]
    b_im = bim_ref[...]
    bbr = fr[:, None, :] * b_re - fi[:, None, :] * b_im
    bbi = fr[:, None, :] * b_im + fi[:, None, :] * b_re
    g, c, n = b_re.shape
    gh = g // bbr_ref.shape[0]
    for ref, blocks in ((bbr_ref, bbr), (bbi_ref, bbi), (ccr_ref, cre_ref[...]), (cci_ref, -cim_ref[...])):
        ref[...] = jnp.zeros(ref.shape, ref.dtype)
        for gi in range(g):
            hf, k = divmod(gi, gh)
            ref[hf, k * c:(k + 1) * c, k * n:(k + 1) * n] = blocks[gi].astype(ref.dtype)


def _s5_disc_call(a_re, a_im, log_dt, b_re_t, b_im_t, c_re, c_im, *, halves):
    g, n = a_re.shape
    c = b_re_t.shape[1]
    gh = g // halves
    op = jax.ShapeDtypeStruct((halves, gh * c, gh * n), BF16)
    return pl.pallas_call(
        _s5_disc_body,
        out_shape=[jax.ShapeDtypeStruct((g, n), F32), jax.ShapeDtypeStruct((g, n), F32), op, op, op, op],
        name="s5_discretise",
    )(a_re, a_im, log_dt.reshape(g, 1), b_re_t, b_im_t, c_re, c_im)


def _s5_body(u_ref, h0r_ref, h0i_ref, ar_ref, ai_ref, bbr_ref, bbi_ref, ccr_ref, cci_ref, d_ref, wglu_ref,
             y_ref, hr_out, hi_out, bur, bui, hr_s, hi_s, *, nb, tc, lane_tiles, unroll):
    c = pl.program_id(0)
    d_ssm = u_ref.shape[-1]
    n_state = hr_s.shape[-1]
    halves = bbr_ref.shape[0]
    ch_half = d_ssm // halves
    st_half = n_state // halves

    @pl.when(c == 0)
    def _init():
        hr_s[...] = h0r_ref[...]
        hi_s[...] = h0i_ref[...]

    u = jnp.swapaxes(u_ref[...], 0, 1).reshape(tc * nb, d_ssm)
    ub = u.astype(BF16)
    tiles_half = st_half // LANES
    for hf in range(halves):
        ublk = ub[:, hf * ch_half:(hf + 1) * ch_half]
        br = _dot(ublk, bbr_ref[hf])
        bi = _dot(ublk, bbi_ref[hf])
        for k in range(tiles_half):
            bur[hf * tiles_half + k] = br[:, k * LANES:(k + 1) * LANES]
            bui[hf * tiles_half + k] = bi[:, k * LANES:(k + 1) * LANES]

    for k0 in range(0, n_state // LANES, lane_tiles):
        tiles = range(k0, k0 + lane_tiles)
        ar = [jnp.broadcast_to(ar_ref[:, k * LANES:(k + 1) * LANES], (nb, LANES)) for k in tiles]
        ai = [jnp.broadcast_to(ai_ref[:, k * LANES:(k + 1) * LANES], (nb, LANES)) for k in tiles]

        def body(t, carry, tiles=tiles, ar=ar, ai=ai):
            rows = pl.ds(pl.multiple_of(t * nb, nb), nb)
            out = []
            for n, k in enumerate(tiles):
                hr, hi = carry[2 * n], carry[2 * n + 1]
                nr = ar[n] * hr - ai[n] * hi + bur[k, rows, :]
                ni = ar[n] * hi + ai[n] * hr + bui[k, rows, :]
                bur[k, rows, :] = nr
                bui[k, rows, :] = ni
                out += [nr, ni]
            return tuple(out)

        init = []
        for k in tiles:
            init += [hr_s[:, k * LANES:(k + 1) * LANES], hi_s[:, k * LANES:(k + 1) * LANES]]
        fin = lax.fori_loop(0, tc, body, tuple(init), unroll=unroll)
        for n, k in enumerate(tiles):
            hr_s[:, k * LANES:(k + 1) * LANES] = fin[2 * n]
            hi_s[:, k * LANES:(k + 1) * LANES] = fin[2 * n + 1]

    hr_out[...] = hr_s[...]
    hi_out[...] = hi_s[...]

    ys = []
    for hf in range(halves):
        hr_hist = jnp.concatenate([bur[hf * tiles_half + k] for k in range(tiles_half)], axis=1).astype(BF16)
        hi_hist = jnp.concatenate([bui[hf * tiles_half + k] for k in range(tiles_half)], axis=1).astype(BF16)
        ys.append(_dot_nt(hr_hist, ccr_ref[hf]) + _dot_nt(hi_hist, cci_ref[hf]))
    y = jnp.concatenate(ys, axis=1) + d_ref[...] * u
    g = jax.nn.gelu(y)
    out = g * jax.nn.sigmoid(_dot(g.astype(BF16), wglu_ref[...]))
    y_ref[...] = jnp.swapaxes(out.reshape(tc, nb, d_ssm), 0, 1)


def _s5_call(u3, h0r, h0i, ar, ai, bbr, bbi, ccr, cci, d, wglu, *, tc):
    nb, l, d_ssm = u3.shape
    n_state = ar.shape[-1]
    lane_tiles = math.gcd(n_state // LANES, max(1, VREGS * SUBLANES // (16 * nb)))
    body = functools.partial(_s5_body, nb=nb, tc=tc, lane_tiles=lane_tiles, unroll=tc)
    return pl.pallas_call(
        body,
        grid=(l // tc,),
        in_specs=[pl.BlockSpec((nb, tc, d_ssm), lambda c: (0, c, 0)),
                  _const_spec(h0r.shape), _const_spec(h0i.shape),
                  _const_spec(ar.shape), _const_spec(ai.shape),
                  _const_spec(bbr.shape), _const_spec(bbi.shape),
                  _const_spec(ccr.shape), _const_spec(cci.shape),
                  _const_spec(d.shape), _const_spec(wglu.shape)],
        out_specs=[pl.BlockSpec((nb, tc, d_ssm), lambda c: (0, c, 0)),
                   pl.BlockSpec((nb, n_state), lambda c: (0, 0)),
                   pl.BlockSpec((nb, n_state), lambda c: (0, 0))],
        out_shape=[jax.ShapeDtypeStruct((nb, l, d_ssm), F32),
                   jax.ShapeDtypeStruct((nb, n_state), F32),
                   jax.ShapeDtypeStruct((nb, n_state), F32)],
        scratch_shapes=[pltpu.VMEM((n_state // LANES, nb * tc, LANES), F32),
                        pltpu.VMEM((n_state // LANES, nb * tc, LANES), F32),
                        pltpu.VMEM((nb, n_state), F32), pltpu.VMEM((nb, n_state), F32)],
        compiler_params=_params("arbitrary"),
        name="s5_scan_glu",
    )(u3, h0r, h0i, ar, ai, bbr, bbi, ccr, cci, d, wglu)


def _memkv_body(m_ref, g_ref, wk_ref, wv_ref, k_ref, v_ref):
    m = _rms(m_ref[...], g_ref[...]).astype(BF16)
    k_ref[...] = _dot(m, wk_ref[...]).reshape(k_ref.shape)
    v_ref[...] = _dot(m, wv_ref[...]).reshape(v_ref.shape)


def _memkv_call(mem2, g, wk, wv, *, tm, mem_heads):
    t, d = mem2.shape
    hd = wk.shape[1] // mem_heads
    out = pl.BlockSpec((tm * mem_heads, hd), lambda i: (i, 0))
    return pl.pallas_call(
        _memkv_body,
        grid=(t // tm,),
        in_specs=[pl.BlockSpec((tm, d), lambda i: (i, 0)),
                  _const_spec(g.shape), _const_spec(wk.shape), _const_spec(wv.shape)],
        out_specs=[out, out],
        out_shape=[jax.ShapeDtypeStruct((t * mem_heads, hd), F32)] * 2,
        compiler_params=_params("parallel"),
        name="mem_kv",
    )(mem2, g, wk, wv)


def _mix_mem_body(x_ref, a_ref, s_ref, mk_ref, mv_ref, gao_ref, gso_ref, woa_ref, wos_ref, gmp_ref,
                  gmem_ref, wq_ref, wo_ref, gmo_ref, o_ref, *, mem_heads, n_mem, seqs, sub, mem_scale):
    def head_rows(ref, si, hh):
        return ref[pl.ds((si * n_mem) * mem_heads + hh, n_mem, stride=mem_heads), :].astype(BF16)

    mem = [[(head_rows(mk_ref, si, hh), head_rows(mv_ref, si, hh)) for hh in range(mem_heads)]
           for si in range(seqs)]
    tile = x_ref.shape[0] // sub
    rows = tile // seqs
    hd = wq_ref.shape[1] // mem_heads
    tiles = [pl.ds(t * tile, tile) for t in range(sub)]
    a = [_rms(a_ref[r, :], gao_ref[...]).astype(BF16) for r in tiles]
    s = [_rms(s_ref[r, :], gso_ref[...]).astype(BF16) for r in tiles]
    mix = [_dot(a[t], woa_ref[...]) + _dot(s[t], wos_ref[...]) for t in range(sub)]
    x = [x_ref[tiles[t], :] + _rms(mix[t], gmp_ref[...]) for t in range(sub)]
    h = [_rms(x[t], gmem_ref[...]).astype(BF16) for t in range(sub)]
    q = [_dot(h[t], wq_ref[...]) for t in range(sub)]
    problems = [(t, si, hh) for t in range(sub) for si in range(seqs) for hh in range(mem_heads)]
    scores = {}
    for t, si, hh in problems:
        qh = q[t][si * rows:(si + 1) * rows, hh * hd:(hh + 1) * hd].astype(BF16)
        scores[t, si, hh] = _dot_nt(qh, mem[si][hh][0]) * mem_scale
    probs = {}
    for key in problems:
        e = jnp.exp(scores[key] - jnp.max(scores[key], axis=1, keepdims=True))
        probs[key] = (e / jnp.sum(e, axis=1, keepdims=True)).astype(BF16)
    outs = {key: _dot(probs[key], mem[key[1]][key[2]][1]) for key in problems}
    for t in range(sub):
        per_seq = [jnp.concatenate([outs[t, si, hh] for hh in range(mem_heads)], axis=1) for si in range(seqs)]
        o = (jnp.concatenate(per_seq, axis=0) if seqs > 1 else per_seq[0]).astype(BF16)
        o_ref[tiles[t], :] = x[t] + _rms(_dot(o, wo_ref[...]), gmo_ref[...])


def _mix_mem_call(x2, a2, s2, mk, mv, gao, gso, woa, wos, gmp, gmem, wq, wo, gmo, *, tm, rows_per_batch,
                  mem_heads, n_mem):
    t, d = x2.shape
    da, ds = a2.shape[1], s2.shape[1]
    hd = mk.shape[1]
    seqs = max(1, tm // rows_per_batch)
    tiles_per_batch = max(1, rows_per_batch // tm)
    sub = tm // MXU_DIM if (seqs == 1 and tm % ROW_TILE == 0) else 1
    body = functools.partial(_mix_mem_body, mem_heads=mem_heads, n_mem=n_mem, seqs=seqs, sub=sub,
                             mem_scale=hd ** -0.5)
    row = lambda w: pl.BlockSpec((tm, w), lambda i: (i, 0))
    mem = pl.BlockSpec((seqs * n_mem * mem_heads, hd), lambda i: (i // tiles_per_batch, 0))
    consts = [gao, gso, woa, wos, gmp, gmem, wq, wo, gmo]
    return pl.pallas_call(
        body,
        grid=(t // tm,),
        in_specs=[row(d), row(da), row(ds), mem, mem] + [_const_spec(c.shape) for c in consts],
        out_specs=row(d),
        out_shape=jax.ShapeDtypeStruct((t, d), F32),
        compiler_params=_params("parallel"),
        name="mix_out_mem_attn",
    )(x2, a2, s2, mk, mv, *consts)


def _ffn_body(x_ref, cprev_ref, gpre_ref, wg_ref, wu_ref, wd_ref, cw_ref, cb_ref, gpost_ref,
              o_ref, cnew_ref, halo, work, act, *, nseq, lc, fc, halo_rows):
    t = pl.program_id(1)
    d_ff = wg_ref.shape[1]
    tm = nseq * lc
    keep = cprev_ref.shape[1]
    lo = halo_rows - keep

    @pl.when(t == 0)
    def _load_state():
        halo[...] = cprev_ref[...]

    x = x_ref[...]
    h = _rms(x, gpre_ref[...]).astype(BF16)

    for c in range(d_ff // fc):
        cols = slice(c * fc, (c + 1) * fc)
        g = _dot(h, wg_ref[:, cols]).reshape(nseq, lc, fc)
        up = _dot(h, wu_ref[:, cols]).reshape(nseq, lc, fc)
        work[c, :, lo:halo_rows, :] = halo[:, :, cols]
        work[c, :, halo_rows:halo_rows + lc, :] = g
        w = cw_ref[:, cols]
        conv = w[0:1, :] * work[c, :, lo:lo + lc, :]
        for k in range(1, keep):
            conv = conv + w[k:k + 1, :] * work[c, :, lo + k:lo + k + lc, :]
        conv = conv + w[keep:keep + 1, :] * g
        gc = cb_ref[:, cols] + conv
        act[:, cols] = (jax.nn.silu(gc) * up).reshape(tm, fc).astype(BF16)
        tail = work[c, :, lc + lo:lc + halo_rows, :]
        halo[:, :, cols] = tail
        cnew_ref[:, :, cols] = tail

    o_ref[...] = x + _rms(_dot(act[...], wd_ref[...]), gpost_ref[...])


def _ffn_call(x2, cprev, gpre, wg, wu, wd, cw, cb, gpost, *, nseq, lc, fc, n_batch_blocks, tiles_per_batch):
    t, d = x2.shape
    d_ff = wg.shape[1]
    n_chunks = d_ff // fc
    keep = cprev.shape[1]
    halo_rows = SUBLANES
    tm = nseq * lc
    body = functools.partial(_ffn_body, nseq=nseq, lc=lc, fc=fc, halo_rows=halo_rows)
    state = pl.BlockSpec((nseq, keep, d_ff), lambda b, i: (b, 0, 0))
    return pl.pallas_call(
        body,
        grid=(n_batch_blocks, tiles_per_batch),
        in_specs=[pl.BlockSpec((tm, d), lambda b, i: (b * tiles_per_batch + i, 0)),
                  state,
                  _const_spec(gpre.shape), _const_spec(wg.shape), _const_spec(wu.shape),
                  _const_spec(wd.shape), _const_spec(cw.shape), _const_spec(cb.shape),
                  _const_spec(gpost.shape)],
        out_specs=[pl.BlockSpec((tm, d), lambda b, i: (b * tiles_per_batch + i, 0)), state],
        out_shape=[jax.ShapeDtypeStruct((t, d), F32),
                   jax.ShapeDtypeStruct(cprev.shape, F32)],
        scratch_shapes=[pltpu.VMEM((nseq, keep, d_ff), F32),
                        pltpu.VMEM((n_chunks, nseq, halo_rows + lc, fc), F32),
                        pltpu.VMEM((tm, d_ff), BF16)],
        compiler_params=_params("parallel", "arbitrary"),
        name="conv_ffn",
    )(x2, cprev, gpre, wg, wu, wd, cw, cb, gpost)


def _rope_tables(pos, rope_dim, nope_dim, q_scale):
    half = rope_dim // 2
    inv = ROPE_THETA ** (-np.arange(half, dtype=np.float64) * (2.0 / rope_dim))
    ang = np.asarray(pos, np.float64)[:, None] * inv[None, :]
    cos, sin = np.cos(ang), np.sin(ang)
    n = ang.shape[0]
    pad = np.zeros((n, LANES - rope_dim))
    ck = np.concatenate([cos, cos, pad], axis=1)
    sk = np.concatenate([sin, sin, pad], axis=1)
    ones = np.concatenate([np.ones((n, nope_dim)), np.zeros((n, LANES - rope_dim - nope_dim))], axis=1)
    cq = np.concatenate([cos, cos, ones], axis=1) * q_scale
    return np.stack([ck, sk, cq, sk * q_scale]).astype(np.float32)


def _rot_half_cols(w):
    half = w.shape[-1] // 2
    return jnp.concatenate([-w[..., half:], w[..., :half]], axis=-1)


def _pad_last(w, width):
    return jnp.pad(w, [(0, 0)] * (w.ndim - 1) + [(0, width - w.shape[-1])])


def _layer_weights(l, w_in, q_norm, kv_norm, w_uq, w_uk, w_uv, ssm_a_re, ssm_a_im, ssm_log_dt, ssm_b_re,
                   ssm_b_im, ssm_c_re, ssm_c_im, ssm_d, ssm_w_glu, w_out, w_gate, w_up, w_down, ffn_conv_w,
                   ffn_conv_b):
    q_rank = q_norm.shape[-1]
    kv_rank = kv_norm.shape[-1]
    n_heads = w_uq.shape[2]
    nope = w_uk.shape[3]
    rope_dim = w_uq.shape[3] - nope
    dv = w_uv.shape[3]
    d_ssm = ssm_d.shape[-1]
    win = w_in[l]
    o2, o3 = q_rank + kv_rank, q_rank + kv_rank + rope_dim
    w_kr = win[:, o2:o3]
    p = {}
    p["win"] = jnp.concatenate([win[:, :o2], win[:, o3:], _pad_last(w_kr, LANES),
                                _pad_last(_rot_half_cols(w_kr), LANES)], axis=1).astype(BF16)
    uq = w_uq[l]
    q_nope, q_pe = uq[..., :nope], uq[..., nope:]
    wq1 = _pad_last(jnp.concatenate([q_pe, q_nope], axis=-1), LANES).reshape(q_rank, n_heads * LANES)
    wq2 = _pad_last(_rot_half_cols(q_pe), LANES).reshape(q_rank, n_heads * LANES)
    p["wq"] = jnp.concatenate([wq1, wq2], axis=1).astype(BF16)
    uk = w_uk[l]
    wuk_slots = jnp.pad(uk, ((0, 0), (0, 0), (rope_dim, LANES - rope_dim - nope)))
    p["wkv"] = jnp.concatenate([wuk_slots.reshape(kv_rank, n_heads * LANES),
                                _pad_last(w_uv[l], LANES).reshape(kv_rank, n_heads * LANES)], axis=1).astype(BF16)
    p["vone"] = jnp.tile((jnp.arange(LANES) == dv).astype(F32), n_heads).reshape(1, n_heads * LANES)
    p["wukt"] = jnp.transpose(wuk_slots, (1, 2, 0)).astype(BF16)
    uv = jnp.transpose(w_uv[l], (1, 0, 2))
    eye = jnp.eye(n_heads, dtype=F32)
    p["wuvp"] = (uv[:, :, None, :] * eye[:, None, :, None]).reshape(n_heads, kv_rank, n_heads * dv).astype(BF16)
    g, n = ssm_a_re.shape[1:]
    abr, abi, p["bbr"], p["bbi"], p["ccr"], p["cci"] = _s5_disc_call(
        ssm_a_re[l], ssm_a_im[l], ssm_log_dt[l], jnp.transpose(ssm_b_re[l], (0, 2, 1)),
        jnp.transpose(ssm_b_im[l], (0, 2, 1)), ssm_c_re[l], ssm_c_im[l], halves=2)
    p["abr"] = abr.reshape(1, g * n)
    p["abi"] = abi.reshape(1, g * n)
    p["ssm_d"] = ssm_d[l].reshape(1, d_ssm)
    p["wglu"] = ssm_w_glu[l].astype(BF16)
    d_attn = n_heads * dv
    p["woa"] = w_out[l][:d_attn].astype(BF16)
    p["wos"] = w_out[l][d_attn:].astype(BF16)
    d_ff = w_gate.shape[2]
    p["wg"] = w_gate[l].astype(BF16)
    p["wu"] = w_up[l].astype(BF16)
    p["wd"] = w_down[l].astype(BF16)
    conv_w = ffn_conv_w.shape[1]
    p["cw"] = jnp.pad(ffn_conv_w[l], ((0, -conv_w % SUBLANES), (0, 0)))
    p["cb"] = ffn_conv_b[l].reshape(1, d_ff)
    p["dims"] = dict(q_rank=q_rank, kv_rank=kv_rank, n_heads=n_heads, nope=nope, rope_dim=rope_dim, dv=dv,
                     d_ssm=d_ssm, g=g, n=n, d_ff=d_ff, conv_w=conv_w)
    return p


def _row(v):
    return v.reshape(1, -1)


def kernel(x_prompt, x_sample, mem_prompt, cache_kv_latent, cache_k_rope, page_table, state_ssm_re, state_ssm_im, state_ffn_conv, cache_mem_k, cache_mem_v, norm_mix_pre, w_in, q_norm, kv_norm, w_uq, w_uk, w_uv, ssm_a_re, ssm_a_im, ssm_log_dt, ssm_b_re, ssm_b_im, ssm_c_re, ssm_c_im, ssm_d, ssm_w_glu, norm_attn_out, norm_ssm_out, w_out, norm_mix_post, norm_mem_pre, mem_norm, w_q_mem, w_k_mem, w_v_mem, w_o_mem, norm_mem_post, norm_ffn_pre, w_gate, w_up, ffn_conv_w, ffn_conv_b, w_down, norm_ffn_post):
    depth = w_in.shape[0]
    b, l, d_model = x_prompt.shape
    db, ls, _ = x_sample.shape
    n_mem = mem_prompt.shape[1]
    mem_heads = cache_mem_k.shape[3]
    past_len = page_table.shape[1] * cache_kv_latent.shape[2]
    fc = MXU_DIM
    tm = min(ROW_TILE, l)
    tm_mix = min(2 * ROW_TILE, l)
    tq = min(ROW_TILE, l)
    tc = min(LANES, l)
    pages = min(2 * ROW_TILE * SUBLANES // cache_kv_latent.shape[2], page_table.shape[1] // 2)

    xp = x_prompt.reshape(b * l, d_model)
    xs = x_sample.reshape(db * ls, d_model)
    outs = {k: [] for k in ("p_kv", "p_kr", "p_sr", "p_si", "p_cv", "p_mk", "p_mv",
                            "s_kv", "s_kr", "s_sr", "s_si", "s_cv")}
    for li in range(depth):
        p = _layer_weights(li, w_in, q_norm, kv_norm, w_uq, w_uk, w_uv, ssm_a_re, ssm_a_im, ssm_log_dt,
                           ssm_b_re, ssm_b_im, ssm_c_re, ssm_c_im, ssm_d, ssm_w_glu, w_out, w_gate, w_up,
                           w_down, ffn_conv_w, ffn_conv_b)
        dm = p["dims"]
        n_heads, dv, rope_dim, nope = dm["n_heads"], dm["dv"], dm["rope_dim"], dm["nope"]
        g, n, d_ssm = dm["g"], dm["n"], dm["d_ssm"]
        q_scale = (nope + rope_dim) ** -0.5 * LOG2E
        pre_kw = dict(n_heads=n_heads, q_rank=dm["q_rank"], kv_rank=dm["kv_rank"], d_ssm=d_ssm,
                      rope_dim=rope_dim)
        gpre, gq, gkv = _row(norm_mix_pre[li]), _row(q_norm[li]), _row(kv_norm[li])
        mix_consts = (_row(norm_attn_out[li]), _row(norm_ssm_out[li]), p["woa"], p["wos"],
                      _row(norm_mix_post[li]), _row(norm_mem_pre[li]), w_q_mem[li].astype(BF16),
                      w_o_mem[li].astype(BF16), _row(norm_mem_post[li]))
        ffn_consts = (_row(norm_ffn_pre[li]), p["wg"], p["wu"], p["wd"], p["cw"], p["cb"],
                      _row(norm_ffn_post[li]))

        mk, mv = _memkv_call(mem_prompt.reshape(b * n_mem, d_model), _row(mem_norm[li]),
                             w_k_mem[li].astype(BF16), w_v_mem[li].astype(BF16), tm=min(ROW_TILE, b * n_mem),
                             mem_heads=mem_heads)
        tab_p = _rope_tables(np.arange(l), rope_dim, nope, q_scale)
        q, k, v, ckv, kr, u = _pre_call(xp, tab_p, gpre, gq, gkv, p["win"], p["wq"], p["wkv"], p["vone"],
                                        tm=tm, q_dtype=BF16, **pre_kw)
        attn = _attn_call(q.reshape(b, l, -1), k.reshape(b, l, -1), v.reshape(b, l, -1),
                          n_heads=n_heads, dv=dv, tq=tq)
        zeros_state = np.zeros((b, g * n), np.float32)
        ssm, hr, hi = _s5_call(u.reshape(b, l, d_ssm), zeros_state, zeros_state, p["abr"], p["abi"],
                               p["bbr"], p["bbi"], p["ccr"], p["cci"], p["ssm_d"], p["wglu"], tc=tc)
        xp = _mix_mem_call(xp, attn.reshape(b * l, -1), ssm.reshape(b * l, -1),
                           mk, mv, *mix_consts, tm=tm_mix, rows_per_batch=l, mem_heads=mem_heads,
                           n_mem=n_mem)
        conv0 = np.zeros((b, dm["conv_w"] - 1, dm["d_ff"]), np.float32)
        xp, cv = _ffn_call(xp, conv0, *ffn_consts, nseq=1, lc=tm, fc=fc, n_batch_blocks=b,
                           tiles_per_batch=l // tm)
        outs["p_kv"].append(ckv.reshape(b, l, -1))
        outs["p_kr"].append(kr.reshape(b, l, -1))
        outs["p_sr"].append(hr.reshape(b, g, n))
        outs["p_si"].append(hi.reshape(b, g, n))
        outs["p_cv"].append(cv)
        outs["p_mk"].append(mk.reshape(b, n_mem, mem_heads, -1))
        outs["p_mv"].append(mv.reshape(b, n_mem, mem_heads, -1))

        ts = db * ls
        tab_s = np.tile(_rope_tables(past_len + np.arange(ls), rope_dim, nope, q_scale), (1, db, 1))
        q, _, _, ckv, kr, u = _pre_call(xs, tab_s, gpre, gq, gkv, p["win"], p["wq"], p["wkv"], p["vone"],
                                        tm=ts, q_dtype=F32, **pre_kw)
        attn = _dec_attn_call(page_table, q.reshape(db, ls, -1), ckv.reshape(db, ls, -1),
                              kr.reshape(db, ls, -1), p["wukt"], p["wuvp"], cache_kv_latent[li],
                              jnp.swapaxes(cache_k_rope[li], 1, 2), n_heads=n_heads, dv=dv, pages=pages)
        ssm, hr, hi = _s5_call(u.reshape(db, ls, d_ssm), state_ssm_re[li].reshape(db, g * n),
                               state_ssm_im[li].reshape(db, g * n), p["abr"], p["abi"], p["bbr"], p["bbi"],
                               p["ccr"], p["cci"], p["ssm_d"], p["wglu"], tc=ls)
        xs = _mix_mem_call(xs, attn.reshape(ts, -1), ssm.reshape(ts, -1),
                           cache_mem_k[li].reshape(db * n_mem * mem_heads, -1),
                           cache_mem_v[li].reshape(db * n_mem * mem_heads, -1),
                           *mix_consts, tm=min(SUBLANES, db) * ls, rows_per_batch=ls, mem_heads=mem_heads,
                           n_mem=n_mem)
        xs, cv = _ffn_call(xs, state_ffn_conv[li], *ffn_consts, nseq=db, lc=ls, fc=fc, n_batch_blocks=1,
                           tiles_per_batch=1)
        outs["s_kv"].append(ckv.reshape(db, ls, -1))
        outs["s_kr"].append(kr.reshape(db, ls, -1))
        outs["s_sr"].append(hr.reshape(db, g, n))
        outs["s_si"].append(hi.reshape(db, g, n))
        outs["s_cv"].append(cv)

    st = lambda key: jnp.stack(outs[key])
    return (xp.reshape(b, l, d_model), xs.reshape(db, ls, d_model),
            st("p_kv"), st("p_kr"), st("p_sr"), st("p_si"), st("p_cv"), st("p_mk"), st("p_mv"),
            st("s_kv"), st("s_kr"), st("s_sr"), st("s_si"), st("s_cv"))
```

```python
import functools
import math

import jax
import jax.numpy as jnp
import numpy as np
from jax import lax
from jax.experimental import pallas as pl
from jax.experimental.pallas import tpu as pltpu

F32 = jnp.float32
BF16 = jnp.bfloat16

EPS = 1e-6
ROPE_THETA = 10000.0
LANES = 128
SUBLANES = 8
MXU_DIM = 256
VREGS = 64
ROW_TILE = 2 * MXU_DIM
NEG_BIG = -1e30
LOG2E = 1.4426950408889634
VMEM_LIMIT = 56 * 1024 * 1024


def _rms(x, g):
    y = x * lax.rsqrt(jnp.mean(x * x, axis=-1, keepdims=True) + EPS)
    return y * g


def _dot(a, b):
    return jnp.dot(a, b, preferred_element_type=F32)


def _dot_nt(a, b):
    return lax.dot_general(a, b, (((1,), (1,)), ((), ())), preferred_element_type=F32)


def _rep_lanes(x, n):
    return jnp.concatenate([x] * n, axis=1) if n > 1 else x


def _const_spec(shape):
    nd = len(shape)
    return pl.BlockSpec(shape, lambda *_: (0,) * nd, pipeline_mode=pl.Buffered(1))


def _params(*sem):
    return pltpu.CompilerParams(dimension_semantics=sem, vmem_limit_bytes=VMEM_LIMIT)


def _pre_body(x_ref, tab_ref, gpre_ref, gq_ref, gkv_ref, win_ref, wq_ref, wkv_ref, vone_ref,
              q_ref, k_ref, v_ref, ckv_ref, kr_ref, u_ref, *, n_heads, q_rank, kv_rank, d_ssm, rope_dim):
    x = x_ref[...]
    h = _rms(x, gpre_ref[...]).astype(BF16)
    z = _dot(h, win_ref[...])
    o1 = q_rank
    o2 = o1 + kv_rank
    o3 = o2 + d_ssm
    o4 = o3 + LANES
    cq, ckv, u = z[:, :o1], z[:, o1:o2], z[:, o2:o3]
    k1, k2 = z[:, o3:o4], z[:, o4:o4 + LANES]
    krs = k1 * tab_ref[0] + k2 * tab_ref[1]
    kr_ref[...] = krs[:, :rope_dim]
    ckv_n = _rms(ckv, gkv_ref[...])
    ckv_ref[...] = ckv_n
    kv2 = _dot(ckv_n.astype(BF16), wkv_ref[...])
    nk = n_heads * LANES
    k_ref[...] = (kv2[:, :nk] + _rep_lanes(krs, n_heads)).astype(k_ref.dtype)
    v_ref[...] = (kv2[:, nk:] + vone_ref[...]).astype(v_ref.dtype)
    qn = _rms(cq, gq_ref[...]).astype(BF16)
    qq = _dot(qn, wq_ref[...])
    q = qq[:, :nk] * _rep_lanes(tab_ref[2], n_heads) + qq[:, nk:] * _rep_lanes(tab_ref[3], n_heads)
    q_ref[...] = q.astype(q_ref.dtype)
    u_ref[...] = u


def _pre_call(x2, tab, gpre, gq, gkv, win, wq, wkv, vone, *, tm, n_heads, q_rank, kv_rank, d_ssm, rope_dim,
              q_dtype):
    t, d = x2.shape
    ntab = tab.shape[1] // tm
    nk = n_heads * LANES
    row = lambda w: pl.BlockSpec((tm, w), lambda i: (i, 0))
    body = functools.partial(_pre_body, n_heads=n_heads, q_rank=q_rank, kv_rank=kv_rank, d_ssm=d_ssm,
                             rope_dim=rope_dim)
    return pl.pallas_call(
        body,
        grid=(t // tm,),
        in_specs=[row(d),
                  pl.BlockSpec((4, tm, LANES), lambda i: (0, i % ntab, 0)),
                  _const_spec(gpre.shape), _const_spec(gq.shape), _const_spec(gkv.shape),
                  _const_spec(win.shape), _const_spec(wq.shape), _const_spec(wkv.shape),
                  _const_spec(vone.shape)],
        out_specs=[row(nk), row(nk), row(nk), row(kv_rank), row(rope_dim), row(d_ssm)],
        out_shape=[jax.ShapeDtypeStruct((t, nk), q_dtype),
                   jax.ShapeDtypeStruct((t, nk), BF16),
                   jax.ShapeDtypeStruct((t, nk), BF16),
                   jax.ShapeDtypeStruct((t, kv_rank), F32),
                   jax.ShapeDtypeStruct((t, rope_dim), F32),
                   jax.ShapeDtypeStruct((t, d_ssm), F32)],
        compiler_params=_params("parallel"),
        name="pre_proj",
    )(x2, tab, gpre, gq, gkv, win, wq, wkv, vone)


def _attn_body(qi_ref, kj_ref, q_ref, k_ref, v_ref, o_ref, m_ref, acc_ref, *, n_heads, tq, tk, dv):
    pair = pl.program_id(1)
    i = qi_ref[pair]
    j = kj_ref[pair]

    @pl.when(j == 0)
    def _init():
        m_ref[...] = jnp.full(m_ref.shape, NEG_BIG, F32)
        acc_ref[...] = jnp.zeros(acc_ref.shape, F32)

    items = [(e, h) for e in range(q_ref.shape[0]) for h in range(n_heads)]

    def scores(n):
        e, h = items[n]
        return _dot_nt(q_ref[e, :, h * LANES:(h + 1) * LANES], k_ref[e, :, h * LANES:(h + 1) * LANES])

    def step(masked):
        ahead = not masked
        s_next = scores(0) if ahead else None
        for n, (e, h) in enumerate(items):
            s = s_next if ahead else scores(n)
            if ahead and n + 1 < len(items):
                s_next = scores(n + 1)
            if masked:
                row = lax.broadcasted_iota(jnp.int32, (tq, tk), 0)
                col = lax.broadcasted_iota(jnp.int32, (tq, tk), 1)
                s = jnp.where(col <= row, s, NEG_BIG)
            m_prev = m_ref[n]
            m_next = jnp.maximum(m_prev, jnp.max(s, axis=1, keepdims=True))
            alpha = jnp.exp2(m_prev - m_next)
            p = jnp.exp2(s - _rep_lanes(m_next, tk // LANES))
            m_ref[n] = m_next
            acc_ref[n] = acc_ref[n] * alpha + _dot(p.astype(BF16), v_ref[e, :, h * LANES:(h + 1) * LANES])

    @pl.when(j < i)
    def _off_diag():
        step(False)

    @pl.when(j == i)
    def _diag():
        step(True)
        for n, (e, h) in enumerate(items):
            acc = acc_ref[n]
            o_ref[e, :, h * dv:(h + 1) * dv] = acc[:, :dv] / acc[:, dv:dv + 1]


def _attn_call(q, k, v, *, n_heads, dv, tq):
    b, l, nk = q.shape
    tk = tq
    nq = l // tq
    qi = jnp.asarray([i for i in range(nq) for _ in range(i + 1)], jnp.int32)
    kj = jnp.asarray([j for i in range(nq) for j in range(i + 1)], jnp.int32)
    body = functools.partial(_attn_body, n_heads=n_heads, tq=tq, tk=tk, dv=dv)
    eb = 2 if b % 2 == 0 else 1
    grid_spec = pltpu.PrefetchScalarGridSpec(
        num_scalar_prefetch=2,
        grid=(b // eb, qi.shape[0]),
        in_specs=[pl.BlockSpec((eb, tq, nk), lambda bb, p, qi, kj: (bb, qi[p], 0)),
                  pl.BlockSpec((eb, tk, nk), lambda bb, p, qi, kj: (bb, kj[p], 0)),
                  pl.BlockSpec((eb, tk, nk), lambda bb, p, qi, kj: (bb, kj[p], 0))],
        out_specs=pl.BlockSpec((eb, tq, n_heads * dv), lambda bb, p, qi, kj: (bb, qi[p], 0)),
        scratch_shapes=[pltpu.VMEM((eb * n_heads, tq, LANES), F32),
                        pltpu.VMEM((eb * n_heads, tq, LANES), F32)])
    return pl.pallas_call(
        body,
        grid_spec=grid_spec,
        out_shape=jax.ShapeDtypeStruct((b, l, n_heads * dv), F32),
        compiler_params=_params("parallel", "arbitrary"),
        name="mla_prompt_attn",
    )(qi, kj, q, k, v)


def _dec_attn_body(pt_ref, q_ref, ckv_ref, kr_ref, wukt_ref, wuvp_ref, lat_hbm, pe_hbm, o_ref,
                   lat_buf, pe_buf, sems, ql_s, qp_s, m_s, l_s, acc_s,
                   *, n_heads, s_len, rope_dim, pages, page, groups, key_block):
    slots = lat_buf.shape[0]
    b = pl.program_id(0)
    nb = pl.num_programs(0)
    rows = n_heads * s_len

    def page_copies(group, slot, real):
        out = []
        for r in range(pages):
            pg = pt_ref[group * pages + r] if real else 0
            out.append(pltpu.make_async_copy(lat_hbm.at[pg], lat_buf.at[slot, pl.ds(r * page, page), :],
                                             sems.at[0, slot]))
            out.append(pltpu.make_async_copy(pe_hbm.at[pg], pe_buf.at[slot, :, pl.ds(r * page, page)],
                                             sems.at[1, slot]))
        return out

    def issue(group, slot):
        for c in page_copies(group, slot, True):
            c.start()

    def wait(slot):
        for c in page_copies(0, slot, False):
            c.wait()

    @pl.when(b == 0)
    def _prime():
        for g in range(slots):
            issue(g, g)

    for h in range(n_heads):
        qs = q_ref[:, h * LANES:(h + 1) * LANES]
        ql_s[h * s_len:(h + 1) * s_len, :] = _dot(qs.astype(BF16), wukt_ref[h])
        qp_s[h * s_len:(h + 1) * s_len, :] = qs[:, :rope_dim]
    m_s[...] = jnp.full(m_s.shape, NEG_BIG, F32)
    l_s[...] = jnp.zeros(l_s.shape, F32)
    acc_s[...] = jnp.zeros(acc_s.shape, F32)
    ql = ql_s[...].astype(BF16)
    qp = qp_s[...].astype(BF16)

    def probs(s):
        m_b = jnp.max(s, axis=1, keepdims=True)
        p = jnp.exp2(s - m_b)
        return m_b, jnp.sum(p, axis=1, keepdims=True), p.astype(BF16)

    def merge(parts):
        m_prev = m_s[...]
        m_next = m_prev
        for m_b, _, _ in parts:
            m_next = jnp.maximum(m_next, m_b)
        w = jnp.exp2(m_prev - m_next)
        l = l_s[...] * w
        acc = acc_s[...] * w
        for m_b, l_b, o_b in parts:
            w = jnp.exp2(m_b - m_next)
            l = l + l_b * w
            acc = acc + o_b * w
        m_s[...] = m_next
        l_s[...] = l
        acc_s[...] = acc

    for j in range(groups):
        slot = j % slots
        wait(slot)
        lats, scores = [], []
        for kb in range(pages * page // key_block):
            keys = pl.ds(kb * key_block, key_block)
            lat = lat_buf[slot, keys, :].astype(BF16)
            pe_t = pe_buf[slot, :, keys].astype(BF16)
            lats.append(lat)
            scores.append(_dot_nt(ql, lat) + _dot(qp, pe_t))
        stats = [probs(s) for s in scores]
        merge([(m_b, l_b, _dot(p, lat)) for (m_b, l_b, p), lat in zip(stats, lats)])
        nxt = b * groups + j + slots
        if j + slots < groups:
            issue(nxt, slot)
        else:
            @pl.when(b + 1 < nb)
            def _next_batch(nxt=nxt, slot=slot):
                issue(nxt, slot)

    pad = LANES - s_len
    new_lat = jnp.concatenate([ckv_ref[...], jnp.zeros((pad, ckv_ref.shape[1]), F32)], axis=0).astype(BF16)
    new_pe = jnp.concatenate([kr_ref[...], jnp.zeros((pad, rope_dim), F32)], axis=0).astype(BF16)
    s = _dot_nt(ql, new_lat) + _dot_nt(qp, new_pe)
    row = lax.broadcasted_iota(jnp.int32, (rows, LANES), 0)
    col = lax.broadcasted_iota(jnp.int32, (rows, LANES), 1)
    m_b, l_b, p = probs(jnp.where(col <= row % s_len, s, NEG_BIG))
    merge([(m_b, l_b, _dot(p, new_lat))])

    o_lat = (acc_s[...] / l_s[...]).astype(BF16)
    out = _dot(o_lat[0:s_len], wuvp_ref[0])
    for h in range(1, n_heads):
        out = out + _dot(o_lat[h * s_len:(h + 1) * s_len], wuvp_ref[h])
    o_ref[...] = out


def _dec_attn_call(page_table, q, ckv, kr, wukt, wuvp, cache_lat, cache_pe_t, *, n_heads, dv, pages):
    db, s_len, nk = q.shape
    kv_rank = ckv.shape[-1]
    rope_dim = kr.shape[-1]
    n_pages = page_table.shape[1]
    page = cache_lat.shape[1]
    groups = n_pages // pages
    key_block = min(ROW_TILE, pages * page)
    slots = math.gcd(groups, 4)
    assert n_pages % pages == 0 and slots >= 2 and (pages * page) % key_block == 0
    rows = n_heads * s_len
    body = functools.partial(_dec_attn_body, n_heads=n_heads, s_len=s_len, rope_dim=rope_dim, pages=pages,
                             page=page, groups=groups, key_block=key_block)
    grid_spec = pltpu.PrefetchScalarGridSpec(
        num_scalar_prefetch=1,
        grid=(db,),
        in_specs=[pl.BlockSpec((None, s_len, nk), lambda b, pt: (b, 0, 0)),
                  pl.BlockSpec((None, s_len, kv_rank), lambda b, pt: (b, 0, 0)),
                  pl.BlockSpec((None, s_len, rope_dim), lambda b, pt: (b, 0, 0)),
                  _const_spec(wukt.shape), _const_spec(wuvp.shape),
                  pl.BlockSpec(memory_space=pl.ANY), pl.BlockSpec(memory_space=pl.ANY)],
        out_specs=pl.BlockSpec((None, s_len, n_heads * dv), lambda b, pt: (b, 0, 0)),
        scratch_shapes=[pltpu.VMEM((slots, pages * page, kv_rank), F32),
                        pltpu.VMEM((slots, rope_dim, pages * page), F32),
                        pltpu.SemaphoreType.DMA((2, slots)),
                        pltpu.VMEM((rows, kv_rank), F32),
                        pltpu.VMEM((rows, rope_dim), F32),
                        pltpu.VMEM((rows, 1), F32),
                        pltpu.VMEM((rows, 1), F32),
                        pltpu.VMEM((rows, kv_rank), F32)])
    return pl.pallas_call(
        body,
        grid_spec=grid_spec,
        out_shape=jax.ShapeDtypeStruct((db, s_len, n_heads * dv), F32),
        compiler_params=_params("arbitrary"),
        name="mla_sample_attn",
    )(page_table.reshape(-1), q, ckv, kr, wukt, wuvp, cache_lat, cache_pe_t)


def _s5_disc_body(are_ref, aim_ref, ldt_ref, bre_ref, bim_ref, cre_ref, cim_ref,
                  abr_ref, abi_ref, bbr_ref, bbi_ref, ccr_ref, cci_ref):
    a_re = are_ref[...]
    a_im = aim_ref[...]
    dt = jnp.exp(ldt_ref[...])
    mag = jnp.exp(dt * a_re)
    abr = mag * jnp.cos(dt * a_im)
    abi = mag * jnp.sin(dt * a_im)
    den = a_re * a_re + a_im * a_im
    nr, ni = abr - 1.0, abi
    fr = (nr * a_re + ni * a_im) / den
    fi = (ni * a_re - nr * a_im) / den
    abr_ref[...] = abr
    abi_ref[...] = abi
    b_re = bre_ref[...]
    b_im = bim_ref[...]
    bbr = fr[:, None, :] * b_re - fi[:, None, :] * b_im
    bbi = fr[:, None, :] * b_im + fi[:, None, :] * b_re
    g, c, n = b_re.shape
    gh = g // bbr_ref.shape[0]
    for ref, blocks in ((bbr_ref, bbr), (bbi_ref, bbi), (ccr_ref, cre_ref[...]), (cci_ref, -cim_ref[...])):
        ref[...] = jnp.zeros(ref.shape, ref.dtype)
        for gi in range(g):
            hf, k = divmod(gi, gh)
            ref[hf, k * c:(k + 1) * c, k * n:(k + 1) * n] = blocks[gi].astype(ref.dtype)


def _s5_disc_call(a_re, a_im, log_dt, b_re_t, b_im_t, c_re, c_im, *, halves):
    g, n = a_re.shape
    c = b_re_t.shape[1]
    gh = g // halves
    op = jax.ShapeDtypeStruct((halves, gh * c, gh * n), BF16)
    return pl.pallas_call(
        _s5_disc_body,
        out_shape=[jax.ShapeDtypeStruct((g, n), F32), jax.ShapeDtypeStruct((g, n), F32), op, op, op, op],
        name="s5_discretise",
    )(a_re, a_im, log_dt.reshape(g, 1), b_re_t, b_im_t, c_re, c_im)


def _s5_body(u_ref, h0r_ref, h0i_ref, ar_ref, ai_ref, bbr_ref, bbi_ref, ccr_ref, cci_ref, d_ref, wglu_ref,
             y_ref, hr_out, hi_out, bur, bui, hr_s, hi_s, *, nb, tc, lane_tiles, unroll):
    c = pl.program_id(0)
    d_ssm = u_ref.shape[-1]
    n_state = hr_s.shape[-1]
    halves = bbr_ref.shape[0]
    ch_half = d_ssm // halves
    st_half = n_state // halves

    @pl.when(c == 0)
    def _init():
        hr_s[...] = h0r_ref[...]
        hi_s[...] = h0i_ref[...]

    u = jnp.swapaxes(u_ref[...], 0, 1).reshape(tc * nb, d_ssm)
    ub = u.astype(BF16)
    tiles_half = st_half // LANES
    for hf in range(halves):
        ublk = ub[:, hf * ch_half:(hf + 1) * ch_half]
        br = _dot(ublk, bbr_ref[hf])
        bi = _dot(ublk, bbi_ref[hf])
        for k in range(tiles_half):
            bur[hf * tiles_half + k] = br[:, k * LANES:(k + 1) * LANES]
            bui[hf * tiles_half + k] = bi[:, k * LANES:(k + 1) * LANES]

    for k0 in range(0, n_state // LANES, lane_tiles):
        tiles = range(k0, k0 + lane_tiles)
        ar = [jnp.broadcast_to(ar_ref[:, k * LANES:(k + 1) * LANES], (nb, LANES)) for k in tiles]
        ai = [jnp.broadcast_to(ai_ref[:, k * LANES:(k + 1) * LANES], (nb, LANES)) for k in tiles]

        def body(t, carry, tiles=tiles, ar=ar, ai=ai):
            rows = pl.ds(pl.multiple_of(t * nb, nb), nb)
            out = []
            for n, k in enumerate(tiles):
                hr, hi = carry[2 * n], carry[2 * n + 1]
                nr = ar[n] * hr - ai[n] * hi + bur[k, rows, :]
                ni = ar[n] * hi + ai[n] * hr + bui[k, rows, :]
                bur[k, rows, :] = nr
                bui[k, rows, :] = ni
                out += [nr, ni]
            return tuple(out)

        init = []
        for k in tiles:
            init += [hr_s[:, k * LANES:(k + 1) * LANES], hi_s[:, k * LANES:(k + 1) * LANES]]
        fin = lax.fori_loop(0, tc, body, tuple(init), unroll=unroll)
        for n, k in enumerate(tiles):
            hr_s[:, k * LANES:(k + 1) * LANES] = fin[2 * n]
            hi_s[:, k * LANES:(k + 1) * LANES] = fin[2 * n + 1]

    hr_out[...] = hr_s[...]
    hi_out[...] = hi_s[...]

    ys = []
    for hf in range(halves):
        hr_hist = jnp.concatenate([bur[hf * tiles_half + k] for k in range(tiles_half)], axis=1).astype(BF16)
        hi_hist = jnp.concatenate([bui[hf * tiles_half + k] for k in range(tiles_half)], axis=1).astype(BF16)
        ys.append(_dot_nt(hr_hist, ccr_ref[hf]) + _dot_nt(hi_hist, cci_ref[hf]))
    y = jnp.concatenate(ys, axis=1) + d_ref[...] * u
    g = jax.nn.gelu(y)
    out = g * jax.nn.sigmoid(_dot(g.astype(BF16), wglu_ref[...]))
    y_ref[...] = jnp.swapaxes(out.reshape(tc, nb, d_ssm), 0, 1)


def _s5_call(u3, h0r, h0i, ar, ai, bbr, bbi, ccr, cci, d, wglu, *, tc):
    nb, l, d_ssm = u3.shape
    n_state = ar.shape[-1]
    lane_tiles = math.gcd(n_state // LANES, max(1, VREGS * SUBLANES // (16 * nb)))
    body = functools.partial(_s5_body, nb=nb, tc=tc, lane_tiles=lane_tiles, unroll=tc)
    return pl.pallas_call(
        body,
        grid=(l // tc,),
        in_specs=[pl.BlockSpec((nb, tc, d_ssm), lambda c: (0, c, 0)),
                  _const_spec(h0r.shape), _const_spec(h0i.shape),
                  _const_spec(ar.shape), _const_spec(ai.shape),
                  _const_spec(bbr.shape), _const_spec(bbi.shape),
                  _const_spec(ccr.shape), _const_spec(cci.shape),
                  _const_spec(d.shape), _const_spec(wglu.shape)],
        out_specs=[pl.BlockSpec((nb, tc, d_ssm), lambda c: (0, c, 0)),
                   pl.BlockSpec((nb, n_state), lambda c: (0, 0)),
                   pl.BlockSpec((nb, n_state), lambda c: (0, 0))],
        out_shape=[jax.ShapeDtypeStruct((nb, l, d_ssm), F32),
                   jax.ShapeDtypeStruct((nb, n_state), F32),
                   jax.ShapeDtypeStruct((nb, n_state), F32)],
        scratch_shapes=[pltpu.VMEM((n_state // LANES, nb * tc, LANES), F32),
                        pltpu.VMEM((n_state // LANES, nb * tc, LANES), F32),
                        pltpu.VMEM((nb, n_state), F32), pltpu.VMEM((nb, n_state), F32)],
        compiler_params=_params("arbitrary"),
        name="s5_scan_glu",
    )(u3, h0r, h0i, ar, ai, bbr, bbi, ccr, cci, d, wglu)


def _memkv_body(m_ref, g_ref, wk_ref, wv_ref, k_ref, v_ref):
    m = _rms(m_ref[...], g_ref[...]).astype(BF16)
    k_ref[...] = _dot(m, wk_ref[...]).reshape(k_ref.shape)
    v_ref[...] = _dot(m, wv_ref[...]).reshape(v_ref.shape)


def _memkv_call(mem2, g, wk, wv, *, tm, mem_heads):
    t, d = mem2.shape
    hd = wk.shape[1] // mem_heads
    out = pl.BlockSpec((tm * mem_heads, hd), lambda i: (i, 0))
    return pl.pallas_call(
        _memkv_body,
        grid=(t // tm,),
        in_specs=[pl.BlockSpec((tm, d), lambda i: (i, 0)),
                  _const_spec(g.shape), _const_spec(wk.shape), _const_spec(wv.shape)],
        out_specs=[out, out],
        out_shape=[jax.ShapeDtypeStruct((t * mem_heads, hd), F32)] * 2,
        compiler_params=_params("parallel"),
        name="mem_kv",
    )(mem2, g, wk, wv)


def _mix_mem_body(x_ref, a_ref, s_ref, mk_ref, mv_ref, gao_ref, gso_ref, woa_ref, wos_ref, gmp_ref,
                  gmem_ref, wq_ref, wo_ref, gmo_ref, o_ref, *, mem_heads, n_mem, seqs, sub, mem_scale):
    def head_rows(ref, si, hh):
        return ref[pl.ds((si * n_mem) * mem_heads + hh, n_mem, stride=mem_heads), :].astype(BF16)

    mem = [[(head_rows(mk_ref, si, hh), head_rows(mv_ref, si, hh)) for hh in range(mem_heads)]
           for si in range(seqs)]
    tile = x_ref.shape[0] // sub
    rows = tile // seqs
    hd = wq_ref.shape[1] // mem_heads
    tiles = [pl.ds(t * tile, tile) for t in range(sub)]
    a = [_rms(a_ref[r, :], gao_ref[...]).astype(BF16) for r in tiles]
    s = [_rms(s_ref[r, :], gso_ref[...]).astype(BF16) for r in tiles]
    mix = [_dot(a[t], woa_ref[...]) + _dot(s[t], wos_ref[...]) for t in range(sub)]
    x = [x_ref[tiles[t], :] + _rms(mix[t], gmp_ref[...]) for t in range(sub)]
    h = [_rms(x[t], gmem_ref[...]).astype(BF16) for t in range(sub)]
    q = [_dot(h[t], wq_ref[...]) for t in range(sub)]
    problems = [(t, si, hh) for t in range(sub) for si in range(seqs) for hh in range(mem_heads)]
    scores = {}
    for t, si, hh in problems:
        qh = q[t][si * rows:(si + 1) * rows, hh * hd:(hh + 1) * hd].astype(BF16)
        scores[t, si, hh] = _dot_nt(qh, mem[si][hh][0]) * mem_scale
    probs = {}
    for key in problems:
        e = jnp.exp(scores[key] - jnp.max(scores[key], axis=1, keepdims=True))
        probs[key] = (e / jnp.sum(e, axis=1, keepdims=True)).astype(BF16)
    outs = {key: _dot(probs[key], mem[key[1]][key[2]][1]) for key in problems}
    for t in range(sub):
        per_seq = [jnp.concatenate([outs[t, si, hh] for hh in range(mem_heads)], axis=1) for si in range(seqs)]
        o = (jnp.concatenate(per_seq, axis=0) if seqs > 1 else per_seq[0]).astype(BF16)
        o_ref[tiles[t], :] = x[t] + _rms(_dot(o, wo_ref[...]), gmo_ref[...])


def _mix_mem_call(x2, a2, s2, mk, mv, gao, gso, woa, wos, gmp, gmem, wq, wo, gmo, *, tm, rows_per_batch,
                  mem_heads, n_mem):
    t, d = x2.shape
    da, ds = a2.shape[1], s2.shape[1]
    hd = mk.shape[1]
    seqs = max(1, tm // rows_per_batch)
    tiles_per_batch = max(1, rows_per_batch // tm)
    sub = tm // MXU_DIM if (seqs == 1 and tm % ROW_TILE == 0) else 1
    body = functools.partial(_mix_mem_body, mem_heads=mem_heads, n_mem=n_mem, seqs=seqs, sub=sub,
                             mem_scale=hd ** -0.5)
    row = lambda w: pl.BlockSpec((tm, w), lambda i: (i, 0))
    mem = pl.BlockSpec((seqs * n_mem * mem_heads, hd), lambda i: (i // tiles_per_batch, 0))
    consts = [gao, gso, woa, wos, gmp, gmem, wq, wo, gmo]
    return pl.pallas_call(
        body,
        grid=(t // tm,),
        in_specs=[row(d), row(da), row(ds), mem, mem] + [_const_spec(c.shape) for c in consts],
        out_specs=row(d),
        out_shape=jax.ShapeDtypeStruct((t, d), F32),
        compiler_params=_params("parallel"),
        name="mix_out_mem_attn",
    )(x2, a2, s2, mk, mv, *consts)


def _ffn_body(x_ref, cprev_ref, gpre_ref, wg_ref, wu_ref, wd_ref, cw_ref, cb_ref, gpost_ref,
              o_ref, cnew_ref, halo, work, act, *, nseq, lc, fc, halo_rows):
    t = pl.program_id(1)
    d_ff = wg_ref.shape[1]
    tm = nseq * lc
    keep = cprev_ref.shape[1]
    lo = halo_rows - keep

    @pl.when(t == 0)
    def _load_state():
        halo[...] = cprev_ref[...]

    x = x_ref[...]
    h = _rms(x, gpre_ref[...]).astype(BF16)

    for c in range(d_ff // fc):
        cols = slice(c * fc, (c + 1) * fc)
        g = _dot(h, wg_ref[:, cols]).reshape(nseq, lc, fc)
        up = _dot(h, wu_ref[:, cols]).reshape(nseq, lc, fc)
        work[c, :, lo:halo_rows, :] = halo[:, :, cols]
        work[c, :, halo_rows:halo_rows + lc, :] = g
        w = cw_ref[:, cols]
        conv = w[0:1, :] * work[c, :, lo:lo + lc, :]
        for k in range(1, keep):
            conv = conv + w[k:k + 1, :] * work[c, :, lo + k:lo + k + lc, :]
        conv = conv + w[keep:keep + 1, :] * g
        gc = cb_ref[:, cols] + conv
        act[:, cols] = (jax.nn.silu(gc) * up).reshape(tm, fc).astype(BF16)
        tail = work[c, :, lc + lo:lc + halo_rows, :]
        halo[:, :, cols] = tail
        cnew_ref[:, :, cols] = tail

    o_ref[...] = x + _rms(_dot(act[...], wd_ref[...]), gpost_ref[...])


def _ffn_call(x2, cprev, gpre, wg, wu, wd, cw, cb, gpost, *, nseq, lc, fc, n_batch_blocks, tiles_per_batch):
    t, d = x2.shape
    d_ff = wg.shape[1]
    n_chunks = d_ff // fc
    keep = cprev.shape[1]
    halo_rows = SUBLANES
    tm = nseq * lc
    body = functools.partial(_ffn_body, nseq=nseq, lc=lc, fc=fc, halo_rows=halo_rows)
    state = pl.BlockSpec((nseq, keep, d_ff), lambda b, i: (b, 0, 0))
    return pl.pallas_call(
        body,
        grid=(n_batch_blocks, tiles_per_batch),
        in_specs=[pl.BlockSpec((tm, d), lambda b, i: (b * tiles_per_batch + i, 0)),
                  state,
                  _const_spec(gpre.shape), _const_spec(wg.shape), _const_spec(wu.shape),
                  _const_spec(wd.shape), _const_spec(cw.shape), _const_spec(cb.shape),
                  _const_spec(gpost.shape)],
        out_specs=[pl.BlockSpec((tm, d), lambda b, i: (b * tiles_per_batch + i, 0)), state],
        out_shape=[jax.ShapeDtypeStruct((t, d), F32),
                   jax.ShapeDtypeStruct(cprev.shape, F32)],
        scratch_shapes=[pltpu.VMEM((nseq, keep, d_ff), F32),
                        pltpu.VMEM((n_chunks, nseq, halo_rows + lc, fc), F32),
                        pltpu.VMEM((tm, d_ff), BF16)],
        compiler_params=_params("parallel", "arbitrary"),
        name="conv_ffn",
    )(x2, cprev, gpre, wg, wu, wd, cw, cb, gpost)


def _rope_tables(pos, rope_dim, nope_dim, q_scale):
    half = rope_dim // 2
    inv = ROPE_THETA ** (-np.arange(half, dtype=np.float64) * (2.0 / rope_dim))
    ang = np.asarray(pos, np.float64)[:, None] * inv[None, :]
    cos, sin = np.cos(ang), np.sin(ang)
    n = ang.shape[0]
    pad = np.zeros((n, LANES - rope_dim))
    ck = np.concatenate([cos, cos, pad], axis=1)
    sk = np.concatenate([sin, sin, pad], axis=1)
    ones = np.concatenate([np.ones((n, nope_dim)), np.zeros((n, LANES - rope_dim - nope_dim))], axis=1)
    cq = np.concatenate([cos, cos, ones], axis=1) * q_scale
    return np.stack([ck, sk, cq, sk * q_scale]).astype(np.float32)


def _rot_half_cols(w):
    half = w.shape[-1] // 2
    return jnp.concatenate([-w[..., half:], w[..., :half]], axis=-1)


def _pad_last(w, width):
    return jnp.pad(w, [(0, 0)] * (w.ndim - 1) + [(0, width - w.shape[-1])])


def _layer_weights(l, w_in, q_norm, kv_norm, w_uq, w_uk, w_uv, ssm_a_re, ssm_a_im, ssm_log_dt, ssm_b_re,
                   ssm_b_im, ssm_c_re, ssm_c_im, ssm_d, ssm_w_glu, w_out, w_gate, w_up, w_down, ffn_conv_w,
                   ffn_conv_b):
    q_rank = q_norm.shape[-1]
    kv_rank = kv_norm.shape[-1]
    n_heads = w_uq.shape[2]
    nope = w_uk.shape[3]
    rope_dim = w_uq.shape[3] - nope
    dv = w_uv.shape[3]
    d_ssm = ssm_d.shape[-1]
    win = w_in[l]
    o2, o3 = q_rank + kv_rank, q_rank + kv_rank + rope_dim
    w_kr = win[:, o2:o3]
    p = {}
    p["win"] = jnp.concatenate([win[:, :o2], win[:, o3:], _pad_last(w_kr, LANES),
                                _pad_last(_rot_half_cols(w_kr), LANES)], axis=1).astype(BF16)
    uq = w_uq[l]
    q_nope, q_pe = uq[..., :nope], uq[..., nope:]
    wq1 = _pad_last(jnp.concatenate([q_pe, q_nope], axis=-1), LANES).reshape(q_rank, n_heads * LANES)
    wq2 = _pad_last(_rot_half_cols(q_pe), LANES).reshape(q_rank, n_heads * LANES)
    p["wq"] = jnp.concatenate([wq1, wq2], axis=1).astype(BF16)
    uk = w_uk[l]
    wuk_slots = jnp.pad(uk, ((0, 0), (0, 0), (rope_dim, LANES - rope_dim - nope)))
    p["wkv"] = jnp.concatenate([wuk_slots.reshape(kv_rank, n_heads * LANES),
                                _pad_last(w_uv[l], LANES).reshape(kv_rank, n_heads * LANES)], axis=1).astype(BF16)
    p["vone"] = jnp.tile((jnp.arange(LANES) == dv).astype(F32), n_heads).reshape(1, n_heads * LANES)
    p["wukt"] = jnp.transpose(wuk_slots, (1, 2, 0)).astype(BF16)
    uv = jnp.transpose(w_uv[l], (1, 0, 2))
    eye = jnp.eye(n_heads, dtype=F32)
    p["wuvp"] = (uv[:, :, None, :] * eye[:, None, :, None]).reshape(n_heads, kv_rank, n_heads * dv).astype(BF16)
    g, n = ssm_a_re.shape[1:]
    abr, abi, p["bbr"], p["bbi"], p["ccr"], p["cci"] = _s5_disc_call(
        ssm_a_re[l], ssm_a_im[l], ssm_log_dt[l], jnp.transpose(ssm_b_re[l], (0, 2, 1)),
        jnp.transpose(ssm_b_im[l], (0, 2, 1)), ssm_c_re[l], ssm_c_im[l], halves=2)
    p["abr"] = abr.reshape(1, g * n)
    p["abi"] = abi.reshape(1, g * n)
    p["ssm_d"] = ssm_d[l].reshape(1, d_ssm)
    p["wglu"] = ssm_w_glu[l].astype(BF16)
    d_attn = n_heads * dv
    p["woa"] = w_out[l][:d_attn].astype(BF16)
    p["wos"] = w_out[l][d_attn:].astype(BF16)
    d_ff = w_gate.shape[2]
    p["wg"] = w_gate[l].astype(BF16)
    p["wu"] = w_up[l].astype(BF16)
    p["wd"] = w_down[l].astype(BF16)
    conv_w = ffn_conv_w.shape[1]
    p["cw"] = jnp.pad(ffn_conv_w[l], ((0, -conv_w % SUBLANES), (0, 0)))
    p["cb"] = ffn_conv_b[l].reshape(1, d_ff)
    p["dims"] = dict(q_rank=q_rank, kv_rank=kv_rank, n_heads=n_heads, nope=nope, rope_dim=rope_dim, dv=dv,
                     d_ssm=d_ssm, g=g, n=n, d_ff=d_ff, conv_w=conv_w)
    return p


def _row(v):
    return v.reshape(1, -1)


def kernel(x_prompt, x_sample, mem_prompt, cache_kv_latent, cache_k_rope, page_table, state_ssm_re, state_ssm_im, state_ffn_conv, cache_mem_k, cache_mem_v, norm_mix_pre, w_in, q_norm, kv_norm, w_uq, w_uk, w_uv, ssm_a_re, ssm_a_im, ssm_log_dt, ssm_b_re, ssm_b_im, ssm_c_re, ssm_c_im, ssm_d, ssm_w_glu, norm_attn_out, norm_ssm_out, w_out, norm_mix_post, norm_mem_pre, mem_norm, w_q_mem, w_k_mem, w_v_mem, w_o_mem, norm_mem_post, norm_ffn_pre, w_gate, w_up, ffn_conv_w, ffn_conv_b, w_down, norm_ffn_post):
    depth = w_in.shape[0]
    b, l, d_model = x_prompt.shape
    db, ls, _ = x_sample.shape
    n_mem = mem_prompt.shape[1]
    mem_heads = cache_mem_k.shape[3]
    past_len = page_table.shape[1] * cache_kv_latent.shape[2]
    fc = MXU_DIM
    tm = min(ROW_TILE, l)
    tm_wide = min(2 * ROW_TILE, l)
    tq = min(ROW_TILE, l)
    tc = min(LANES, l)
    pages = min(2 * ROW_TILE * SUBLANES // cache_kv_latent.shape[2], page_table.shape[1] // 2)

    xp = x_prompt.reshape(b * l, d_model)
    xs = x_sample.reshape(db * ls, d_model)
    outs = {k: [] for k in ("p_kv", "p_kr", "p_sr", "p_si", "p_cv", "p_mk", "p_mv",
                            "s_kv", "s_kr", "s_sr", "s_si", "s_cv")}
    for li in range(depth):
        p = _layer_weights(li, w_in, q_norm, kv_norm, w_uq, w_uk, w_uv, ssm_a_re, ssm_a_im, ssm_log_dt,
                           ssm_b_re, ssm_b_im, ssm_c_re, ssm_c_im, ssm_d, ssm_w_glu, w_out, w_gate, w_up,
                           w_down, ffn_conv_w, ffn_conv_b)
        dm = p["dims"]
        n_heads, dv, rope_dim, nope = dm["n_heads"], dm["dv"], dm["rope_dim"], dm["nope"]
        g, n, d_ssm = dm["g"], dm["n"], dm["d_ssm"]
        q_scale = (nope + rope_dim) ** -0.5 * LOG2E
        pre_kw = dict(n_heads=n_heads, q_rank=dm["q_rank"], kv_rank=dm["kv_rank"], d_ssm=d_ssm,
                      rope_dim=rope_dim)
        gpre, gq, gkv = _row(norm_mix_pre[li]), _row(q_norm[li]), _row(kv_norm[li])
        mix_consts = (_row(norm_attn_out[li]), _row(norm_ssm_out[li]), p["woa"], p["wos"],
                      _row(norm_mix_post[li]), _row(norm_mem_pre[li]), w_q_mem[li].astype(BF16),
                      w_o_mem[li].astype(BF16), _row(norm_mem_post[li]))
        ffn_consts = (_row(norm_ffn_pre[li]), p["wg"], p["wu"], p["wd"], p["cw"], p["cb"],
                      _row(norm_ffn_post[li]))

        mk, mv = _memkv_call(mem_prompt.reshape(b * n_mem, d_model), _row(mem_norm[li]),
                             w_k_mem[li].astype(BF16), w_v_mem[li].astype(BF16), tm=min(ROW_TILE, b * n_mem),
                             mem_heads=mem_heads)
        tab_p = _rope_tables(np.arange(l), rope_dim, nope, q_scale)
        q, k, v, ckv, kr, u = _pre_call(xp, tab_p, gpre, gq, gkv, p["win"], p["wq"], p["wkv"], p["vone"],
                                        tm=tm_wide, q_dtype=BF16, **pre_kw)
        attn = _attn_call(q.reshape(b, l, -1), k.reshape(b, l, -1), v.reshape(b, l, -1),
                          n_heads=n_heads, dv=dv, tq=tq)
        zeros_state = np.zeros((b, g * n), np.float32)
        ssm, hr, hi = _s5_call(u.reshape(b, l, d_ssm), zeros_state, zeros_state, p["abr"], p["abi"],
                               p["bbr"], p["bbi"], p["ccr"], p["cci"], p["ssm_d"], p["wglu"], tc=tc)
        xp = _mix_mem_call(xp, attn.reshape(b * l, -1), ssm.reshape(b * l, -1),
                           mk, mv, *mix_consts, tm=tm_wide, rows_per_batch=l, mem_heads=mem_heads,
                           n_mem=n_mem)
        conv0 = np.zeros((b, dm["conv_w"] - 1, dm["d_ff"]), np.float32)
        xp, cv = _ffn_call(xp, conv0, *ffn_consts, nseq=1, lc=tm, fc=fc, n_batch_blocks=b,
                           tiles_per_batch=l // tm)
        outs["p_kv"].append(ckv.reshape(b, l, -1))
        outs["p_kr"].append(kr.reshape(b, l, -1))
        outs["p_sr"].append(hr.reshape(b, g, n))
        outs["p_si"].append(hi.reshape(b, g, n))
        outs["p_cv"].append(cv)
        outs["p_mk"].append(mk.reshape(b, n_mem, mem_heads, -1))
        outs["p_mv"].append(mv.reshape(b, n_mem, mem_heads, -1))

        ts = db * ls
        tab_s = np.tile(_rope_tables(past_len + np.arange(ls), rope_dim, nope, q_scale), (1, db, 1))
        q, _, _, ckv, kr, u = _pre_call(xs, tab_s, gpre, gq, gkv, p["win"], p["wq"], p["wkv"], p["vone"],
                                        tm=ts, q_dtype=F32, **pre_kw)
        attn = _dec_attn_call(page_table, q.reshape(db, ls, -1), ckv.reshape(db, ls, -1),
                              kr.reshape(db, ls, -1), p["wukt"], p["wuvp"], cache_kv_latent[li],
                              jnp.swapaxes(cache_k_rope[li], 1, 2), n_heads=n_heads, dv=dv, pages=pages)
        ssm, hr, hi = _s5_call(u.reshape(db, ls, d_ssm), state_ssm_re[li].reshape(db, g * n),
                               state_ssm_im[li].reshape(db, g * n), p["abr"], p["abi"], p["bbr"], p["bbi"],
                               p["ccr"], p["cci"], p["ssm_d"], p["wglu"], tc=ls)
        xs = _mix_mem_call(xs, attn.reshape(ts, -1), ssm.reshape(ts, -1),
                           cache_mem_k[li].reshape(db * n_mem * mem_heads, -1),
                           cache_mem_v[li].reshape(db * n_mem * mem_heads, -1),
                           *mix_consts, tm=min(SUBLANES, db) * ls, rows_per_batch=ls, mem_heads=mem_heads,
                           n_mem=n_mem)
        xs, cv = _ffn_call(xs, state_ffn_conv[li], *ffn_consts, nseq=db, lc=ls, fc=fc, n_batch_blocks=1,
                           tiles_per_batch=1)
        outs["s_kv"].append(ckv.reshape(db, ls, -1))
        outs["s_kr"].append(kr.reshape(db, ls, -1))
        outs["s_sr"].append(hr.reshape(db, g, n))
        outs["s_si"].append(hi.reshape(db, g, n))
        outs["s_cv"].append(cv)

    st = lambda key: jnp.stack(outs[key])
    return (xp.reshape(b, l, d_model), xs.reshape(db, ls, d_model),
            st("p_kv"), st("p_kr"), st("p_sr"), st("p_si"), st("p_cv"), st("p_mk"), st("p_mv"),
            st("s_kv"), st("s_kr"), st("s_sr"), st("s_si"), st("s_cv"))
```

```python
import functools
import math

import jax
import jax.numpy as jnp
import numpy as np
from jax import lax
from jax.experimental import pallas as pl
from jax.experimental.pallas import tpu as pltpu

F32 = jnp.float32
BF16 = jnp.bfloat16

EPS = 1e-6
ROPE_THETA = 10000.0
LANES = 128
SUBLANES = 8
MXU_DIM = 256
VREGS = 64
ROW_TILE = 2 * MXU_DIM
NEG_BIG = -1e30
LOG2E = 1.4426950408889634
VMEM_LIMIT = 56 * 1024 * 1024


def _rms(x, g):
    y = x * lax.rsqrt(jnp.mean(x * x, axis=-1, keepdims=True) + EPS)
    return y * g


def _dot(a, b):
    return jnp.dot(a, b, preferred_element_type=F32)


def _dot_nt(a, b):
    return lax.dot_general(a, b, (((1,), (1,)), ((), ())), preferred_element_type=F32)


def _rep_lanes(x, n):
    return jnp.concatenate([x] * n, axis=1) if n > 1 else x


def _const_spec(shape):
    nd = len(shape)
    return pl.BlockSpec(shape, lambda *_: (0,) * nd, pipeline_mode=pl.Buffered(1))


def _params(*sem):
    return pltpu.CompilerParams(dimension_semantics=sem, vmem_limit_bytes=VMEM_LIMIT)


def _pre_body(x_ref, tab_ref, gpre_ref, gq_ref, gkv_ref, win_ref, wq_ref, wkv_ref, vone_ref,
              q_ref, k_ref, v_ref, ckv_ref, kr_ref, u_ref, *, n_heads, q_rank, kv_rank, d_ssm, rope_dim):
    x = x_ref[...]
    h = _rms(x, gpre_ref[...]).astype(BF16)
    z = _dot(h, win_ref[...])
    o1 = q_rank
    o2 = o1 + kv_rank
    o3 = o2 + d_ssm
    o4 = o3 + LANES
    cq, ckv, u = z[:, :o1], z[:, o1:o2], z[:, o2:o3]
    k1, k2 = z[:, o3:o4], z[:, o4:o4 + LANES]
    krs = k1 * tab_ref[0] + k2 * tab_ref[1]
    kr_ref[...] = krs[:, :rope_dim]
    ckv_n = _rms(ckv, gkv_ref[...])
    ckv_ref[...] = ckv_n
    kv2 = _dot(ckv_n.astype(BF16), wkv_ref[...])
    nk = n_heads * LANES
    k_ref[...] = (kv2[:, :nk] + _rep_lanes(krs, n_heads)).astype(k_ref.dtype)
    v_ref[...] = (kv2[:, nk:] + vone_ref[...]).astype(v_ref.dtype)
    qn = _rms(cq, gq_ref[...]).astype(BF16)
    qq = _dot(qn, wq_ref[...])
    q = qq[:, :nk] * _rep_lanes(tab_ref[2], n_heads) + qq[:, nk:] * _rep_lanes(tab_ref[3], n_heads)
    q_ref[...] = q.astype(q_ref.dtype)
    u_ref[...] = u


def _pre_call(x2, tab, gpre, gq, gkv, win, wq, wkv, vone, *, tm, n_heads, q_rank, kv_rank, d_ssm, rope_dim,
              q_dtype):
    t, d = x2.shape
    ntab = tab.shape[1] // tm
    nk = n_heads * LANES
    row = lambda w: pl.BlockSpec((tm, w), lambda i: (i, 0))
    body = functools.partial(_pre_body, n_heads=n_heads, q_rank=q_rank, kv_rank=kv_rank, d_ssm=d_ssm,
                             rope_dim=rope_dim)
    return pl.pallas_call(
        body,
        grid=(t // tm,),
        in_specs=[row(d),
                  pl.BlockSpec((4, tm, LANES), lambda i: (0, i % ntab, 0)),
                  _const_spec(gpre.shape), _const_spec(gq.shape), _const_spec(gkv.shape),
                  _const_spec(win.shape), _const_spec(wq.shape), _const_spec(wkv.shape),
                  _const_spec(vone.shape)],
        out_specs=[row(nk), row(nk), row(nk), row(kv_rank), row(rope_dim), row(d_ssm)],
        out_shape=[jax.ShapeDtypeStruct((t, nk), q_dtype),
                   jax.ShapeDtypeStruct((t, nk), BF16),
                   jax.ShapeDtypeStruct((t, nk), BF16),
                   jax.ShapeDtypeStruct((t, kv_rank), F32),
                   jax.ShapeDtypeStruct((t, rope_dim), F32),
                   jax.ShapeDtypeStruct((t, d_ssm), F32)],
        compiler_params=_params("parallel"),
        name="pre_proj",
    )(x2, tab, gpre, gq, gkv, win, wq, wkv, vone)


def _attn_body(qi_ref, kj_ref, q_ref, k_ref, v_ref, o_ref, m_ref, acc_ref, *, n_heads, tq, tk, dv):
    pair = pl.program_id(1)
    i = qi_ref[pair]
    j = kj_ref[pair]

    @pl.when(j == 0)
    def _init():
        m_ref[...] = jnp.full(m_ref.shape, NEG_BIG, F32)
        acc_ref[...] = jnp.zeros(acc_ref.shape, F32)

    items = [(e, h) for e in range(q_ref.shape[0]) for h in range(n_heads)]

    def scores(n):
        e, h = items[n]
        return _dot_nt(q_ref[e, :, h * LANES:(h + 1) * LANES], k_ref[e, :, h * LANES:(h + 1) * LANES])

    def step(masked):
        ahead = not masked
        s_next = scores(0) if ahead else None
        for n, (e, h) in enumerate(items):
            s = s_next if ahead else scores(n)
            if ahead and n + 1 < len(items):
                s_next = scores(n + 1)
            if masked:
                row = lax.broadcasted_iota(jnp.int32, (tq, tk), 0)
                col = lax.broadcasted_iota(jnp.int32, (tq, tk), 1)
                s = jnp.where(col <= row, s, NEG_BIG)
            m_prev = m_ref[n]
            m_next = jnp.maximum(m_prev, jnp.max(s, axis=1, keepdims=True))
            alpha = jnp.exp2(m_prev - m_next)
            p = jnp.exp2(s - _rep_lanes(m_next, tk // LANES))
            m_ref[n] = m_next
            acc_ref[n] = acc_ref[n] * alpha + _dot(p.astype(BF16), v_ref[e, :, h * LANES:(h + 1) * LANES])

    @pl.when(j < i)
    def _off_diag():
        step(False)

    @pl.when(j == i)
    def _diag():
        step(True)
        for n, (e, h) in enumerate(items):
            acc = acc_ref[n]
            o_ref[e, :, h * dv:(h + 1) * dv] = acc[:, :dv] / acc[:, dv:dv + 1]


def _attn_call(q, k, v, *, n_heads, dv, tq):
    b, l, nk = q.shape
    tk = tq
    nq = l // tq
    qi = jnp.asarray([i for i in range(nq) for _ in range(i + 1)], jnp.int32)
    kj = jnp.asarray([j for i in range(nq) for j in range(i + 1)], jnp.int32)
    body = functools.partial(_attn_body, n_heads=n_heads, tq=tq, tk=tk, dv=dv)
    eb = 2 if b % 2 == 0 else 1
    grid_spec = pltpu.PrefetchScalarGridSpec(
        num_scalar_prefetch=2,
        grid=(b // eb, qi.shape[0]),
        in_specs=[pl.BlockSpec((eb, tq, nk), lambda bb, p, qi, kj: (bb, qi[p], 0)),
                  pl.BlockSpec((eb, tk, nk), lambda bb, p, qi, kj: (bb, kj[p], 0)),
                  pl.BlockSpec((eb, tk, nk), lambda bb, p, qi, kj: (bb, kj[p], 0))],
        out_specs=pl.BlockSpec((eb, tq, n_heads * dv), lambda bb, p, qi, kj: (bb, qi[p], 0)),
        scratch_shapes=[pltpu.VMEM((eb * n_heads, tq, LANES), F32),
                        pltpu.VMEM((eb * n_heads, tq, LANES), F32)])
    return pl.pallas_call(
        body,
        grid_spec=grid_spec,
        out_shape=jax.ShapeDtypeStruct((b, l, n_heads * dv), F32),
        compiler_params=_params("parallel", "arbitrary"),
        name="mla_prompt_attn",
    )(qi, kj, q, k, v)


def _dec_attn_body(pt_ref, q_ref, ckv_ref, kr_ref, wukt_ref, wuvp_ref, lat_hbm, pe_hbm, o_ref,
                   lat_buf, pe_buf, sems, ql_s, qp_s, m_s, l_s, acc_s,
                   *, n_heads, s_len, rope_dim, pages, page, groups, key_block):
    slots = lat_buf.shape[0]
    b = pl.program_id(0)
    nb = pl.num_programs(0)
    rows = n_heads * s_len

    def page_copies(group, slot, real):
        out = []
        for r in range(pages):
            pg = pt_ref[group * pages + r] if real else 0
            out.append(pltpu.make_async_copy(lat_hbm.at[pg], lat_buf.at[slot, pl.ds(r * page, page), :],
                                             sems.at[0, slot]))
            out.append(pltpu.make_async_copy(pe_hbm.at[pg], pe_buf.at[slot, :, pl.ds(r * page, page)],
                                             sems.at[1, slot]))
        return out

    def issue(group, slot):
        for n, c in enumerate(page_copies(group, slot, True)):
            c.start(priority=n % 2)

    def wait(slot):
        for c in page_copies(0, slot, False):
            c.wait()

    @pl.when(b == 0)
    def _prime():
        for g in range(slots):
            issue(g, g)

    for h in range(n_heads):
        qs = q_ref[:, h * LANES:(h + 1) * LANES]
        ql_s[h * s_len:(h + 1) * s_len, :] = _dot(qs.astype(BF16), wukt_ref[h])
        qp_s[h * s_len:(h + 1) * s_len, :] = qs[:, :rope_dim]
    m_s[...] = jnp.full(m_s.shape, NEG_BIG, F32)
    l_s[...] = jnp.zeros(l_s.shape, F32)
    acc_s[...] = jnp.zeros(acc_s.shape, F32)
    ql = ql_s[...].astype(BF16)
    qp = qp_s[...].astype(BF16)

    def probs(s):
        m_b = jnp.max(s, axis=1, keepdims=True)
        p = jnp.exp2(s - m_b)
        return m_b, jnp.sum(p, axis=1, keepdims=True), p.astype(BF16)

    def merge(parts):
        m_prev = m_s[...]
        m_next = m_prev
        for m_b, _, _ in parts:
            m_next = jnp.maximum(m_next, m_b)
        w = jnp.exp2(m_prev - m_next)
        l = l_s[...] * w
        acc = acc_s[...] * w
        for m_b, l_b, o_b in parts:
            w = jnp.exp2(m_b - m_next)
            l = l + l_b * w
            acc = acc + o_b * w
        m_s[...] = m_next
        l_s[...] = l
        acc_s[...] = acc

    for j in range(groups):
        slot = j % slots
        wait(slot)
        lats, scores = [], []
        for kb in range(pages * page // key_block):
            keys = pl.ds(kb * key_block, key_block)
            lat = lat_buf[slot, keys, :].astype(BF16)
            pe_t = pe_buf[slot, :, keys].astype(BF16)
            lats.append(lat)
            scores.append(_dot_nt(ql, lat) + _dot(qp, pe_t))
        stats = [probs(s) for s in scores]
        merge([(m_b, l_b, _dot(p, lat)) for (m_b, l_b, p), lat in zip(stats, lats)])
        nxt = b * groups + j + slots
        if j + slots < groups:
            issue(nxt, slot)
        else:
            @pl.when(b + 1 < nb)
            def _next_batch(nxt=nxt, slot=slot):
                issue(nxt, slot)

    pad = LANES - s_len
    new_lat = jnp.concatenate([ckv_ref[...], jnp.zeros((pad, ckv_ref.shape[1]), F32)], axis=0).astype(BF16)
    new_pe = jnp.concatenate([kr_ref[...], jnp.zeros((pad, rope_dim), F32)], axis=0).astype(BF16)
    s = _dot_nt(ql, new_lat) + _dot_nt(qp, new_pe)
    row = lax.broadcasted_iota(jnp.int32, (rows, LANES), 0)
    col = lax.broadcasted_iota(jnp.int32, (rows, LANES), 1)
    m_b, l_b, p = probs(jnp.where(col <= row % s_len, s, NEG_BIG))
    merge([(m_b, l_b, _dot(p, new_lat))])

    o_lat = (acc_s[...] / l_s[...]).astype(BF16)
    out = _dot(o_lat[0:s_len], wuvp_ref[0])
    for h in range(1, n_heads):
        out = out + _dot(o_lat[h * s_len:(h + 1) * s_len], wuvp_ref[h])
    o_ref[...] = out


def _dec_attn_call(page_table, q, ckv, kr, wukt, wuvp, cache_lat, cache_pe_t, *, n_heads, dv, pages):
    db, s_len, nk = q.shape
    kv_rank = ckv.shape[-1]
    rope_dim = kr.shape[-1]
    n_pages = page_table.shape[1]
    page = cache_lat.shape[1]
    groups = n_pages // pages
    key_block = min(ROW_TILE, pages * page)
    slots = math.gcd(groups, 4)
    assert n_pages % pages == 0 and slots >= 2 and (pages * page) % key_block == 0
    rows = n_heads * s_len
    body = functools.partial(_dec_attn_body, n_heads=n_heads, s_len=s_len, rope_dim=rope_dim, pages=pages,
                             page=page, groups=groups, key_block=key_block)
    grid_spec = pltpu.PrefetchScalarGridSpec(
        num_scalar_prefetch=1,
        grid=(db,),
        in_specs=[pl.BlockSpec((None, s_len, nk), lambda b, pt: (b, 0, 0)),
                  pl.BlockSpec((None, s_len, kv_rank), lambda b, pt: (b, 0, 0)),
                  pl.BlockSpec((None, s_len, rope_dim), lambda b, pt: (b, 0, 0)),
                  _const_spec(wukt.shape), _const_spec(wuvp.shape),
                  pl.BlockSpec(memory_space=pl.ANY), pl.BlockSpec(memory_space=pl.ANY)],
        out_specs=pl.BlockSpec((None, s_len, n_heads * dv), lambda b, pt: (b, 0, 0)),
        scratch_shapes=[pltpu.VMEM((slots, pages * page, kv_rank), F32),
                        pltpu.VMEM((slots, rope_dim, pages * page), F32),
                        pltpu.SemaphoreType.DMA((2, slots)),
                        pltpu.VMEM((rows, kv_rank), F32),
                        pltpu.VMEM((rows, rope_dim), F32),
                        pltpu.VMEM((rows, 1), F32),
                        pltpu.VMEM((rows, 1), F32),
                        pltpu.VMEM((rows, kv_rank), F32)])
    return pl.pallas_call(
        body,
        grid_spec=grid_spec,
        out_shape=jax.ShapeDtypeStruct((db, s_len, n_heads * dv), F32),
        compiler_params=_params("arbitrary"),
        name="mla_sample_attn",
    )(page_table.reshape(-1), q, ckv, kr, wukt, wuvp, cache_lat, cache_pe_t)


def _s5_disc_body(are_ref, aim_ref, ldt_ref, bre_ref, bim_ref, cre_ref, cim_ref,
                  abr_ref, abi_ref, bbr_ref, bbi_ref, ccr_ref, cci_ref):
    a_re = are_ref[...]
    a_im = aim_ref[...]
    dt = jnp.exp(ldt_ref[...])
    mag = jnp.exp(dt * a_re)
    abr = mag * jnp.cos(dt * a_im)
    abi = mag * jnp.sin(dt * a_im)
    den = a_re * a_re + a_im * a_im
    nr, ni = abr - 1.0, abi
    fr = (nr * a_re + ni * a_im) / den
    fi = (ni * a_re - nr * a_im) / den
    abr_ref[...] = abr
    abi_ref[...] = abi
    b_re = bre_ref[...]
    b_im = bim_ref[...]
    bbr = fr[:, None, :] * b_re - fi[:, None, :] * b_im
    bbi = fr[:, None, :] * b_im + fi[:, None, :] * b_re
    g, c, n = b_re.shape
    gh = g // bbr_ref.shape[0]
    for ref, blocks in ((bbr_ref, bbr), (bbi_ref, bbi), (ccr_ref, cre_ref[...]), (cci_ref, -cim_ref[...])):
        ref[...] = jnp.zeros(ref.shape, ref.dtype)
        for gi in range(g):
            hf, k = divmod(gi, gh)
            ref[hf, k * c:(k + 1) * c, k * n:(k + 1) * n] = blocks[gi].astype(ref.dtype)


def _s5_disc_call(a_re, a_im, log_dt, b_re_t, b_im_t, c_re, c_im, *, halves):
    g, n = a_re.shape
    c = b_re_t.shape[1]
    gh = g // halves
    op = jax.ShapeDtypeStruct((halves, gh * c, gh * n), BF16)
    return pl.pallas_call(
        _s5_disc_body,
        out_shape=[jax.ShapeDtypeStruct((g, n), F32), jax.ShapeDtypeStruct((g, n), F32), op, op, op, op],
        name="s5_discretise",
    )(a_re, a_im, log_dt.reshape(g, 1), b_re_t, b_im_t, c_re, c_im)


def _s5_body(u_ref, h0r_ref, h0i_ref, ar_ref, ai_ref, bbr_ref, bbi_ref, ccr_ref, cci_ref, d_ref, wglu_ref,
             y_ref, hr_out, hi_out, bur, bui, hr_s, hi_s, *, nb, tc, lane_tiles, unroll):
    c = pl.program_id(0)
    d_ssm = u_ref.shape[-1]
    n_state = hr_s.shape[-1]
    halves = bbr_ref.shape[0]
    ch_half = d_ssm // halves
    st_half = n_state // halves

    @pl.when(c == 0)
    def _init():
        hr_s[...] = h0r_ref[...]
        hi_s[...] = h0i_ref[...]

    u = jnp.swapaxes(u_ref[...], 0, 1).reshape(tc * nb, d_ssm)
    ub = u.astype(BF16)
    tiles_half = st_half // LANES
    for hf in range(halves):
        ublk = ub[:, hf * ch_half:(hf + 1) * ch_half]
        br = _dot(ublk, bbr_ref[hf])
        bi = _dot(ublk, bbi_ref[hf])
        for k in range(tiles_half):
            bur[hf * tiles_half + k] = br[:, k * LANES:(k + 1) * LANES]
            bui[hf * tiles_half + k] = bi[:, k * LANES:(k + 1) * LANES]

    for k0 in range(0, n_state // LANES, lane_tiles):
        tiles = range(k0, k0 + lane_tiles)
        ar = [jnp.broadcast_to(ar_ref[:, k * LANES:(k + 1) * LANES], (nb, LANES)) for k in tiles]
        ai = [jnp.broadcast_to(ai_ref[:, k * LANES:(k + 1) * LANES], (nb, LANES)) for k in tiles]

        def body(t, carry, tiles=tiles, ar=ar, ai=ai):
            rows = pl.ds(pl.multiple_of(t * nb, nb), nb)
            out = []
            for n, k in enumerate(tiles):
                hr, hi = carry[2 * n], carry[2 * n + 1]
                nr = ar[n] * hr - ai[n] * hi + bur[k, rows, :]
                ni = ar[n] * hi + ai[n] * hr + bui[k, rows, :]
                bur[k, rows, :] = nr
                bui[k, rows, :] = ni
                out += [nr, ni]
            return tuple(out)

        init = []
        for k in tiles:
            init += [hr_s[:, k * LANES:(k + 1) * LANES], hi_s[:, k * LANES:(k + 1) * LANES]]
        fin = lax.fori_loop(0, tc, body, tuple(init), unroll=unroll)
        for n, k in enumerate(tiles):
            hr_s[:, k * LANES:(k + 1) * LANES] = fin[2 * n]
            hi_s[:, k * LANES:(k + 1) * LANES] = fin[2 * n + 1]

    hr_out[...] = hr_s[...]
    hi_out[...] = hi_s[...]

    ys = []
    for hf in range(halves):
        hr_hist = jnp.concatenate([bur[hf * tiles_half + k] for k in range(tiles_half)], axis=1).astype(BF16)
        hi_hist = jnp.concatenate([bui[hf * tiles_half + k] for k in range(tiles_half)], axis=1).astype(BF16)
        ys.append(_dot_nt(hr_hist, ccr_ref[hf]) + _dot_nt(hi_hist, cci_ref[hf]))
    y = jnp.concatenate(ys, axis=1) + d_ref[...] * u
    g = jax.nn.gelu(y)
    out = g * jax.nn.sigmoid(_dot(g.astype(BF16), wglu_ref[...]))
    y_ref[...] = jnp.swapaxes(out.reshape(tc, nb, d_ssm), 0, 1)


def _s5_call(u3, h0r, h0i, ar, ai, bbr, bbi, ccr, cci, d, wglu, *, tc):
    nb, l, d_ssm = u3.shape
    n_state = ar.shape[-1]
    lane_tiles = math.gcd(n_state // LANES, max(1, VREGS * SUBLANES // (16 * nb)))
    body = functools.partial(_s5_body, nb=nb, tc=tc, lane_tiles=lane_tiles, unroll=tc)
    return pl.pallas_call(
        body,
        grid=(l // tc,),
        in_specs=[pl.BlockSpec((nb, tc, d_ssm), lambda c: (0, c, 0)),
                  _const_spec(h0r.shape), _const_spec(h0i.shape),
                  _const_spec(ar.shape), _const_spec(ai.shape),
                  _const_spec(bbr.shape), _const_spec(bbi.shape),
                  _const_spec(ccr.shape), _const_spec(cci.shape),
                  _const_spec(d.shape), _const_spec(wglu.shape)],
        out_specs=[pl.BlockSpec((nb, tc, d_ssm), lambda c: (0, c, 0)),
                   pl.BlockSpec((nb, n_state), lambda c: (0, 0)),
                   pl.BlockSpec((nb, n_state), lambda c: (0, 0))],
        out_shape=[jax.ShapeDtypeStruct((nb, l, d_ssm), F32),
                   jax.ShapeDtypeStruct((nb, n_state), F32),
                   jax.ShapeDtypeStruct((nb, n_state), F32)],
        scratch_shapes=[pltpu.VMEM((n_state // LANES, nb * tc, LANES), F32),
                        pltpu.VMEM((n_state // LANES, nb * tc, LANES), F32),
                        pltpu.VMEM((nb, n_state), F32), pltpu.VMEM((nb, n_state), F32)],
        compiler_params=_params("arbitrary"),
        name="s5_scan_glu",
    )(u3, h0r, h0i, ar, ai, bbr, bbi, ccr, cci, d, wglu)


def _memkv_body(m_ref, g_ref, wk_ref, wv_ref, k_ref, v_ref):
    m = _rms(m_ref[...], g_ref[...]).astype(BF16)
    k_ref[...] = _dot(m, wk_ref[...]).reshape(k_ref.shape)
    v_ref[...] = _dot(m, wv_ref[...]).reshape(v_ref.shape)


def _memkv_call(mem2, g, wk, wv, *, tm, mem_heads):
    t, d = mem2.shape
    hd = wk.shape[1] // mem_heads
    out = pl.BlockSpec((tm * mem_heads, hd), lambda i: (i, 0))
    return pl.pallas_call(
        _memkv_body,
        grid=(t // tm,),
        in_specs=[pl.BlockSpec((tm, d), lambda i: (i, 0)),
                  _const_spec(g.shape), _const_spec(wk.shape), _const_spec(wv.shape)],
        out_specs=[out, out],
        out_shape=[jax.ShapeDtypeStruct((t * mem_heads, hd), F32)] * 2,
        compiler_params=_params("parallel"),
        name="mem_kv",
    )(mem2, g, wk, wv)


def _mix_mem_body(x_ref, a_ref, s_ref, mk_ref, mv_ref, gao_ref, gso_ref, woa_ref, wos_ref, gmp_ref,
                  gmem_ref, wq_ref, wo_ref, gmo_ref, o_ref, *, mem_heads, n_mem, seqs, sub, mem_scale):
    def head_rows(ref, si, hh):
        return ref[pl.ds((si * n_mem) * mem_heads + hh, n_mem, stride=mem_heads), :].astype(BF16)

    mem = [[(head_rows(mk_ref, si, hh), head_rows(mv_ref, si, hh)) for hh in range(mem_heads)]
           for si in range(seqs)]
    tile = x_ref.shape[0] // sub
    rows = tile // seqs
    hd = wq_ref.shape[1] // mem_heads
    tiles = [pl.ds(t * tile, tile) for t in range(sub)]
    a = [_rms(a_ref[r, :], gao_ref[...]).astype(BF16) for r in tiles]
    s = [_rms(s_ref[r, :], gso_ref[...]).astype(BF16) for r in tiles]
    mix = [_dot(a[t], woa_ref[...]) + _dot(s[t], wos_ref[...]) for t in range(sub)]
    x = [x_ref[tiles[t], :] + _rms(mix[t], gmp_ref[...]) for t in range(sub)]
    h = [_rms(x[t], gmem_ref[...]).astype(BF16) for t in range(sub)]
    q = [_dot(h[t], wq_ref[...]) for t in range(sub)]
    problems = [(t, si, hh) for t in range(sub) for si in range(seqs) for hh in range(mem_heads)]
    scores = {}
    for t, si, hh in problems:
        qh = q[t][si * rows:(si + 1) * rows, hh * hd:(hh + 1) * hd].astype(BF16)
        scores[t, si, hh] = _dot_nt(qh, mem[si][hh][0]) * mem_scale
    probs = {}
    for key in problems:
        e = jnp.exp(scores[key] - jnp.max(scores[key], axis=1, keepdims=True))
        probs[key] = (e / jnp.sum(e, axis=1, keepdims=True)).astype(BF16)
    outs = {key: _dot(probs[key], mem[key[1]][key[2]][1]) for key in problems}
    for t in range(sub):
        per_seq = [jnp.concatenate([outs[t, si, hh] for hh in range(mem_heads)], axis=1) for si in range(seqs)]
        o = (jnp.concatenate(per_seq, axis=0) if seqs > 1 else per_seq[0]).astype(BF16)
        o_ref[tiles[t], :] = x[t] + _rms(_dot(o, wo_ref[...]), gmo_ref[...])


def _mix_mem_call(x2, a2, s2, mk, mv, gao, gso, woa, wos, gmp, gmem, wq, wo, gmo, *, tm, rows_per_batch,
                  mem_heads, n_mem):
    t, d = x2.shape
    da, ds = a2.shape[1], s2.shape[1]
    hd = mk.shape[1]
    seqs = max(1, tm // rows_per_batch)
    tiles_per_batch = max(1, rows_per_batch // tm)
    sub = tm // MXU_DIM if (seqs == 1 and tm % ROW_TILE == 0) else 1
    body = functools.partial(_mix_mem_body, mem_heads=mem_heads, n_mem=n_mem, seqs=seqs, sub=sub,
                             mem_scale=hd ** -0.5)
    row = lambda w: pl.BlockSpec((tm, w), lambda i: (i, 0))
    mem = pl.BlockSpec((seqs * n_mem * mem_heads, hd), lambda i: (i // tiles_per_batch, 0))
    consts = [gao, gso, woa, wos, gmp, gmem, wq, wo, gmo]
    return pl.pallas_call(
        body,
        grid=(t // tm,),
        in_specs=[row(d), row(da), row(ds), mem, mem] + [_const_spec(c.shape) for c in consts],
        out_specs=row(d),
        out_shape=jax.ShapeDtypeStruct((t, d), F32),
        compiler_params=_params("parallel"),
        name="mix_out_mem_attn",
    )(x2, a2, s2, mk, mv, *consts)


def _ffn_body(x_ref, cprev_ref, gpre_ref, wg_ref, wu_ref, wd_ref, cw_ref, cb_ref, gpost_ref,
              o_ref, cnew_ref, halo, work, act, *, nseq, lc, fc, halo_rows):
    t = pl.program_id(1)
    d_ff = wg_ref.shape[1]
    tm = nseq * lc
    keep = cprev_ref.shape[1]
    lo = halo_rows - keep

    @pl.when(t == 0)
    def _load_state():
        halo[...] = cprev_ref[...]

    x = x_ref[...]
    h = _rms(x, gpre_ref[...]).astype(BF16)

    for c in range(d_ff // fc):
        cols = slice(c * fc, (c + 1) * fc)
        g = _dot(h, wg_ref[:, cols]).reshape(nseq, lc, fc)
        up = _dot(h, wu_ref[:, cols]).reshape(nseq, lc, fc)
        work[c, :, lo:halo_rows, :] = halo[:, :, cols]
        work[c, :, halo_rows:halo_rows + lc, :] = g
        w = cw_ref[:, cols]
        conv = w[0:1, :] * work[c, :, lo:lo + lc, :]
        for k in range(1, keep):
            conv = conv + w[k:k + 1, :] * work[c, :, lo + k:lo + k + lc, :]
        conv = conv + w[keep:keep + 1, :] * g
        gc = cb_ref[:, cols] + conv
        act[:, cols] = (jax.nn.silu(gc) * up).reshape(tm, fc).astype(BF16)
        tail = work[c, :, lc + lo:lc + halo_rows, :]
        halo[:, :, cols] = tail
        cnew_ref[:, :, cols] = tail

    o_ref[...] = x + _rms(_dot(act[...], wd_ref[...]), gpost_ref[...])


def _ffn_call(x2, cprev, gpre, wg, wu, wd, cw, cb, gpost, *, nseq, lc, fc, n_batch_blocks, tiles_per_batch):
    t, d = x2.shape
    d_ff = wg.shape[1]
    n_chunks = d_ff // fc
    keep = cprev.shape[1]
    halo_rows = SUBLANES
    tm = nseq * lc
    body = functools.partial(_ffn_body, nseq=nseq, lc=lc, fc=fc, halo_rows=halo_rows)
    state = pl.BlockSpec((nseq, keep, d_ff), lambda b, i: (b, 0, 0))
    return pl.pallas_call(
        body,
        grid=(n_batch_blocks, tiles_per_batch),
        in_specs=[pl.BlockSpec((tm, d), lambda b, i: (b * tiles_per_batch + i, 0)),
                  state,
                  _const_spec(gpre.shape), _const_spec(wg.shape), _const_spec(wu.shape),
                  _const_spec(wd.shape), _const_spec(cw.shape), _const_spec(cb.shape),
                  _const_spec(gpost.shape)],
        out_specs=[pl.BlockSpec((tm, d), lambda b, i: (b * tiles_per_batch + i, 0)), state],
        out_shape=[jax.ShapeDtypeStruct((t, d), F32),
                   jax.ShapeDtypeStruct(cprev.shape, F32)],
        scratch_shapes=[pltpu.VMEM((nseq, keep, d_ff), F32),
                        pltpu.VMEM((n_chunks, nseq, halo_rows + lc, fc), F32),
                        pltpu.VMEM((tm, d_ff), BF16)],
        compiler_params=_params("parallel", "arbitrary"),
        name="conv_ffn",
    )(x2, cprev, gpre, wg, wu, wd, cw, cb, gpost)


def _rope_tables(pos, rope_dim, nope_dim, q_scale):
    half = rope_dim // 2
    inv = ROPE_THETA ** (-np.arange(half, dtype=np.float64) * (2.0 / rope_dim))
    ang = np.asarray(pos, np.float64)[:, None] * inv[None, :]
    cos, sin = np.cos(ang), np.sin(ang)
    n = ang.shape[0]
    pad = np.zeros((n, LANES - rope_dim))
    ck = np.concatenate([cos, cos, pad], axis=1)
    sk = np.concatenate([sin, sin, pad], axis=1)
    ones = np.concatenate([np.ones((n, nope_dim)), np.zeros((n, LANES - rope_dim - nope_dim))], axis=1)
    cq = np.concatenate([cos, cos, ones], axis=1) * q_scale
    return np.stack([ck, sk, cq, sk * q_scale]).astype(np.float32)


def _rot_half_cols(w):
    half = w.shape[-1] // 2
    return jnp.concatenate([-w[..., half:], w[..., :half]], axis=-1)


def _pad_last(w, width):
    return jnp.pad(w, [(0, 0)] * (w.ndim - 1) + [(0, width - w.shape[-1])])


def _layer_weights(l, w_in, q_norm, kv_norm, w_uq, w_uk, w_uv, ssm_a_re, ssm_a_im, ssm_log_dt, ssm_b_re,
                   ssm_b_im, ssm_c_re, ssm_c_im, ssm_d, ssm_w_glu, w_out, w_gate, w_up, w_down, ffn_conv_w,
                   ffn_conv_b):
    q_rank = q_norm.shape[-1]
    kv_rank = kv_norm.shape[-1]
    n_heads = w_uq.shape[2]
    nope = w_uk.shape[3]
    rope_dim = w_uq.shape[3] - nope
    dv = w_uv.shape[3]
    d_ssm = ssm_d.shape[-1]
    win = w_in[l]
    o2, o3 = q_rank + kv_rank, q_rank + kv_rank + rope_dim
    w_kr = win[:, o2:o3]
    p = {}
    p["win"] = jnp.concatenate([win[:, :o2], win[:, o3:], _pad_last(w_kr, LANES),
                                _pad_last(_rot_half_cols(w_kr), LANES)], axis=1).astype(BF16)
    uq = w_uq[l]
    q_nope, q_pe = uq[..., :nope], uq[..., nope:]
    wq1 = _pad_last(jnp.concatenate([q_pe, q_nope], axis=-1), LANES).reshape(q_rank, n_heads * LANES)
    wq2 = _pad_last(_rot_half_cols(q_pe), LANES).reshape(q_rank, n_heads * LANES)
    p["wq"] = jnp.concatenate([wq1, wq2], axis=1).astype(BF16)
    uk = w_uk[l]
    wuk_slots = jnp.pad(uk, ((0, 0), (0, 0), (rope_dim, LANES - rope_dim - nope)))
    p["wkv"] = jnp.concatenate([wuk_slots.reshape(kv_rank, n_heads * LANES),
                                _pad_last(w_uv[l], LANES).reshape(kv_rank, n_heads * LANES)], axis=1).astype(BF16)
    p["vone"] = jnp.tile((jnp.arange(LANES) == dv).astype(F32), n_heads).reshape(1, n_heads * LANES)
    p["wukt"] = jnp.transpose(wuk_slots, (1, 2, 0)).astype(BF16)
    uv = jnp.transpose(w_uv[l], (1, 0, 2))
    eye = jnp.eye(n_heads, dtype=F32)
    p["wuvp"] = (uv[:, :, None, :] * eye[:, None, :, None]).reshape(n_heads, kv_rank, n_heads * dv).astype(BF16)
    g, n = ssm_a_re.shape[1:]
    abr, abi, p["bbr"], p["bbi"], p["ccr"], p["cci"] = _s5_disc_call(
        ssm_a_re[l], ssm_a_im[l], ssm_log_dt[l], jnp.transpose(ssm_b_re[l], (0, 2, 1)),
        jnp.transpose(ssm_b_im[l], (0, 2, 1)), ssm_c_re[l], ssm_c_im[l], halves=2)
    p["abr"] = abr.reshape(1, g * n)
    p["abi"] = abi.reshape(1, g * n)
    p["ssm_d"] = ssm_d[l].reshape(1, d_ssm)
    p["wglu"] = ssm_w_glu[l].astype(BF16)
    d_attn = n_heads * dv
    p["woa"] = w_out[l][:d_attn].astype(BF16)
    p["wos"] = w_out[l][d_attn:].astype(BF16)
    d_ff = w_gate.shape[2]
    p["wg"] = w_gate[l].astype(BF16)
    p["wu"] = w_up[l].astype(BF16)
    p["wd"] = w_down[l].astype(BF16)
    conv_w = ffn_conv_w.shape[1]
    p["cw"] = jnp.pad(ffn_conv_w[l], ((0, -conv_w % SUBLANES), (0, 0)))
    p["cb"] = ffn_conv_b[l].reshape(1, d_ff)
    p["dims"] = dict(q_rank=q_rank, kv_rank=kv_rank, n_heads=n_heads, nope=nope, rope_dim=rope_dim, dv=dv,
                     d_ssm=d_ssm, g=g, n=n, d_ff=d_ff, conv_w=conv_w)
    return p


def _row(v):
    return v.reshape(1, -1)


def kernel(x_prompt, x_sample, mem_prompt, cache_kv_latent, cache_k_rope, page_table, state_ssm_re, state_ssm_im, state_ffn_conv, cache_mem_k, cache_mem_v, norm_mix_pre, w_in, q_norm, kv_norm, w_uq, w_uk, w_uv, ssm_a_re, ssm_a_im, ssm_log_dt, ssm_b_re, ssm_b_im, ssm_c_re, ssm_c_im, ssm_d, ssm_w_glu, norm_attn_out, norm_ssm_out, w_out, norm_mix_post, norm_mem_pre, mem_norm, w_q_mem, w_k_mem, w_v_mem, w_o_mem, norm_mem_post, norm_ffn_pre, w_gate, w_up, ffn_conv_w, ffn_conv_b, w_down, norm_ffn_post):
    depth = w_in.shape[0]
    b, l, d_model = x_prompt.shape
    db, ls, _ = x_sample.shape
    n_mem = mem_prompt.shape[1]
    mem_heads = cache_mem_k.shape[3]
    past_len = page_table.shape[1] * cache_kv_latent.shape[2]
    fc = MXU_DIM
    tm = min(ROW_TILE, l)
    tm_wide = min(2 * ROW_TILE, l)
    tq = min(ROW_TILE, l)
    tc = min(LANES, l)
    pages = min(2 * ROW_TILE * SUBLANES // cache_kv_latent.shape[2], page_table.shape[1] // 2)

    xp = x_prompt.reshape(b * l, d_model)
    xs = x_sample.reshape(db * ls, d_model)
    outs = {k: [] for k in ("p_kv", "p_kr", "p_sr", "p_si", "p_cv", "p_mk", "p_mv",
                            "s_kv", "s_kr", "s_sr", "s_si", "s_cv")}
    for li in range(depth):
        p = _layer_weights(li, w_in, q_norm, kv_norm, w_uq, w_uk, w_uv, ssm_a_re, ssm_a_im, ssm_log_dt,
                           ssm_b_re, ssm_b_im, ssm_c_re, ssm_c_im, ssm_d, ssm_w_glu, w_out, w_gate, w_up,
                           w_down, ffn_conv_w, ffn_conv_b)
        dm = p["dims"]
        n_heads, dv, rope_dim, nope = dm["n_heads"], dm["dv"], dm["rope_dim"], dm["nope"]
        g, n, d_ssm = dm["g"], dm["n"], dm["d_ssm"]
        q_scale = (nope + rope_dim) ** -0.5 * LOG2E
        pre_kw = dict(n_heads=n_heads, q_rank=dm["q_rank"], kv_rank=dm["kv_rank"], d_ssm=d_ssm,
                      rope_dim=rope_dim)
        gpre, gq, gkv = _row(norm_mix_pre[li]), _row(q_norm[li]), _row(kv_norm[li])
        mix_consts = (_row(norm_attn_out[li]), _row(norm_ssm_out[li]), p["woa"], p["wos"],
                      _row(norm_mix_post[li]), _row(norm_mem_pre[li]), w_q_mem[li].astype(BF16),
                      w_o_mem[li].astype(BF16), _row(norm_mem_post[li]))
        ffn_consts = (_row(norm_ffn_pre[li]), p["wg"], p["wu"], p["wd"], p["cw"], p["cb"],
                      _row(norm_ffn_post[li]))

        mk, mv = _memkv_call(mem_prompt.reshape(b * n_mem, d_model), _row(mem_norm[li]),
                             w_k_mem[li].astype(BF16), w_v_mem[li].astype(BF16), tm=min(ROW_TILE, b * n_mem),
                             mem_heads=mem_heads)
        tab_p = _rope_tables(np.arange(l), rope_dim, nope, q_scale)
        q, k, v, ckv, kr, u = _pre_call(xp, tab_p, gpre, gq, gkv, p["win"], p["wq"], p["wkv"], p["vone"],
                                        tm=tm_wide, q_dtype=BF16, **pre_kw)
        attn = _attn_call(q.reshape(b, l, -1), k.reshape(b, l, -1), v.reshape(b, l, -1),
                          n_heads=n_heads, dv=dv, tq=tq)
        zeros_state = np.zeros((b, g * n), np.float32)
        ssm, hr, hi = _s5_call(u.reshape(b, l, d_ssm), zeros_state, zeros_state, p["abr"], p["abi"],
                               p["bbr"], p["bbi"], p["ccr"], p["cci"], p["ssm_d"], p["wglu"], tc=tc)
        xp = _mix_mem_call(xp, attn.reshape(b * l, -1), ssm.reshape(b * l, -1),
                           mk, mv, *mix_consts, tm=tm_wide, rows_per_batch=l, mem_heads=mem_heads,
                           n_mem=n_mem)
        conv0 = np.zeros((b, dm["conv_w"] - 1, dm["d_ff"]), np.float32)
        xp, cv = _ffn_call(xp, conv0, *ffn_consts, nseq=1, lc=tm, fc=fc, n_batch_blocks=b,
                           tiles_per_batch=l // tm)
        outs["p_kv"].append(ckv.reshape(b, l, -1))
        outs["p_kr"].append(kr.reshape(b, l, -1))
        outs["p_sr"].append(hr.reshape(b, g, n))
        outs["p_si"].append(hi.reshape(b, g, n))
        outs["p_cv"].append(cv)
        outs["p_mk"].append(mk.reshape(b, n_mem, mem_heads, -1))
        outs["p_mv"].append(mv.reshape(b, n_mem, mem_heads, -1))

        ts = db * ls
        tab_s = np.tile(_rope_tables(past_len + np.arange(ls), rope_dim, nope, q_scale), (1, db, 1))
        q, _, _, ckv, kr, u = _pre_call(xs, tab_s, gpre, gq, gkv, p["win"], p["wq"], p["wkv"], p["vone"],
                                        tm=ts, q_dtype=F32, **pre_kw)
        attn = _dec_attn_call(page_table, q.reshape(db, ls, -1), ckv.reshape(db, ls, -1),
                              kr.reshape(db, ls, -1), p["wukt"], p["wuvp"], cache_kv_latent[li],
                              jnp.swapaxes(cache_k_rope[li], 1, 2), n_heads=n_heads, dv=dv, pages=pages)
        ssm, hr, hi = _s5_call(u.reshape(db, ls, d_ssm), state_ssm_re[li].reshape(db, g * n),
                               state_ssm_im[li].reshape(db, g * n), p["abr"], p["abi"], p["bbr"], p["bbi"],
                               p["ccr"], p["cci"], p["ssm_d"], p["wglu"], tc=ls)
        xs = _mix_mem_call(xs, attn.reshape(ts, -1), ssm.reshape(ts, -1),
                           cache_mem_k[li].reshape(db * n_mem * mem_heads, -1),
                           cache_mem_v[li].reshape(db * n_mem * mem_heads, -1),
                           *mix_consts, tm=min(SUBLANES, db) * ls, rows_per_batch=ls, mem_heads=mem_heads,
                           n_mem=n_mem)
        xs, cv = _ffn_call(xs, state_ffn_conv[li], *ffn_consts, nseq=db, lc=ls, fc=fc, n_batch_blocks=1,
                           tiles_per_batch=1)
        outs["s_kv"].append(ckv.reshape(db, ls, -1))
        outs["s_kr"].append(kr.reshape(db, ls, -1))
        outs["s_sr"].append(hr.reshape(db, g, n))
        outs["s_si"].append(hi.reshape(db, g, n))
        outs["s_cv"].append(cv)

    st = lambda key: jnp.stack(outs[key])
    return (xp.reshape(b, l, d_model), xs.reshape(db, ls, d_model),
            st("p_kv"), st("p_kr"), st("p_sr"), st("p_si"), st("p_cv"), st("p_mk"), st("p_mv"),
            st("s_kv"), st("s_kr"), st("s_sr"), st("s_si"), st("s_cv"))
```
